```python
import jax
import jax.numpy as jnp
from jax import lax
import numpy as np

D_MODEL = 1024
BATCH = 2
SEQ = 16384
DEPTH = 2

D_MIX = D_MODEL
POOL_WIDTH = D_MIX // 4
POOL_WINDOWS = (2, 4, 8, 16)
POOL_GROUP_DIM = POOL_WIDTH // len(POOL_WINDOWS)
NSA_WIDTH = D_MIX - POOL_WIDTH
N_Q_HEADS = 8
HEAD_DIM = NSA_WIDTH // N_Q_HEADS
N_KV_GROUPS = 2
HEADS_PER_GROUP = N_Q_HEADS // N_KV_GROUPS
KV_WIDTH = N_KV_GROUPS * HEAD_DIM
N_BRANCHES = 3
CMP_STRIDE = 16
CMP_BLOCK = 2 * CMP_STRIDE
SLC_BLOCK = 64
N_SELECT = 16
WINDOW = 512
Q_BLOCK = 128
D_FF = 2816
D_IN = POOL_WIDTH + NSA_WIDTH + 6 * KV_WIDTH + N_BRANCHES * N_Q_HEADS
ALPHA = (2.0 * DEPTH) ** 0.25
BETA = (8.0 * DEPTH) ** -0.25
LN_EPS = 1e-5
NEG_BIG = -1e30
SEL_BIG = 1e30

kernel_name = 'hymba_pool_nsa_macaron_deepnorm'


def layer_norm(x, g, b):
    xf = x.astype(jnp.float32)
    mu = jnp.mean(xf, axis=-1, keepdims=True)
    var = jnp.mean(jnp.square(xf - mu), axis=-1, keepdims=True)
    y = (xf - mu) * lax.rsqrt(var + LN_EPS) * g.astype(jnp.float32) + b.astype(jnp.float32)
    return y.astype(x.dtype)


def swiglu(x, w_gate, w_up, w_down):
    return (jax.nn.silu(x @ w_gate) * (x @ w_up)) @ w_down


def masked_softmax(s, mask):
    s = jnp.where(mask, s, NEG_BIG)
    m = jnp.max(s, axis=-1, keepdims=True)
    e = jnp.exp(s - m) * mask
    return e / jnp.maximum(jnp.sum(e, axis=-1, keepdims=True), 1e-30)


def pool_mixer(u, pool_w, pool_scale):
    B, S, _ = u.shape
    uf = u.astype(jnp.float32)
    csum = jnp.pad(jnp.cumsum(uf, axis=1), ((0, 0), (1, 0), (0, 0)))
    t = jnp.arange(S)
    groups = []
    for g, w in enumerate(POOL_WINDOWS):
        sl = slice(g * POOL_GROUP_DIM, (g + 1) * POOL_GROUP_DIM)
        cg = csum[..., sl]
        lo = jnp.maximum(t + 1 - w, 0)
        win_sum = cg[:, 1:] - cg[:, lo]
        cnt = jnp.minimum(t + 1, w).astype(jnp.float32)[None, :, None]
        groups.append(win_sum / cnt - uf[..., sl])
    pooled = jnp.stack(groups, axis=2).astype(u.dtype)
    mixed = jnp.einsum('bsgc,gcd->bsgd', pooled, pool_w)
    return mixed.reshape(B, S, POOL_WIDTH) * pool_scale


def compress(kv, pos, w1, w2):
    B, G, S, dh = kv.shape
    chunks = kv.reshape(B, G, S // CMP_STRIDE, CMP_STRIDE, dh)
    blocks = jnp.concatenate([chunks[:, :, :-1], chunks[:, :, 1:]], axis=3)
    flat = (blocks + pos).reshape(B, G, S // CMP_STRIDE - 1, CMP_BLOCK * dh)
    return jax.nn.gelu(flat @ w1) @ w2


def nsa_mixer(q, kc, vc, k_slc, v_slc, k_win, v_win, gates):
    B, S = q.shape[0], q.shape[1]
    G = N_KV_GROUPS
    NQB = S // Q_BLOCK
    NS = S // SLC_BLOCK
    NC = S // CMP_STRIDE - 1
    n_sel = min(N_SELECT, NS)
    scale = HEAD_DIM ** -0.5
    qb = (q * scale).reshape(B, NQB, Q_BLOCK, G, HEADS_PER_GROUP, HEAD_DIM).transpose(1, 0, 3, 4, 2, 5)
    gb = gates.reshape(B, NQB, Q_BLOCK, G, HEADS_PER_GROUP, N_BRANCHES).transpose(1, 0, 3, 4, 2, 5)
    ks_blocks = k_slc.reshape(B, G, NS, SLC_BLOCK, HEAD_DIM)
    vs_blocks = v_slc.reshape(B, G, NS, SLC_BLOCK, HEAD_DIM)
    kw_pad = jnp.pad(k_win, ((0, 0), (0, 0), (WINDOW, 0), (0, 0)))
    vw_pad = jnp.pad(v_win, ((0, 0), (0, 0), (WINDOW, 0), (0, 0)))
    cmp_end = jnp.arange(NC) * CMP_STRIDE + CMP_BLOCK - 1
    blk = jnp.arange(NS)
    bi = jnp.arange(B)[:, None, None, None]
    gi = jnp.arange(G)[None, :, None, None]
    slc_w = jnp.array([0.5, 1.0, 1.0, 1.0], jnp.float32)

    def one_block(args):
        c, q_c, g_c = args
        t = c * Q_BLOCK + jnp.arange(Q_BLOCK)
        s = jnp.einsum('bghtd,bgnd->bghtn', q_c, kc).astype(jnp.float32)
        p_cmp = masked_softmax(s, cmp_end[None, :] <= t[:, None])
        o_cmp = jnp.einsum('bghtn,bgnd->bghtd', p_cmp.astype(vc.dtype), vc)
        imp = jnp.pad(p_cmp.sum(axis=2), ((0, 0), (0, 0), (0, 0), (1, 1)))
        imp_slc = imp[..., :4 * NS].reshape(B, G, Q_BLOCK, NS, 4) @ slc_w + 0.5 * imp[..., 4::4]
        jt = (t // SLC_BLOCK)[:, None]
        forced = (blk == 0) | (blk == jt) | (blk == jt - 1)
        score = jnp.where(blk <= jt, jnp.where(forced, SEL_BIG, imp_slc), NEG_BIG)
        top_s, idx = lax.top_k(score, n_sel)
        sel_ok = top_s > 0.5 * NEG_BIG
        ks = ks_blocks[bi, gi, idx].reshape(B, G, Q_BLOCK, n_sel * SLC_BLOCK, HEAD_DIM)
        vs = vs_blocks[bi, gi, idx].reshape(B, G, Q_BLOCK, n_sel * SLC_BLOCK, HEAD_DIM)
        pos = (idx[..., None] * SLC_BLOCK + jnp.arange(SLC_BLOCK)).reshape(B, G, Q_BLOCK, n_sel * SLC_BLOCK)
        m_slc = (pos <= t[:, None]) & jnp.repeat(sel_ok, SLC_BLOCK, axis=-1)
        s = jnp.einsum('bghtd,bgtkd->bghtk', q_c, ks).astype(jnp.float32)
        p = masked_softmax(s, m_slc[:, :, None])
        o_slc = jnp.einsum('bghtk,bgtkd->bghtd', p.astype(vs.dtype), vs)
        start = c * Q_BLOCK
        kw = lax.dynamic_slice_in_dim(kw_pad, start, WINDOW + Q_BLOCK, axis=2)
        vw = lax.dynamic_slice_in_dim(vw_pad, start, WINDOW + Q_BLOCK, axis=2)
        s_pos = start - WINDOW + jnp.arange(WINDOW + Q_BLOCK)
        diff = t[:, None] - s_pos[None, :]
        m_win = (s_pos[None, :] >= 0) & (diff >= 0) & (diff < WINDOW)
        s = jnp.einsum('bghtd,bgkd->bghtk', q_c, kw).astype(jnp.float32)
        p = masked_softmax(s, m_win)
        o_win = jnp.einsum('bghtk,bgkd->bghtd', p.astype(vw.dtype), vw)
        o = g_c[..., 0:1] * o_cmp + g_c[..., 1:2] * o_slc + g_c[..., 2:3] * o_win
        return o.transpose(0, 3, 1, 2, 4).reshape(B, Q_BLOCK, NSA_WIDTH)

    out = lax.map(one_block, (jnp.arange(NQB), qb, gb))
    return out.transpose(1, 0, 2, 3).reshape(B, S, NSA_WIDTH)


def hybrid_mixer(x, w_in, b_gate, pool_w, pool_scale, cmp_pos_k, cmp_k_w1, cmp_k_w2,
                 cmp_pos_v, cmp_v_w1, cmp_v_w2, w_out):
    B, S, _ = x.shape
    proj = x @ w_in
    cuts = tuple(int(v) for v in np.cumsum([POOL_WIDTH, NSA_WIDTH] + [KV_WIDTH] * 6))
    u, q, kc, vc, ks, vs, kw, vw, g = jnp.split(proj, cuts, axis=-1)
    y_pool = pool_mixer(u, pool_w, pool_scale)

    def heads(a):
        return a.reshape(B, S, N_KV_GROUPS, HEAD_DIM).transpose(0, 2, 1, 3)

    k_cmp = compress(heads(kc), cmp_pos_k, cmp_k_w1, cmp_k_w2)
    v_cmp = compress(heads(vc), cmp_pos_v, cmp_v_w1, cmp_v_w2)
    gates = jax.nn.sigmoid((g + b_gate).astype(jnp.float32)).astype(x.dtype)
    gates = gates.reshape(B, S, N_Q_HEADS, N_BRANCHES)
    y_nsa = nsa_mixer(q.reshape(B, S, N_Q_HEADS, HEAD_DIM), k_cmp, v_cmp,
                      heads(ks), heads(vs), heads(kw), heads(vw), gates)
    return jnp.concatenate([y_pool, y_nsa], axis=-1) @ w_out


def setup_inputs(seed: int = 0) -> dict:
    key = jax.random.key(seed)
    k = jax.random.split(key, 24)
    L = DEPTH

    def nrm(kk, shape, scale):
        return jax.random.normal(kk, shape, jnp.float32) * scale

    return {
        'x': nrm(k[0], (BATCH, SEQ, D_MODEL), 1.0),
        'ln1_g': 1.0 + nrm(k[1], (L, D_MODEL), 0.02),
        'ln1_b': nrm(k[2], (L, D_MODEL), 0.02),
        'ffn1_w_gate': nrm(k[3], (L, D_MODEL, D_FF), D_MODEL ** -0.5),
        'ffn1_w_up': nrm(k[4], (L, D_MODEL, D_FF), D_MODEL ** -0.5),
        'ffn1_w_down': nrm(k[5], (L, D_FF, D_MODEL), BETA * D_FF ** -0.5),
        'w_in': nrm(k[6], (L, D_MODEL, D_IN), D_MODEL ** -0.5),
        'b_gate': nrm(k[7], (L, N_BRANCHES * N_Q_HEADS), 0.5),
        'pool_w': nrm(k[8], (L, len(POOL_WINDOWS), POOL_GROUP_DIM, POOL_GROUP_DIM), POOL_GROUP_DIM ** -0.5),
        'pool_scale': 1.0 + nrm(k[9], (L, POOL_WIDTH), 0.02),
        'cmp_pos_k': nrm(k[10], (L, CMP_BLOCK, HEAD_DIM), 0.02),
        'cmp_k_w1': nrm(k[11], (L, CMP_BLOCK * HEAD_DIM, HEAD_DIM), (CMP_BLOCK * HEAD_DIM) ** -0.5),
        'cmp_k_w2': nrm(k[12], (L, HEAD_DIM, HEAD_DIM), HEAD_DIM ** -0.5),
        'cmp_pos_v': nrm(k[13], (L, CMP_BLOCK, HEAD_DIM), 0.02),
        'cmp_v_w1': nrm(k[14], (L, CMP_BLOCK * HEAD_DIM, HEAD_DIM), (CMP_BLOCK * HEAD_DIM) ** -0.5),
        'cmp_v_w2': nrm(k[15], (L, HEAD_DIM, HEAD_DIM), HEAD_DIM ** -0.5),
        'w_out': nrm(k[16], (L, D_MIX, D_MODEL), BETA * D_MIX ** -0.5),
        'ln2_g': 1.0 + nrm(k[17], (L, D_MODEL), 0.02),
        'ln2_b': nrm(k[18], (L, D_MODEL), 0.02),
        'ffn2_w_gate': nrm(k[19], (L, D_MODEL, D_FF), D_MODEL ** -0.5),
        'ffn2_w_up': nrm(k[20], (L, D_MODEL, D_FF), D_MODEL ** -0.5),
        'ffn2_w_down': nrm(k[21], (L, D_FF, D_MODEL), BETA * D_FF ** -0.5),
        'ln3_g': 1.0 + nrm(k[22], (L, D_MODEL), 0.02),
        'ln3_b': nrm(k[23], (L, D_MODEL), 0.02),
    }


def reference(x, ln1_g, ln1_b, ffn1_w_gate, ffn1_w_up, ffn1_w_down, w_in, b_gate,
              pool_w, pool_scale, cmp_pos_k, cmp_k_w1, cmp_k_w2, cmp_pos_v, cmp_v_w1,
              cmp_v_w2, w_out, ln2_g, ln2_b, ffn2_w_gate, ffn2_w_up, ffn2_w_down,
              ln3_g, ln3_b):
    for l in range(DEPTH):
        x = layer_norm(ALPHA * x + 0.5 * swiglu(x, ffn1_w_gate[l], ffn1_w_up[l], ffn1_w_down[l]),
                       ln1_g[l], ln1_b[l])
        y = hybrid_mixer(x, w_in[l], b_gate[l], pool_w[l], pool_scale[l], cmp_pos_k[l],
                         cmp_k_w1[l], cmp_k_w2[l], cmp_pos_v[l], cmp_v_w1[l], cmp_v_w2[l], w_out[l])
        x = layer_norm(ALPHA * x + y, ln2_g[l], ln2_b[l])
        x = layer_norm(ALPHA * x + 0.5 * swiglu(x, ffn2_w_gate[l], ffn2_w_up[l], ffn2_w_down[l]),
                       ln3_g[l], ln3_b[l])
    return x
```

```python
import functools

import numpy as np
import jax
import jax.numpy as jnp
from jax import lax
from jax.experimental import pallas as pl
from jax.experimental.pallas import tpu as pltpu

D_MODEL = 1024
DEPTH = 2
POOL_WIDTH = 256
POOL_WINDOWS = (2, 4, 8, 16)
POOL_GROUP_DIM = 64
N_Q_HEADS = 8
HEAD_DIM = 96
N_KV_GROUPS = 2
HEADS_PER_GROUP = 4
N_BRANCHES = 3
CMP_STRIDE = 16
CMP_BLOCK = 32
SLC_BLOCK = 64
SLC_BLOCK_LOG2 = 6
N_SELECT = 16
WINDOW = 512
D_FF = 2816
ALPHA = (2.0 * DEPTH) ** 0.25
LN_EPS = 1e-5
NEG_BIG = -1e30
SEL_BIG = 1e30
QK_SCALE = HEAD_DIM ** -0.5

LANES = 128
HEAD_PAD = LANES
Q_PAD = N_Q_HEADS * HEAD_PAD
KV_PAD = N_KV_GROUPS * HEAD_PAD
GROUP_Q = HEADS_PER_GROUP * HEAD_PAD
CMP_PER_SLC = SLC_BLOCK // CMP_STRIDE
CHUNK_FLAT = CMP_STRIDE * KV_PAD

VMEM_LIMIT = 56 * 1024 * 1024

F32 = jnp.float32
BF16 = jnp.bfloat16

_C_U = 0
_C_Q = _C_U + POOL_WIDTH
_C_KC = _C_Q + Q_PAD
_C_VC = _C_KC + KV_PAD
_C_KS = _C_VC + KV_PAD
_C_VS = _C_KS + KV_PAD
_C_KW = _C_VS + KV_PAD
_C_VW = _C_KW + KV_PAD
_C_G = _C_VW + KV_PAD
_C_END = _C_G + KV_PAD


def _params(*sem):
    return pltpu.CompilerParams(dimension_semantics=sem, vmem_limit_bytes=VMEM_LIMIT)


def _layer_norm(z, g, b):
    mu = jnp.mean(z, axis=-1, keepdims=True)
    zc = z - mu
    var = jnp.mean(zc * zc, axis=-1, keepdims=True)
    return zc * lax.rsqrt(var + LN_EPS) * g + b


def _dot(a, b):
    return jnp.dot(a, b, preferred_element_type=F32)


def _dot_nt(a, b):
    return lax.dot_general(a, b, (((1,), (1,)), ((), ())), preferred_element_type=F32)


FFN_TM = 512
FFN_CHUNK = 1408


def _ffn_ln_kernel(x_ref, wg_ref, wu_ref, wd_ref, g_ref, b_ref, o_ref):
    x = x_ref[...]
    xb = x.astype(BF16)
    acc = None
    for c in range(D_FF // FFN_CHUNK):
        lo = c * FFN_CHUNK
        hg = _dot(xb, wg_ref[:, lo:lo + FFN_CHUNK])
        hu = _dot(xb, wu_ref[:, lo:lo + FFN_CHUNK])
        h = (hg * jax.nn.sigmoid(hg)) * hu
        part = _dot(h.astype(BF16), wd_ref[lo:lo + FFN_CHUNK, :])
        acc = part if acc is None else acc + part
    z = ALPHA * x + 0.5 * acc
    o_ref[...] = _layer_norm(z, g_ref[...], b_ref[...])


def _ffn_ln(x2d, wg, wu, wd, g, b):
    t = x2d.shape[0]
    const = lambda i: (0, 0)
    return pl.pallas_call(
        _ffn_ln_kernel,
        grid=(t // FFN_TM,),
        in_specs=[
            pl.BlockSpec((FFN_TM, D_MODEL), lambda i: (i, 0)),
            pl.BlockSpec((D_MODEL, D_FF), const, pipeline_mode=pl.Buffered(1)),
            pl.BlockSpec((D_MODEL, D_FF), const, pipeline_mode=pl.Buffered(1)),
            pl.BlockSpec((D_FF, D_MODEL), const, pipeline_mode=pl.Buffered(1)),
            pl.BlockSpec((1, D_MODEL), const),
            pl.BlockSpec((1, D_MODEL), const),
        ],
        out_specs=pl.BlockSpec((FFN_TM, D_MODEL), lambda i: (i, 0)),
        out_shape=jax.ShapeDtypeStruct((t, D_MODEL), F32),
        compiler_params=_params("parallel"),
        name="ffn_ln",
    )(x2d, wg.astype(BF16), wu.astype(BF16), wd.astype(BF16), g.reshape(1, -1), b.reshape(1, -1))


PROJ_TM = 512


def _proj_kernel(x_ref, w_ref, bg_ref, u_ref, q_ref, kc_ref, vc_ref, ks_ref, vs_ref,
                 kw_ref, vw_ref, gt_ref):
    xb = x_ref[0].astype(BF16)

    def mm(lo, n):
        return _dot(xb, w_ref[:, lo:lo + n])

    u_ref[0] = mm(_C_U, POOL_WIDTH)
    q_ref[0] = (mm(_C_Q, Q_PAD) * QK_SCALE).astype(BF16)
    kc_ref[0] = mm(_C_KC, KV_PAD)
    vc_ref[0] = mm(_C_VC, KV_PAD)
    ks_ref[0] = mm(_C_KS, KV_PAD).astype(BF16)
    vs_ref[0] = mm(_C_VS, KV_PAD).astype(BF16)
    kw_ref[0] = mm(_C_KW, KV_PAD).astype(BF16)
    vw_ref[0] = mm(_C_VW, KV_PAD).astype(BF16)
    gt_ref[0] = jax.nn.sigmoid(mm(_C_G, KV_PAD) + bg_ref[...])


def _pad_heads(w, n_heads):
    lead = w.shape[:-1]
    w = w.reshape(lead + (n_heads, HEAD_DIM))
    w = jnp.pad(w, [(0, 0)] * len(lead) + [(0, 0), (0, HEAD_PAD - HEAD_DIM)])
    return w.reshape(lead + (n_heads * HEAD_PAD,))


def _pad_gate_cols(w):
    lead = w.shape[:-1]
    per_group = HEADS_PER_GROUP * N_BRANCHES
    w = w.reshape(lead + (N_KV_GROUPS, per_group))
    w = jnp.pad(w, [(0, 0)] * len(lead) + [(0, 0), (0, HEAD_PAD - per_group)])
    return w.reshape(lead + (KV_PAD,))


def _proj(x, w_in, b_gate):
    bsz, s, _ = x.shape
    cuts = np.cumsum([POOL_WIDTH, N_Q_HEADS * HEAD_DIM] + [N_KV_GROUPS * HEAD_DIM] * 6)
    parts = jnp.split(w_in, [int(c) for c in cuts], axis=-1)
    cols = [parts[0], _pad_heads(parts[1], N_Q_HEADS)]
    cols += [_pad_heads(p, N_KV_GROUPS) for p in parts[2:8]]
    cols.append(_pad_gate_cols(parts[8]))
    w = jnp.concatenate(cols, axis=-1).astype(BF16)
    bg = _pad_gate_cols(b_gate).reshape(1, KV_PAD)

    def tile(width):
        return pl.BlockSpec((1, PROJ_TM, width), lambda b, i: (b, i, 0))

    def out(width, dtype):
        return jax.ShapeDtypeStruct((bsz, s, width), dtype)

    return pl.pallas_call(
        _proj_kernel,
        grid=(bsz, s // PROJ_TM),
        in_specs=[
            tile(D_MODEL),
            pl.BlockSpec((D_MODEL, _C_END), lambda b, i: (0, 0), pipeline_mode=pl.Buffered(1)),
            pl.BlockSpec((1, KV_PAD), lambda b, i: (0, 0)),
        ],
        out_specs=[tile(POOL_WIDTH), tile(Q_PAD), tile(KV_PAD), tile(KV_PAD), tile(KV_PAD),
                   tile(KV_PAD), tile(KV_PAD), tile(KV_PAD), tile(KV_PAD)],
        out_shape=[out(POOL_WIDTH, F32), out(Q_PAD, BF16), out(KV_PAD, F32), out(KV_PAD, F32),
                   out(KV_PAD, BF16), out(KV_PAD, BF16), out(KV_PAD, BF16), out(KV_PAD, BF16),
                   out(KV_PAD, F32)],
        compiler_params=_params("parallel", "parallel"),
        name="in_proj",
    )(x, w, bg)


def _gelu_tanh(x):
    c = np.float32(np.sqrt(2.0 / np.pi))
    return x * (0.5 * (1.0 + jnp.tanh(c * (x + 0.044715 * (x * x * x)))))


def _compress_kernel(x_ref, plo_ref, phi_ref, wlo_ref, whi_ref, w2_ref, o_ref, a_scr, b_scr):
    r = pl.program_id(1)
    nsb = x_ref.shape[1]
    xr = x_ref[0]
    a_scr[r] = _dot((xr + plo_ref[...]).astype(BF16), wlo_ref[...])
    b_scr[r, 0:nsb, :] = _dot((xr + phi_ref[...]).astype(BF16), whi_ref[...])

    @pl.when(r == 0)
    def _():
        b_scr[0, nsb:nsb + 8, :] = jnp.zeros((8, KV_PAD), F32)

    @pl.when(r == CMP_PER_SLC - 1)
    def _():
        for rr in range(CMP_PER_SLC):
            if rr < CMP_PER_SLC - 1:
                h = a_scr[rr] + b_scr[rr + 1, 0:nsb, :]
            else:
                h = a_scr[rr] + b_scr[0, 1:nsb + 1, :]
            y = _dot(_gelu_tanh(h).astype(BF16), w2_ref[...])
            o_ref[0, rr * nsb:(rr + 1) * nsb, :] = y.astype(BF16)


def _compress_weights(pos, w1, w2):
    eye_g = jnp.eye(N_KV_GROUPS, dtype=F32)
    w1r = w1.reshape(CMP_BLOCK, HEAD_DIM, HEAD_DIM)
    w1r = jnp.pad(w1r, ((0, 0), (0, HEAD_PAD - HEAD_DIM), (0, HEAD_PAD - HEAD_DIM)))
    w1c = w1r[:, None, :, None, :] * eye_g[None, :, None, :, None]
    w1c = w1c.reshape(CMP_BLOCK, KV_PAD, KV_PAD)
    wlo = w1c[:CMP_STRIDE].reshape(CHUNK_FLAT, KV_PAD).astype(BF16)
    whi = w1c[CMP_STRIDE:].reshape(CHUNK_FLAT, KV_PAD).astype(BF16)
    posp = jnp.pad(pos, ((0, 0), (0, HEAD_PAD - HEAD_DIM)))
    posp = jnp.tile(posp[:, None, :], (1, N_KV_GROUPS, 1))
    plo = posp[:CMP_STRIDE].reshape(1, CHUNK_FLAT)
    phi = posp[CMP_STRIDE:].reshape(1, CHUNK_FLAT)
    w2p = jnp.pad(w2, ((0, HEAD_PAD - HEAD_DIM), (0, HEAD_PAD - HEAD_DIM)))
    w2c = (w2p[None, :, None, :] * eye_g[:, None, :, None]).reshape(KV_PAD, KV_PAD).astype(BF16)
    return plo, phi, wlo, whi, w2c


def _compress(kv, pos, w1, w2):
    bsz, s, _ = kv.shape
    nsb = s // SLC_BLOCK
    plo, phi, wlo, whi, w2c = _compress_weights(pos, w1, w2)
    x = kv.reshape(bsz, nsb, CMP_PER_SLC * CHUNK_FLAT)
    const = lambda b, r: (0, 0)
    return pl.pallas_call(
        _compress_kernel,
        grid=(bsz, CMP_PER_SLC),
        in_specs=[
            pl.BlockSpec((1, nsb, CHUNK_FLAT), lambda b, r: (b, 0, r)),
            pl.BlockSpec((1, CHUNK_FLAT), const),
            pl.BlockSpec((1, CHUNK_FLAT), const),
            pl.BlockSpec((CHUNK_FLAT, KV_PAD), const),
            pl.BlockSpec((CHUNK_FLAT, KV_PAD), const),
            pl.BlockSpec((KV_PAD, KV_PAD), const),
        ],
        out_specs=pl.BlockSpec((1, CMP_PER_SLC * nsb, KV_PAD), lambda b, r: (b, 0, 0)),
        out_shape=jax.ShapeDtypeStruct((bsz, CMP_PER_SLC * nsb, KV_PAD), BF16),
        scratch_shapes=[pltpu.VMEM((CMP_PER_SLC, nsb, KV_PAD), F32),
                        pltpu.VMEM((CMP_PER_SLC, nsb + 8, KV_PAD), F32)],
        compiler_params=_params("parallel", "arbitrary"),
        name="compress",
    )(x, plo, phi, wlo, whi, w2c)


TQ = 128
ROWS = HEADS_PER_GROUP * TQ


def _load_q(q_ref):
    return jnp.concatenate(
        [q_ref[0, :, h * HEAD_PAD:(h + 1) * HEAD_PAD] for h in range(HEADS_PER_GROUP)], axis=0)


def _store_gated(o_ref, g_ref, o, branch):
    for h in range(HEADS_PER_GROUP):
        c = h * N_BRANCHES + branch
        gate = g_ref[0, :, c:c + 1]
        o_ref[0, :, h * HEAD_PAD:(h + 1) * HEAD_PAD] = o[h * TQ:(h + 1) * TQ, :] * gate


def _attn_specs(s):
    q_spec = pl.BlockSpec((1, TQ, GROUP_Q), lambda b, g, i: (b, i, g))
    g_spec = pl.BlockSpec((1, TQ, HEAD_PAD), lambda b, g, i: (b, i, g))
    o_spec = pl.BlockSpec((1, TQ, GROUP_Q), lambda b, g, i: (b, i, g))
    return q_spec, g_spec, o_spec


def _cmp_kernel(q_ref, kc_ref, vc_ref, g_ref, o_ref, sel_ref):
    ncp = kc_ref.shape[1]
    nsb = ncp // CMP_PER_SLC
    t0 = pl.program_id(2) * TQ
    q4 = _load_q(q_ref)
    s = _dot_nt(q4, kc_ref[0]).reshape(HEADS_PER_GROUP, TQ, ncp)

    col = lax.broadcasted_iota(jnp.int32, (TQ, ncp), 1)
    trow = t0 + lax.broadcasted_iota(jnp.int32, (TQ, ncp), 0)
    nsb_log2 = nsb.bit_length() - 1
    cmp_end = ((col & (nsb - 1)) * SLC_BLOCK + (col >> nsb_log2) * CMP_STRIDE
               + (CMP_BLOCK - 1))
    mask = (cmp_end <= trow)[None]
    s = jnp.where(mask, s, NEG_BIG)
    m = jnp.max(s, axis=-1, keepdims=True)
    e = jnp.where(mask, jnp.exp(s - m), 0.0)
    p = e / jnp.maximum(jnp.sum(e, axis=-1, keepdims=True), 1e-30)

    o = _dot(p.reshape(ROWS, ncp).astype(BF16), vc_ref[0])
    _store_gated(o_ref, g_ref, o, 0)

    imp = jnp.sum(p, axis=0)
    p0, p1, p2, p3 = (imp[:, r * nsb:(r + 1) * nsb] for r in range(CMP_PER_SLC))
    blk = lax.broadcasted_iota(jnp.int32, (TQ, nsb), 1)
    p3_prev = jnp.where(blk == 0, 0.0, pltpu.roll(p3, 1, axis=1))
    imp_slc = 0.5 * p3_prev + p0 + p1 + p2 + 0.5 * p3

    jt = (t0 + lax.broadcasted_iota(jnp.int32, (TQ, nsb), 0)) >> SLC_BLOCK_LOG2
    forced = (blk == 0) | (blk == jt) | (blk == jt - 1)
    valid = blk <= jt
    score = jnp.where(valid, jnp.where(forced, SEL_BIG, imp_slc), NEG_BIG)

    score_t = score.T
    row = lax.broadcasted_iota(jnp.int32, (nsb, TQ), 0)

    def pick(_, sc):
        mx = jnp.max(sc, axis=0, keepdims=True)
        first = jnp.min(jnp.where(sc == mx, row, nsb), axis=0, keepdims=True)
        return jnp.where(row == first, -jnp.inf, sc)

    picked = lax.fori_loop(0, min(N_SELECT, nsb), pick, score_t) == -jnp.inf
    sel = jnp.where(picked, 1.0, 0.0).T
    sel_ref[0, 0] = jnp.where(valid, sel, 0.0).astype(BF16)


def _cmp_attention(q, kcmp, vcmp, gates):
    bsz, s, _ = q.shape
    nsb = s // SLC_BLOCK
    ncp = CMP_PER_SLC * nsb
    q_spec, g_spec, o_spec = _attn_specs(s)
    kv_spec = pl.BlockSpec((1, ncp, HEAD_PAD), lambda b, g, i: (b, 0, g))
    return pl.pallas_call(
        _cmp_kernel,
        grid=(bsz, N_KV_GROUPS, s // TQ),
        in_specs=[q_spec, kv_spec, kv_spec, g_spec],
        out_specs=[o_spec, pl.BlockSpec((1, 1, TQ, nsb), lambda b, g, i: (b, g, i, 0))],
        out_shape=[jax.ShapeDtypeStruct((bsz, s, Q_PAD), F32),
                   jax.ShapeDtypeStruct((bsz, N_KV_GROUPS, s, nsb), BF16)],
        compiler_params=_params("parallel", "parallel", "arbitrary"),
        name="cmp_attn_topk",
    )(q, kcmp, vcmp, gates)


SLC_TK = 512


def _slc_kernel(q_ref, k_ref, v_ref, sel_ref, g_ref, o_ref):
    nsb = sel_ref.shape[3]
    qi = pl.program_id(2)
    t0 = qi * TQ
    q4 = _load_q(q_ref)
    sel = sel_ref[0, 0]
    trow = t0 + lax.broadcasted_iota(jnp.int32, (TQ, SLC_TK), 0)
    kcol = lax.broadcasted_iota(jnp.int32, (TQ, SLC_TK), 1)
    rel = (lax.broadcasted_iota(jnp.int32, (nsb, SLC_TK), 0)
           - (lax.broadcasted_iota(jnp.int32, (nsb, SLC_TK), 1) >> SLC_BLOCK_LOG2))

    def step(kt, carry):
        m, l, acc = carry
        k0 = pl.multiple_of(kt * SLC_TK, SLC_TK)
        k = k_ref[0, pl.ds(k0, SLC_TK), :]
        v = v_ref[0, pl.ds(k0, SLC_TK), :]
        s = _dot_nt(q4, k).reshape(HEADS_PER_GROUP, TQ, SLC_TK)
        expand = jnp.where(rel == kt * (SLC_TK // SLC_BLOCK), 1.0, 0.0).astype(BF16)
        chosen = _dot(sel, expand)
        mask = ((chosen > 0.5) & (k0 + kcol <= trow))[None]
        s = jnp.where(mask, s, NEG_BIG)
        m_new = jnp.maximum(m, jnp.max(s, axis=-1, keepdims=True))
        e = jnp.where(mask, jnp.exp(s - m_new), 0.0)
        scale = jnp.exp(m - m_new)
        l = scale * l + jnp.sum(e, axis=-1, keepdims=True)
        pv = _dot(e.reshape(ROWS, SLC_TK).astype(BF16), v)
        acc = scale.reshape(ROWS, 1) * acc + pv
        return m_new, l, acc

    n_kt = (t0 + TQ + SLC_TK - 1) // SLC_TK
    init = (jnp.full((HEADS_PER_GROUP, TQ, 1), NEG_BIG, F32),
            jnp.zeros((HEADS_PER_GROUP, TQ, 1), F32),
            jnp.zeros((ROWS, HEAD_PAD), F32))
    _, l, acc = lax.fori_loop(0, n_kt, step, init)
    o = acc / jnp.maximum(l, 1e-30).reshape(ROWS, 1)
    _store_gated(o_ref, g_ref, o, 1)


def _slc_attention(q, ks, vs, sel, gates):
    bsz, s, _ = q.shape
    nsb = s // SLC_BLOCK
    q_spec, g_spec, o_spec = _attn_specs(s)
    kv_spec = pl.BlockSpec((1, s, HEAD_PAD), lambda b, g, i: (b, 0, g))
    return pl.pallas_call(
        _slc_kernel,
        grid=(bsz, N_KV_GROUPS, s // TQ),
        in_specs=[q_spec, kv_spec, kv_spec,
                  pl.BlockSpec((1, 1, TQ, nsb), lambda b, g, i: (b, g, i, 0)), g_spec],
        out_specs=o_spec,
        out_shape=jax.ShapeDtypeStruct((bsz, s, Q_PAD), F32),
        compiler_params=_params("parallel", "parallel", "arbitrary"),
        name="slc_attn",
    )(q, ks, vs, sel, gates)


WIN_KEYS = WINDOW + TQ


def _win_kernel(q_ref, k_ref, v_ref, g_ref, o_ref):
    t0 = pl.program_id(2) * TQ
    k0 = pl.multiple_of(jnp.maximum(t0 - WINDOW, 0), TQ)
    q4 = _load_q(q_ref)
    k = k_ref[0, pl.ds(k0, WIN_KEYS), :]
    v = v_ref[0, pl.ds(k0, WIN_KEYS), :]
    s = _dot_nt(q4, k).reshape(HEADS_PER_GROUP, TQ, WIN_KEYS)
    diff = ((t0 + lax.broadcasted_iota(jnp.int32, (TQ, WIN_KEYS), 0))
            - (k0 + lax.broadcasted_iota(jnp.int32, (TQ, WIN_KEYS), 1)))
    mask = ((diff >= 0) & (diff < WINDOW))[None]
    s = jnp.where(mask, s, NEG_BIG)
    m = jnp.max(s, axis=-1, keepdims=True)
    e = jnp.where(mask, jnp.exp(s - m), 0.0)
    l = jnp.sum(e, axis=-1, keepdims=True)
    o = _dot(e.reshape(ROWS, WIN_KEYS).astype(BF16), v)
    o = o / jnp.maximum(l, 1e-30).reshape(ROWS, 1)
    _store_gated(o_ref, g_ref, o, 2)


def _win_attention(q, kw, vw, gates):
    bsz, s, _ = q.shape
    q_spec, g_spec, o_spec = _attn_specs(s)
    kv_spec = pl.BlockSpec((1, s, HEAD_PAD), lambda b, g, i: (b, 0, g))
    return pl.pallas_call(
        _win_kernel,
        grid=(bsz, N_KV_GROUPS, s // TQ),
        in_specs=[q_spec, kv_spec, kv_spec, g_spec],
        out_specs=o_spec,
        out_shape=jax.ShapeDtypeStruct((bsz, s, Q_PAD), F32),
        compiler_params=_params("parallel", "parallel", "arbitrary"),
        name="win_attn",
    )(q, kw, vw, gates)


MIX_TM = 512
POOL_HALO = 16


def _mix_out_kernel(x_ref, u_ref, halo_ref, oc_ref, os_ref, ow_ref, pw_ref, ps_ref,
                    wo_ref, g_ref, b_ref, o_ref, ext_scr):
    i = pl.program_id(1)
    u = u_ref[0]
    halo = jnp.where(i == 0, 0.0, halo_ref[0])
    ext_scr[0:POOL_HALO, :] = halo
    ext_scr[POOL_HALO:POOL_HALO + MIX_TM, :] = u

    lane = lax.broadcasted_iota(jnp.int32, (MIX_TM, POOL_WIDTH), 1)
    tpos = i * MIX_TM + lax.broadcasted_iota(jnp.int32, (MIX_TM, POOL_WIDTH), 0)
    grp = lane >> (POOL_GROUP_DIM.bit_length() - 1)
    run = u
    win_sum = jnp.zeros_like(u)
    cnt = jnp.zeros_like(u)
    done = 1
    for gidx, w in enumerate(POOL_WINDOWS):
        for kback in range(done, w):
            run = run + ext_scr[POOL_HALO - kback:POOL_HALO - kback + MIX_TM, :]
        done = w
        win_sum = jnp.where(grp == gidx, run, win_sum)
        cnt = jnp.where(grp == gidx, jnp.minimum(tpos + 1, w).astype(F32), cnt)
    pooled = win_sum / cnt - u
    mixed = _dot(pooled.astype(BF16), pw_ref[...]) * ps_ref[...]

    y_nsa = oc_ref[0] + os_ref[0] + ow_ref[0]
    y = (_dot(mixed.astype(BF16), wo_ref[0:POOL_WIDTH, :])
         + _dot(y_nsa.astype(BF16), wo_ref[POOL_WIDTH:, :]))
    z = ALPHA * x_ref[0] + y
    o_ref[0] = _layer_norm(z, g_ref[...], b_ref[...])


def _mix_out(x, u, o_cmp, o_slc, o_win, pool_w, pool_scale, w_out, g, b):
    bsz, s, _ = x.shape
    n_grp = len(POOL_WINDOWS)
    eye = jnp.eye(n_grp, dtype=F32)
    pw = (pool_w[:, :, None, :] * eye[:, None, :, None]).reshape(POOL_WIDTH, POOL_WIDTH).astype(BF16)
    wo_nsa = w_out[POOL_WIDTH:].reshape(N_Q_HEADS, HEAD_DIM, D_MODEL)
    wo_nsa = jnp.pad(wo_nsa, ((0, 0), (0, HEAD_PAD - HEAD_DIM), (0, 0))).reshape(Q_PAD, D_MODEL)
    wo = jnp.concatenate([w_out[:POOL_WIDTH], wo_nsa], axis=0).astype(BF16)

    def tile(width):
        return pl.BlockSpec((1, MIX_TM, width), lambda bb, i: (bb, i, 0))

    halo_blocks = MIX_TM // POOL_HALO
    const = lambda bb, i: (0, 0)
    return pl.pallas_call(
        _mix_out_kernel,
        grid=(bsz, s // MIX_TM),
        in_specs=[
            tile(D_MODEL), tile(POOL_WIDTH),
            pl.BlockSpec((1, POOL_HALO, POOL_WIDTH),
                         lambda bb, i: (bb, jnp.maximum(i * halo_blocks - 1, 0), 0)),
            tile(Q_PAD), tile(Q_PAD), tile(Q_PAD),
            pl.BlockSpec((POOL_WIDTH, POOL_WIDTH), const),
            pl.BlockSpec((1, POOL_WIDTH), const),
            pl.BlockSpec((POOL_WIDTH + Q_PAD, D_MODEL), const),
            pl.BlockSpec((1, D_MODEL), const),
            pl.BlockSpec((1, D_MODEL), const),
        ],
        out_specs=tile(D_MODEL),
        out_shape=jax.ShapeDtypeStruct((bsz, s, D_MODEL), F32),
        scratch_shapes=[pltpu.VMEM((POOL_HALO + MIX_TM, POOL_WIDTH), F32)],
        compiler_params=_params("parallel", "arbitrary"),
        name="mix_out_ln",
    )(x, u, u, o_cmp, o_slc, o_win, pw, pool_scale.reshape(1, -1), wo,
      g.reshape(1, -1), b.reshape(1, -1))


def kernel(x, ln1_g, ln1_b, ffn1_w_gate, ffn1_w_up, ffn1_w_down, w_in, b_gate, pool_w, pool_scale, cmp_pos_k, cmp_k_w1, cmp_k_w2, cmp_pos_v, cmp_v_w1, cmp_v_w2, w_out, ln2_g, ln2_b, ffn2_w_gate, ffn2_w_up, ffn2_w_down, ln3_g, ln3_b):
    bsz, s, d = x.shape
    assert d == D_MODEL and s % max(SLC_TK, FFN_TM, MIX_TM, PROJ_TM) == 0 and s >= WIN_KEYS
    for l in range(DEPTH):
        x = _ffn_ln(x.reshape(bsz * s, d), ffn1_w_gate[l], ffn1_w_up[l], ffn1_w_down[l],
                    ln1_g[l], ln1_b[l]).reshape(bsz, s, d)
        u, q, kc, vc, ks, vs, kw, vw, gates = _proj(x, w_in[l], b_gate[l])
        kcmp = _compress(kc, cmp_pos_k[l], cmp_k_w1[l], cmp_k_w2[l])
        vcmp = _compress(vc, cmp_pos_v[l], cmp_v_w1[l], cmp_v_w2[l])
        o_cmp, sel = _cmp_attention(q, kcmp, vcmp, gates)
        o_slc = _slc_attention(q, ks, vs, sel, gates)
        o_win = _win_attention(q, kw, vw, gates)
        x = _mix_out(x, u, o_cmp, o_slc, o_win, pool_w[l], pool_scale[l], w_out[l],
                     ln2_g[l], ln2_b[l])
        x = _ffn_ln(x.reshape(bsz * s, d), ffn2_w_gate[l], ffn2_w_up[l], ffn2_w_down[l],
                    ln3_g[l], ln3_b[l]).reshape(bsz, s, d)
    return x
```

```python
import functools

import numpy as np
import jax
import jax.numpy as jnp
from jax import lax
from jax.experimental import pallas as pl
from jax.experimental.pallas import tpu as pltpu

D_MODEL = 1024
DEPTH = 2
POOL_WIDTH = 256
POOL_WINDOWS = (2, 4, 8, 16)
POOL_GROUP_DIM = 64
N_Q_HEADS = 8
HEAD_DIM = 96
N_KV_GROUPS = 2
HEADS_PER_GROUP = 4
N_BRANCHES = 3
CMP_STRIDE = 16
CMP_BLOCK = 32
SLC_BLOCK = 64
SLC_BLOCK_LOG2 = 6
N_SELECT = 16
WINDOW = 512
D_FF = 2816
ALPHA = (2.0 * DEPTH) ** 0.25
LN_EPS = 1e-5
NEG_BIG = -1e30
SEL_BIG = 1e30
QK_SCALE = HEAD_DIM ** -0.5

LANES = 128
HEAD_PAD = LANES
Q_PAD = N_Q_HEADS * HEAD_PAD
KV_PAD = N_KV_GROUPS * HEAD_PAD
GROUP_Q = HEADS_PER_GROUP * HEAD_PAD
CMP_PER_SLC = SLC_BLOCK // CMP_STRIDE
CHUNK_FLAT = CMP_STRIDE * KV_PAD

VMEM_LIMIT = 56 * 1024 * 1024

F32 = jnp.float32
BF16 = jnp.bfloat16

_C_U = 0
_C_Q = _C_U + POOL_WIDTH
_C_KC = _C_Q + Q_PAD
_C_VC = _C_KC + KV_PAD
_C_KS = _C_VC + KV_PAD
_C_KW = _C_KS + KV_PAD
_C_G = _C_KW + KV_PAD
_C_END = _C_G + KV_PAD


def _params(*sem):
    return pltpu.CompilerParams(dimension_semantics=sem, vmem_limit_bytes=VMEM_LIMIT)


def _layer_norm(z, g, b):
    mu = jnp.mean(z, axis=-1, keepdims=True)
    zc = z - mu
    var = jnp.mean(zc * zc, axis=-1, keepdims=True)
    return zc * lax.rsqrt(var + LN_EPS) * g + b


def _dot(a, b):
    return jnp.dot(a, b, preferred_element_type=F32)


def _dot_nt(a, b):
    return lax.dot_general(a, b, (((1,), (1,)), ((), ())), preferred_element_type=F32)


FFN_TM = 512
FFN_CHUNK = 1408


def _ffn_ln_kernel(x_ref, wg_ref, wu_ref, wd_ref, g_ref, b_ref, o_ref):
    x = x_ref[...]
    xb = x.astype(BF16)
    acc = None
    for c in range(D_FF // FFN_CHUNK):
        lo = c * FFN_CHUNK
        hg = _dot(xb, wg_ref[:, lo:lo + FFN_CHUNK])
        hu = _dot(xb, wu_ref[:, lo:lo + FFN_CHUNK])
        h = (hg * jax.nn.sigmoid(hg)) * hu
        part = _dot(h.astype(BF16), wd_ref[lo:lo + FFN_CHUNK, :])
        acc = part if acc is None else acc + part
    z = ALPHA * x + 0.5 * acc
    o_ref[...] = _layer_norm(z, g_ref[...], b_ref[...])


def _ffn_ln(x2d, wg, wu, wd, g, b):
    t = x2d.shape[0]
    const = lambda i: (0, 0)
    return pl.pallas_call(
        _ffn_ln_kernel,
        grid=(t // FFN_TM,),
        in_specs=[
            pl.BlockSpec((FFN_TM, D_MODEL), lambda i: (i, 0)),
            pl.BlockSpec((D_MODEL, D_FF), const, pipeline_mode=pl.Buffered(1)),
            pl.BlockSpec((D_MODEL, D_FF), const, pipeline_mode=pl.Buffered(1)),
            pl.BlockSpec((D_FF, D_MODEL), const, pipeline_mode=pl.Buffered(1)),
            pl.BlockSpec((1, D_MODEL), const),
            pl.BlockSpec((1, D_MODEL), const),
        ],
        out_specs=pl.BlockSpec((FFN_TM, D_MODEL), lambda i: (i, 0)),
        out_shape=jax.ShapeDtypeStruct((t, D_MODEL), F32),
        compiler_params=_params("parallel"),
        name="ffn_ln",
    )(x2d, wg.astype(BF16), wu.astype(BF16), wd.astype(BF16), g.reshape(1, -1), b.reshape(1, -1))


PROJ_TM = 512
MASK_LANES = HEAD_PAD - HEAD_DIM


def _proj_kernel(x_ref, w_ref, wvt_ref, bg_ref, u_ref, q_ref, kc_ref, vc_ref, ks_ref, kw_ref,
                 gt_ref, vst_ref, vwt_ref):
    xb = x_ref[0].astype(BF16)

    def mm(lo, n):
        return _dot(xb, w_ref[:, lo:lo + n])

    u_ref[0] = mm(_C_U, POOL_WIDTH)
    q_ref[0] = (mm(_C_Q, Q_PAD) * QK_SCALE).astype(BF16)
    kc_ref[0] = mm(_C_KC, KV_PAD)
    vc_ref[0] = mm(_C_VC, KV_PAD)
    tpos = pl.program_id(1) * PROJ_TM + lax.broadcasted_iota(jnp.int32, (PROJ_TM, KV_PAD), 0)
    lane = lax.broadcasted_iota(jnp.int32, (PROJ_TM, KV_PAD), 1) & (HEAD_PAD - 1)
    hot = lane == HEAD_DIM + ((tpos >> SLC_BLOCK_LOG2) & (MASK_LANES - 1))
    ks_ref[0] = jnp.where(hot, 1.0, mm(_C_KS, KV_PAD)).astype(BF16)
    kw_ref[0] = mm(_C_KW, KV_PAD).astype(BF16)
    gt_ref[0] = jax.nn.sigmoid(mm(_C_G, KV_PAD) + bg_ref[...])
    vst_ref[0] = _dot_nt(wvt_ref[0:KV_PAD, :], xb).astype(BF16)
    vwt_ref[0] = _dot_nt(wvt_ref[KV_PAD:2 * KV_PAD, :], xb).astype(BF16)


def _pad_heads(w, n_heads):
    lead = w.shape[:-1]
    w = w.reshape(lead + (n_heads, HEAD_DIM))
    w = jnp.pad(w, [(0, 0)] * len(lead) + [(0, 0), (0, HEAD_PAD - HEAD_DIM)])
    return w.reshape(lead + (n_heads * HEAD_PAD,))


def _pad_gate_cols(w):
    lead = w.shape[:-1]
    per_group = HEADS_PER_GROUP * N_BRANCHES
    w = w.reshape(lead + (N_KV_GROUPS, per_group))
    w = jnp.pad(w, [(0, 0)] * len(lead) + [(0, 0), (0, HEAD_PAD - per_group)])
    return w.reshape(lead + (KV_PAD,))


def _proj(x, w_in, b_gate):
    bsz, s, _ = x.shape
    cuts = np.cumsum([POOL_WIDTH, N_Q_HEADS * HEAD_DIM] + [N_KV_GROUPS * HEAD_DIM] * 6)
    parts = jnp.split(w_in, [int(c) for c in cuts], axis=-1)
    cols = [parts[0], _pad_heads(parts[1], N_Q_HEADS)]
    cols += [_pad_heads(parts[i], N_KV_GROUPS) for i in (2, 3, 4, 6)]
    cols.append(_pad_gate_cols(parts[8]))
    w = jnp.concatenate(cols, axis=-1).astype(BF16)
    wvt = jnp.concatenate([_pad_heads(parts[5], N_KV_GROUPS),
                           _pad_heads(parts[7], N_KV_GROUPS)], axis=-1).T.astype(BF16)
    bg = _pad_gate_cols(b_gate).reshape(1, KV_PAD)

    def tile(width):
        return pl.BlockSpec((1, PROJ_TM, width), lambda b, i: (b, i, 0))

    def out(width, dtype):
        return jax.ShapeDtypeStruct((bsz, s, width), dtype)

    vt_spec = pl.BlockSpec((1, KV_PAD, PROJ_TM), lambda b, i: (b, 0, i))
    vt_out = jax.ShapeDtypeStruct((bsz, KV_PAD, s), BF16)
    return pl.pallas_call(
        _proj_kernel,
        grid=(bsz, s // PROJ_TM),
        in_specs=[
            tile(D_MODEL),
            pl.BlockSpec((D_MODEL, _C_END), lambda b, i: (0, 0), pipeline_mode=pl.Buffered(1)),
            pl.BlockSpec((2 * KV_PAD, D_MODEL), lambda b, i: (0, 0), pipeline_mode=pl.Buffered(1)),
            pl.BlockSpec((1, KV_PAD), lambda b, i: (0, 0)),
        ],
        out_specs=[tile(POOL_WIDTH), tile(Q_PAD), tile(KV_PAD), tile(KV_PAD), tile(KV_PAD),
                   tile(KV_PAD), tile(KV_PAD), vt_spec, vt_spec],
        out_shape=[out(POOL_WIDTH, F32), out(Q_PAD, BF16), out(KV_PAD, F32), out(KV_PAD, F32),
                   out(KV_PAD, BF16), out(KV_PAD, BF16), out(KV_PAD, F32), vt_out, vt_out],
        compiler_params=_params("parallel", "parallel"),
        name="in_proj",
    )(x, w, wvt, bg)


def _gelu_tanh(x):
    c = np.float32(np.sqrt(2.0 / np.pi))
    return x * (0.5 * (1.0 + jnp.tanh(c * (x + 0.044715 * (x * x * x)))))


def _compress_kernel(x_ref, plo_ref, phi_ref, wlo_ref, whi_ref, w2_ref, o_ref, a_scr, b_scr,
                     *, channel_major):
    r = pl.program_id(1)
    nsb = x_ref.shape[1]
    xr = x_ref[0]
    a_scr[r] = _dot((xr + plo_ref[...]).astype(BF16), wlo_ref[...])
    b_scr[r, 0:nsb, :] = _dot((xr + phi_ref[...]).astype(BF16), whi_ref[...])

    @pl.when(r == 0)
    def _():
        b_scr[0, nsb:nsb + 8, :] = jnp.zeros((8, KV_PAD), F32)

    @pl.when(r == CMP_PER_SLC - 1)
    def _():
        for rr in range(CMP_PER_SLC):
            if rr < CMP_PER_SLC - 1:
                h = a_scr[rr] + b_scr[rr + 1, 0:nsb, :]
            else:
                h = a_scr[rr] + b_scr[0, 1:nsb + 1, :]
            act = _gelu_tanh(h).astype(BF16)
            if channel_major:
                o_ref[0, :, rr * nsb:(rr + 1) * nsb] = _dot_nt(w2_ref[...], act).astype(BF16)
            else:
                o_ref[0, rr * nsb:(rr + 1) * nsb, :] = _dot(act, w2_ref[...]).astype(BF16)


def _compress_weights(pos, w1, w2):
    eye_g = jnp.eye(N_KV_GROUPS, dtype=F32)
    w1r = w1.reshape(CMP_BLOCK, HEAD_DIM, HEAD_DIM)
    w1r = jnp.pad(w1r, ((0, 0), (0, HEAD_PAD - HEAD_DIM), (0, HEAD_PAD - HEAD_DIM)))
    w1c = w1r[:, None, :, None, :] * eye_g[None, :, None, :, None]
    w1c = w1c.reshape(CMP_BLOCK, KV_PAD, KV_PAD)
    wlo = w1c[:CMP_STRIDE].reshape(CHUNK_FLAT, KV_PAD).astype(BF16)
    whi = w1c[CMP_STRIDE:].reshape(CHUNK_FLAT, KV_PAD).astype(BF16)
    posp = jnp.pad(pos, ((0, 0), (0, HEAD_PAD - HEAD_DIM)))
    posp = jnp.tile(posp[:, None, :], (1, N_KV_GROUPS, 1))
    plo = posp[:CMP_STRIDE].reshape(1, CHUNK_FLAT)
    phi = posp[CMP_STRIDE:].reshape(1, CHUNK_FLAT)
    w2p = jnp.pad(w2, ((0, HEAD_PAD - HEAD_DIM), (0, HEAD_PAD - HEAD_DIM)))
    w2c = (w2p[None, :, None, :] * eye_g[:, None, :, None]).reshape(KV_PAD, KV_PAD).astype(BF16)
    return plo, phi, wlo, whi, w2c


def _compress(kv, pos, w1, w2, channel_major):
    bsz, s, _ = kv.shape
    nsb = s // SLC_BLOCK
    ncp = CMP_PER_SLC * nsb
    plo, phi, wlo, whi, w2c = _compress_weights(pos, w1, w2)
    if channel_major:
        w2c = w2c.T
    out_dims = (KV_PAD, ncp) if channel_major else (ncp, KV_PAD)
    x = kv.reshape(bsz, nsb, CMP_PER_SLC * CHUNK_FLAT)
    const = lambda b, r: (0, 0)
    return pl.pallas_call(
        functools.partial(_compress_kernel, channel_major=channel_major),
        grid=(bsz, CMP_PER_SLC),
        in_specs=[
            pl.BlockSpec((1, nsb, CHUNK_FLAT), lambda b, r: (b, 0, r)),
            pl.BlockSpec((1, CHUNK_FLAT), const),
            pl.BlockSpec((1, CHUNK_FLAT), const),
            pl.BlockSpec((CHUNK_FLAT, KV_PAD), const),
            pl.BlockSpec((CHUNK_FLAT, KV_PAD), const),
            pl.BlockSpec((KV_PAD, KV_PAD), const),
        ],
        out_specs=pl.BlockSpec((1,) + out_dims, lambda b, r: (b, 0, 0)),
        out_shape=jax.ShapeDtypeStruct((bsz,) + out_dims, BF16),
        scratch_shapes=[pltpu.VMEM((CMP_PER_SLC, nsb, KV_PAD), F32),
                        pltpu.VMEM((CMP_PER_SLC, nsb + 8, KV_PAD), F32)],
        compiler_params=_params("parallel", "arbitrary"),
        name="compress",
    )(x, plo, phi, wlo, whi, w2c)


TQ = 128
COLS = HEADS_PER_GROUP * TQ


def _load_q(q_ref):
    return jnp.concatenate(
        [q_ref[0, :, h * HEAD_PAD:(h + 1) * HEAD_PAD] for h in range(HEADS_PER_GROUP)], axis=0)


def _per_head(row):
    return jnp.concatenate([row] * HEADS_PER_GROUP, axis=1)


def _store_gated(o_ref, g_ref, o_t, branch):
    for h in range(HEADS_PER_GROUP):
        c = h * N_BRANCHES + branch
        gate = g_ref[0, :, c:c + 1]
        o_ref[0, :, h * HEAD_PAD:(h + 1) * HEAD_PAD] = o_t[:, h * TQ:(h + 1) * TQ].T * gate


def _attn_specs():
    q_spec = pl.BlockSpec((1, TQ, GROUP_Q), lambda b, g, i: (b, i, g))
    g_spec = pl.BlockSpec((1, TQ, HEAD_PAD), lambda b, g, i: (b, i, g))
    o_spec = pl.BlockSpec((1, TQ, GROUP_Q), lambda b, g, i: (b, i, g))
    return q_spec, g_spec, o_spec


def _cmp_kernel(q_ref, kc_ref, vct_ref, g_ref, cend_ref, place_ref, o_ref, selq_ref):
    ncp = kc_ref.shape[1]
    nsb = ncp // CMP_PER_SLC
    t0 = pl.program_id(2) * TQ
    s = _dot_nt(kc_ref[0], _load_q(q_ref))
    s = s + _per_head(jnp.where(cend_ref[...] <= t0, 0.0, NEG_BIG))
    m = jnp.max(s, axis=0, keepdims=True)
    e = jnp.exp(s - m)
    l = jnp.sum(e, axis=0, keepdims=True)
    tcol = t0 + lax.broadcasted_iota(jnp.int32, (1, TQ), 1)
    any_visible = _per_head(jnp.where(tcol >= CMP_BLOCK - 1, 1.0, 0.0))
    p = e * (any_visible / jnp.maximum(l, 1e-30))

    o_t = _dot(vct_ref[0], p.astype(BF16))
    _store_gated(o_ref, g_ref, o_t, 0)

    imp = p[:, 0:TQ]
    for h in range(1, HEADS_PER_GROUP):
        imp = imp + p[:, h * TQ:(h + 1) * TQ]
    p0, p1, p2, p3 = (imp[r * nsb:(r + 1) * nsb, :] for r in range(CMP_PER_SLC))
    blk = lax.broadcasted_iota(jnp.int32, (nsb, TQ), 0)
    p3_prev = jnp.where(blk == 0, 0.0, pltpu.roll(p3, 1, axis=0))
    imp_slc = 0.5 * p3_prev + p0 + p1 + p2 + 0.5 * p3

    jt = (t0 + lax.broadcasted_iota(jnp.int32, (nsb, TQ), 1)) >> SLC_BLOCK_LOG2
    forced = (blk == 0) | (blk == jt) | (blk == jt - 1)
    valid = blk <= jt
    score = jnp.where(valid, jnp.where(forced, SEL_BIG, imp_slc), NEG_BIG)

    def pick(_, sc):
        mx = jnp.max(sc, axis=0, keepdims=True)
        first = jnp.min(jnp.where(sc == mx, blk, nsb), axis=0, keepdims=True)
        return jnp.where(blk == first, -jnp.inf, sc)

    picked = lax.fori_loop(0, min(N_SELECT, nsb), pick, score) == -jnp.inf
    bias = jnp.where(picked & valid, 0.0, NEG_BIG).T.astype(BF16)
    selq_ref[0, 0] = _dot(bias, place_ref[...]).astype(BF16)


def _mask_lane_placement(nsb):
    j = np.arange(nsb)
    place = np.zeros((nsb, (nsb // MASK_LANES) * HEAD_PAD), np.float32)
    place[j, (j // MASK_LANES) * HEAD_PAD + HEAD_DIM + j % MASK_LANES] = 1.0
    return jnp.asarray(place, BF16)


def _cmp_end_minus_token(nsb):
    row = np.arange(CMP_PER_SLC * nsb)
    end = (row % nsb) * SLC_BLOCK + (row // nsb) * CMP_STRIDE + CMP_BLOCK - 1
    return jnp.asarray(end[:, None] - np.arange(TQ)[None, :], jnp.int32)


def _cmp_attention(q, kcmp, vcmp_t, gates):
    bsz, s, _ = q.shape
    nsb = s // SLC_BLOCK
    ncp = CMP_PER_SLC * nsb
    selq_w = (nsb // MASK_LANES) * HEAD_PAD
    q_spec, g_spec, o_spec = _attn_specs()
    const = lambda b, g, i: (0, 0)
    return pl.pallas_call(
        _cmp_kernel,
        grid=(bsz, N_KV_GROUPS, s // TQ),
        in_specs=[q_spec,
                  pl.BlockSpec((1, ncp, HEAD_PAD), lambda b, g, i: (b, 0, g)),
                  pl.BlockSpec((1, HEAD_PAD, ncp), lambda b, g, i: (b, g, 0)),
                  g_spec,
                  pl.BlockSpec((ncp, TQ), const),
                  pl.BlockSpec((nsb, selq_w), const)],
        out_specs=[o_spec, pl.BlockSpec((1, 1, TQ, selq_w), lambda b, g, i: (b, g, i, 0))],
        out_shape=[jax.ShapeDtypeStruct((bsz, s, Q_PAD), F32),
                   jax.ShapeDtypeStruct((bsz, N_KV_GROUPS, s, selq_w), BF16)],
        compiler_params=_params("parallel", "parallel", "arbitrary"),
        name="cmp_attn_topk",
    )(q, kcmp, vcmp_t, gates, _cmp_end_minus_token(nsb), _mask_lane_placement(nsb))


SLC_TK = 512
SUPER_KEYS = MASK_LANES * SLC_BLOCK
TILES_PER_SUPER = SUPER_KEYS // SLC_TK


PV_COLS = 256


def _online_softmax_step(s, s_max, v_t, m, l, acc):
    m_new = jnp.maximum(m, s_max)
    alpha = jnp.exp(m - m_new)
    l_parts, pv_parts = [], []
    for c0 in range(0, COLS, PV_COLS):
        p = jnp.exp(s[:, c0:c0 + PV_COLS] - m_new[:, c0:c0 + PV_COLS])
        l_parts.append(jnp.sum(p, axis=0, keepdims=True))
        pv_parts.append(_dot(v_t, p.astype(BF16)))
    l = alpha * l + jnp.concatenate(l_parts, axis=1)
    acc = alpha * acc + jnp.concatenate(pv_parts, axis=1)
    return m_new, l, acc


def _slc_kernel(q_ref, k_ref, vt_ref, selq_ref, g_ref, o_ref, qm_scr):
    n_super = selq_ref.shape[3] // HEAD_PAD
    qi = pl.program_id(2)
    t0 = qi * TQ
    n_kt = (t0 + TQ + SLC_TK - 1) // SLC_TK

    for st in range(n_super):
        @pl.when(st * TILES_PER_SUPER < n_kt)
        def _():
            slab = selq_ref[0, 0, :, st * HEAD_PAD:(st + 1) * HEAD_PAD]
            for h in range(HEADS_PER_GROUP):
                qm_scr[st, h * TQ:(h + 1) * TQ, :] = (
                    q_ref[0, :, h * HEAD_PAD:(h + 1) * HEAD_PAD] + slab)

    def scores(kt):
        k0 = pl.multiple_of(kt * SLC_TK, SLC_TK)
        return _dot_nt(k_ref[0, pl.ds(k0, SLC_TK), :], qm_scr[kt // TILES_PER_SUPER])

    def values(kt):
        return vt_ref[0, :, pl.ds(pl.multiple_of(kt * SLC_TK, SLC_TK), SLC_TK)]

    def step(kt, carry):
        s, s_max, m, l, acc = carry
        s_next = scores(kt + 1)
        s_next_max = jnp.max(s_next, axis=0, keepdims=True)
        m, l, acc = _online_softmax_step(s, s_max, values(kt), m, l, acc)
        return s_next, s_next_max, m, l, acc

    s0 = scores(0)
    init = (s0, jnp.max(s0, axis=0, keepdims=True),
            jnp.full((1, COLS), NEG_BIG, F32),
            jnp.zeros((1, COLS), F32),
            jnp.zeros((HEAD_PAD, COLS), F32))
    s, _, m, l, acc = lax.fori_loop(0, n_kt - 1, step, init)

    last = n_kt - 1
    kpos = last * SLC_TK + lax.broadcasted_iota(jnp.int32, (SLC_TK, TQ), 0)
    tpos = t0 + lax.broadcasted_iota(jnp.int32, (SLC_TK, TQ), 1)
    s = s + _per_head(jnp.where(kpos <= tpos, 0.0, NEG_BIG))
    _, l, acc = _online_softmax_step(s, jnp.max(s, axis=0, keepdims=True), values(last),
                                     m, l, acc)
    _store_gated(o_ref, g_ref, acc / jnp.maximum(l, 1e-30), 1)


def _slc_attention(q, ks, vs_t, selq, gates):
    bsz, s, _ = q.shape
    n_super = s // SUPER_KEYS
    q_spec, g_spec, o_spec = _attn_specs()
    return pl.pallas_call(
        _slc_kernel,
        grid=(bsz, N_KV_GROUPS, s // TQ),
        in_specs=[q_spec,
                  pl.BlockSpec((1, s, HEAD_PAD), lambda b, g, i: (b, 0, g)),
                  pl.BlockSpec((1, HEAD_PAD, s), lambda b, g, i: (b, g, 0)),
                  pl.BlockSpec((1, 1, TQ, n_super * HEAD_PAD), lambda b, g, i: (b, g, i, 0)),
                  g_spec],
        out_specs=o_spec,
        out_shape=jax.ShapeDtypeStruct((bsz, s, Q_PAD), F32),
        scratch_shapes=[pltpu.VMEM((n_super, COLS, HEAD_PAD), BF16)],
        compiler_params=_params("parallel", "parallel", "arbitrary"),
        name="slc_attn",
    )(q, ks, vs_t, selq, gates)


WIN_KEYS = WINDOW + TQ


def _win_kernel(q_ref, k_ref, vt_ref, g_ref, o_ref):
    t0 = pl.program_id(2) * TQ
    k0 = pl.multiple_of(jnp.maximum(t0 - WINDOW, 0), TQ)
    s = _dot_nt(k_ref[0, pl.ds(k0, WIN_KEYS), :], _load_q(q_ref))
    diff = ((t0 + lax.broadcasted_iota(jnp.int32, (WIN_KEYS, TQ), 1))
            - (k0 + lax.broadcasted_iota(jnp.int32, (WIN_KEYS, TQ), 0)))
    s = s + _per_head(jnp.where((diff >= 0) & (diff < WINDOW), 0.0, NEG_BIG))
    m = jnp.max(s, axis=0, keepdims=True)
    e = jnp.exp(s - m)
    l = jnp.sum(e, axis=0, keepdims=True)
    o_t = _dot(vt_ref[0, :, pl.ds(k0, WIN_KEYS)], e.astype(BF16))
    _store_gated(o_ref, g_ref, o_t / jnp.maximum(l, 1e-30), 2)


def _win_attention(q, kw, vw_t, gates):
    bsz, s, _ = q.shape
    q_spec, g_spec, o_spec = _attn_specs()
    return pl.pallas_call(
        _win_kernel,
        grid=(bsz, N_KV_GROUPS, s // TQ),
        in_specs=[q_spec,
                  pl.BlockSpec((1, s, HEAD_PAD), lambda b, g, i: (b, 0, g)),
                  pl.BlockSpec((1, HEAD_PAD, s), lambda b, g, i: (b, g, 0)),
                  g_spec],
        out_specs=o_spec,
        out_shape=jax.ShapeDtypeStruct((bsz, s, Q_PAD), F32),
        compiler_params=_params("parallel", "parallel", "arbitrary"),
        name="win_attn",
    )(q, kw, vw_t, gates)


MIX_TM = 512
POOL_HALO = 16


def _mix_out_kernel(x_ref, u_ref, halo_ref, oc_ref, os_ref, ow_ref, pw_ref, ps_ref,
                    wo_ref, g_ref, b_ref, o_ref, ext_scr):
    i = pl.program_id(1)
    u = u_ref[0]
    halo = jnp.where(i == 0, 0.0, halo_ref[0])
    ext_scr[0:POOL_HALO, :] = halo
    ext_scr[POOL_HALO:POOL_HALO + MIX_TM, :] = u

    lane = lax.broadcasted_iota(jnp.int32, (MIX_TM, POOL_WIDTH), 1)
    tpos = i * MIX_TM + lax.broadcasted_iota(jnp.int32, (MIX_TM, POOL_WIDTH), 0)
    grp = lane >> (POOL_GROUP_DIM.bit_length() - 1)
    run = u
    win_sum = jnp.zeros_like(u)
    cnt = jnp.zeros_like(u)
    done = 1
    for gidx, w in enumerate(POOL_WINDOWS):
        for kback in range(done, w):
            run = run + ext_scr[POOL_HALO - kback:POOL_HALO - kback + MIX_TM, :]
        done = w
        win_sum = jnp.where(grp == gidx, run, win_sum)
        cnt = jnp.where(grp == gidx, jnp.minimum(tpos + 1, w).astype(F32), cnt)
    pooled = win_sum / cnt - u
    mixed = _dot(pooled.astype(BF16), pw_ref[...]) * ps_ref[...]

    y_nsa = oc_ref[0] + os_ref[0] + ow_ref[0]
    y = (_dot(mixed.astype(BF16), wo_ref[0:POOL_WIDTH, :])
         + _dot(y_nsa.astype(BF16), wo_ref[POOL_WIDTH:, :]))
    z = ALPHA * x_ref[0] + y
    o_ref[0] = _layer_norm(z, g_ref[...], b_ref[...])


def _mix_out(x, u, o_cmp, o_slc, o_win, pool_w, pool_scale, w_out, g, b):
    bsz, s, _ = x.shape
    n_grp = len(POOL_WINDOWS)
    eye = jnp.eye(n_grp, dtype=F32)
    pw = (pool_w[:, :, None, :] * eye[:, None, :, None]).reshape(POOL_WIDTH, POOL_WIDTH).astype(BF16)
    wo_nsa = w_out[POOL_WIDTH:].reshape(N_Q_HEADS, HEAD_DIM, D_MODEL)
    wo_nsa = jnp.pad(wo_nsa, ((0, 0), (0, HEAD_PAD - HEAD_DIM), (0, 0))).reshape(Q_PAD, D_MODEL)
    wo = jnp.concatenate([w_out[:POOL_WIDTH], wo_nsa], axis=0).astype(BF16)

    def tile(width):
        return pl.BlockSpec((1, MIX_TM, width), lambda bb, i: (bb, i, 0))

    halo_blocks = MIX_TM // POOL_HALO
    const = lambda bb, i: (0, 0)
    return pl.pallas_call(
        _mix_out_kernel,
        grid=(bsz, s // MIX_TM),
        in_specs=[
            tile(D_MODEL), tile(POOL_WIDTH),
            pl.BlockSpec((1, POOL_HALO, POOL_WIDTH),
                         lambda bb, i: (bb, jnp.maximum(i * halo_blocks - 1, 0), 0)),
            tile(Q_PAD), tile(Q_PAD), tile(Q_PAD),
            pl.BlockSpec((POOL_WIDTH, POOL_WIDTH), const),
            pl.BlockSpec((1, POOL_WIDTH), const),
            pl.BlockSpec((POOL_WIDTH + Q_PAD, D_MODEL), const),
            pl.BlockSpec((1, D_MODEL), const),
            pl.BlockSpec((1, D_MODEL), const),
        ],
        out_specs=tile(D_MODEL),
        out_shape=jax.ShapeDtypeStruct((bsz, s, D_MODEL), F32),
        scratch_shapes=[pltpu.VMEM((POOL_HALO + MIX_TM, POOL_WIDTH), F32)],
        compiler_params=_params("parallel", "arbitrary"),
        name="mix_out_ln",
    )(x, u, u, o_cmp, o_slc, o_win, pw, pool_scale.reshape(1, -1), wo,
      g.reshape(1, -1), b.reshape(1, -1))


def kernel(x, ln1_g, ln1_b, ffn1_w_gate, ffn1_w_up, ffn1_w_down, w_in, b_gate, pool_w, pool_scale, cmp_pos_k, cmp_k_w1, cmp_k_w2, cmp_pos_v, cmp_v_w1, cmp_v_w2, w_out, ln2_g, ln2_b, ffn2_w_gate, ffn2_w_up, ffn2_w_down, ln3_g, ln3_b):
    bsz, s, d = x.shape
    assert d == D_MODEL and s % max(SUPER_KEYS, FFN_TM, MIX_TM, PROJ_TM) == 0
    for l in range(DEPTH):
        x = _ffn_ln(x.reshape(bsz * s, d), ffn1_w_gate[l], ffn1_w_up[l], ffn1_w_down[l],
                    ln1_g[l], ln1_b[l]).reshape(bsz, s, d)
        u, q, kc, vc, ks, kw, gates, vs_t, vw_t = _proj(x, w_in[l], b_gate[l])
        kcmp = _compress(kc, cmp_pos_k[l], cmp_k_w1[l], cmp_k_w2[l], channel_major=False)
        vcmp_t = _compress(vc, cmp_pos_v[l], cmp_v_w1[l], cmp_v_w2[l], channel_major=True)
        o_cmp, selq = _cmp_attention(q, kcmp, vcmp_t, gates)
        o_slc = _slc_attention(q, ks, vs_t, selq, gates)
        o_win = _win_attention(q, kw, vw_t, gates)
        x = _mix_out(x, u, o_cmp, o_slc, o_win, pool_w[l], pool_scale[l], w_out[l],
                     ln2_g[l], ln2_b[l])
        x = _ffn_ln(x.reshape(bsz * s, d), ffn2_w_gate[l], ffn2_w_up[l], ffn2_w_down[l],
                    ln3_g[l], ln3_b[l]).reshape(bsz, s, d)
    return x
```

```python
import functools

import numpy as np
import jax
import jax.numpy as jnp
from jax import lax
from jax.experimental import pallas as pl
from jax.experimental.pallas import tpu as pltpu

D_MODEL = 1024
DEPTH = 2
POOL_WIDTH = 256
POOL_WINDOWS = (2, 4, 8, 16)
POOL_GROUP_DIM = 64
N_Q_HEADS = 8
HEAD_DIM = 96
N_KV_GROUPS = 2
HEADS_PER_GROUP = 4
N_BRANCHES = 3
CMP_STRIDE = 16
CMP_BLOCK = 32
SLC_BLOCK = 64
SLC_BLOCK_LOG2 = 6
N_SELECT = 16
N_FORCED = 3
WINDOW = 512
D_FF = 2816
ALPHA = (2.0 * DEPTH) ** 0.25
LN_EPS = 1e-5
NEG_BIG = -1e30
SEL_BIG = 1e30
QK_SCALE = HEAD_DIM ** -0.5
LOG2_E = 1.4426950408889634
ONES_ROW = HEAD_DIM

LANES = 128
HEAD_PAD = LANES
Q_PAD = N_Q_HEADS * HEAD_PAD
KV_PAD = N_KV_GROUPS * HEAD_PAD
GROUP_Q = HEADS_PER_GROUP * HEAD_PAD
CMP_PER_SLC = SLC_BLOCK // CMP_STRIDE
CHUNK_FLAT = CMP_STRIDE * KV_PAD

VMEM_LIMIT = 56 * 1024 * 1024

F32 = jnp.float32
BF16 = jnp.bfloat16

_C_U = 0
_C_Q = _C_U + POOL_WIDTH
_C_KC = _C_Q + Q_PAD
_C_VC = _C_KC + KV_PAD
_C_KS = _C_VC + KV_PAD
_C_KW = _C_KS + KV_PAD
_C_G = _C_KW + KV_PAD
_C_END = _C_G + KV_PAD


def _params(*sem):
    return pltpu.CompilerParams(dimension_semantics=sem, vmem_limit_bytes=VMEM_LIMIT)


def _layer_norm(z, g, b):
    mu = jnp.mean(z, axis=-1, keepdims=True)
    zc = z - mu
    var = jnp.mean(zc * zc, axis=-1, keepdims=True)
    return zc * lax.rsqrt(var + LN_EPS) * g + b


def _dot(a, b):
    return jnp.dot(a, b, preferred_element_type=F32)


def _dot_nt(a, b):
    return lax.dot_general(a, b, (((1,), (1,)), ((), ())), preferred_element_type=F32)


FFN_TM = 512
FFN_CHUNK = 1408


def _ffn_ln_kernel(x_ref, wg_ref, wu_ref, wd_ref, g_ref, b_ref, o_ref):
    x = x_ref[...]
    xb = x.astype(BF16)
    acc = None
    for c in range(D_FF // FFN_CHUNK):
        lo = c * FFN_CHUNK
        hg = _dot(xb, wg_ref[:, lo:lo + FFN_CHUNK])
        hu = _dot(xb, wu_ref[:, lo:lo + FFN_CHUNK])
        h = (hg * jax.nn.sigmoid(hg)) * hu
        part = _dot(h.astype(BF16), wd_ref[lo:lo + FFN_CHUNK, :])
        acc = part if acc is None else acc + part
    z = ALPHA * x + 0.5 * acc
    o_ref[...] = _layer_norm(z, g_ref[...], b_ref[...])


def _ffn_ln(x2d, wg, wu, wd, g, b):
    t = x2d.shape[0]
    const = lambda i: (0, 0)
    return pl.pallas_call(
        _ffn_ln_kernel,
        grid=(t // FFN_TM,),
        in_specs=[
            pl.BlockSpec((FFN_TM, D_MODEL), lambda i: (i, 0)),
            pl.BlockSpec((D_MODEL, D_FF), const, pipeline_mode=pl.Buffered(1)),
            pl.BlockSpec((D_MODEL, D_FF), const, pipeline_mode=pl.Buffered(1)),
            pl.BlockSpec((D_FF, D_MODEL), const, pipeline_mode=pl.Buffered(1)),
            pl.BlockSpec((1, D_MODEL), const),
            pl.BlockSpec((1, D_MODEL), const),
        ],
        out_specs=pl.BlockSpec((FFN_TM, D_MODEL), lambda i: (i, 0)),
        out_shape=jax.ShapeDtypeStruct((t, D_MODEL), F32),
        compiler_params=_params("parallel"),
        name="ffn_ln",
    )(x2d, wg.astype(BF16), wu.astype(BF16), wd.astype(BF16), g.reshape(1, -1), b.reshape(1, -1))


PROJ_TM = 512
MASK_LANES = HEAD_PAD - HEAD_DIM


def _proj_kernel(x_ref, w_ref, wvt_ref, bg_ref, u_ref, q_ref, kc_ref, vc_ref, ks_ref, kw_ref,
                 gt_ref, vst_ref, vwt_ref):
    xb = x_ref[0].astype(BF16)

    def mm(lo, n):
        return _dot(xb, w_ref[:, lo:lo + n])

    u_ref[0] = mm(_C_U, POOL_WIDTH)
    q_ref[0] = (mm(_C_Q, Q_PAD) * (QK_SCALE * LOG2_E)).astype(BF16)
    kc_ref[0] = mm(_C_KC, KV_PAD)
    vc_ref[0] = mm(_C_VC, KV_PAD)
    tpos = pl.program_id(1) * PROJ_TM + lax.broadcasted_iota(jnp.int32, (PROJ_TM, KV_PAD), 0)
    lane = lax.broadcasted_iota(jnp.int32, (PROJ_TM, KV_PAD), 1) & (HEAD_PAD - 1)
    hot = lane == HEAD_DIM + ((tpos >> SLC_BLOCK_LOG2) & (MASK_LANES - 1))
    ks_ref[0] = jnp.where(hot, 1.0, mm(_C_KS, KV_PAD)).astype(BF16)
    kw_ref[0] = mm(_C_KW, KV_PAD).astype(BF16)
    gt_ref[0] = jax.nn.sigmoid(mm(_C_G, KV_PAD) + bg_ref[...])
    chan = lax.broadcasted_iota(jnp.int32, (KV_PAD, PROJ_TM), 0) & (HEAD_PAD - 1)
    vst_ref[0] = jnp.where(chan == ONES_ROW, 1.0, _dot_nt(wvt_ref[0:KV_PAD, :], xb)).astype(BF16)
    vwt_ref[0] = jnp.where(chan == ONES_ROW, 1.0,
                           _dot_nt(wvt_ref[KV_PAD:2 * KV_PAD, :], xb)).astype(BF16)


def _pad_heads(w, n_heads):
    lead = w.shape[:-1]
    w = w.reshape(lead + (n_heads, HEAD_DIM))
    w = jnp.pad(w, [(0, 0)] * len(lead) + [(0, 0), (0, HEAD_PAD - HEAD_DIM)])
    return w.reshape(lead + (n_heads * HEAD_PAD,))


def _pad_gate_cols(w):
    lead = w.shape[:-1]
    per_group = HEADS_PER_GROUP * N_BRANCHES
    w = w.reshape(lead + (N_KV_GROUPS, per_group))
    w = jnp.pad(w, [(0, 0)] * len(lead) + [(0, 0), (0, HEAD_PAD - per_group)])
    return w.reshape(lead + (KV_PAD,))


def _proj(x, w_in, b_gate):
    bsz, s, _ = x.shape
    cuts = np.cumsum([POOL_WIDTH, N_Q_HEADS * HEAD_DIM] + [N_KV_GROUPS * HEAD_DIM] * 6)
    parts = jnp.split(w_in, [int(c) for c in cuts], axis=-1)
    cols = [parts[0], _pad_heads(parts[1], N_Q_HEADS)]
    cols += [_pad_heads(parts[i], N_KV_GROUPS) for i in (2, 3, 4, 6)]
    cols.append(_pad_gate_cols(parts[8]))
    w = jnp.concatenate(cols, axis=-1).astype(BF16)
    wvt = jnp.concatenate([_pad_heads(parts[5], N_KV_GROUPS),
                           _pad_heads(parts[7], N_KV_GROUPS)], axis=-1).T.astype(BF16)
    bg = _pad_gate_cols(b_gate).reshape(1, KV_PAD)

    def tile(width):
        return pl.BlockSpec((1, PROJ_TM, width), lambda b, i: (b, i, 0))

    def out(width, dtype):
        return jax.ShapeDtypeStruct((bsz, s, width), dtype)

    vt_spec = pl.BlockSpec((1, KV_PAD, PROJ_TM), lambda b, i: (b, 0, i))
    vt_out = jax.ShapeDtypeStruct((bsz, KV_PAD, s), BF16)
    return pl.pallas_call(
        _proj_kernel,
        grid=(bsz, s // PROJ_TM),
        in_specs=[
            tile(D_MODEL),
            pl.BlockSpec((D_MODEL, _C_END), lambda b, i: (0, 0), pipeline_mode=pl.Buffered(1)),
            pl.BlockSpec((2 * KV_PAD, D_MODEL), lambda b, i: (0, 0), pipeline_mode=pl.Buffered(1)),
            pl.BlockSpec((1, KV_PAD), lambda b, i: (0, 0)),
        ],
        out_specs=[tile(POOL_WIDTH), tile(Q_PAD), tile(KV_PAD), tile(KV_PAD), tile(KV_PAD),
                   tile(KV_PAD), tile(KV_PAD), vt_spec, vt_spec],
        out_shape=[out(POOL_WIDTH, F32), out(Q_PAD, BF16), out(KV_PAD, F32), out(KV_PAD, F32),
                   out(KV_PAD, BF16), out(KV_PAD, BF16), out(KV_PAD, F32), vt_out, vt_out],
        compiler_params=_params("parallel", "parallel"),
        name="in_proj",
    )(x, w, wvt, bg)


def _gelu_tanh(x):
    c = np.float32(np.sqrt(2.0 / np.pi))
    return x * (0.5 * (1.0 + jnp.tanh(c * (x + 0.044715 * (x * x * x)))))


def _compress_kernel(x_ref, plo_ref, phi_ref, wlo_ref, whi_ref, w2_ref, o_ref, a_scr, b_scr,
                     *, channel_major):
    r = pl.program_id(1)
    nsb = x_ref.shape[1]
    xr = x_ref[0]
    a_scr[r] = _dot((xr + plo_ref[...]).astype(BF16), wlo_ref[...])
    b_scr[r, 0:nsb, :] = _dot((xr + phi_ref[...]).astype(BF16), whi_ref[...])

    @pl.when(r == 0)
    def _():
        b_scr[0, nsb:nsb + 8, :] = jnp.zeros((8, KV_PAD), F32)

    @pl.when(r == CMP_PER_SLC - 1)
    def _():
        for rr in range(CMP_PER_SLC):
            if rr < CMP_PER_SLC - 1:
                h = a_scr[rr] + b_scr[rr + 1, 0:nsb, :]
            else:
                h = a_scr[rr] + b_scr[0, 1:nsb + 1, :]
            act = _gelu_tanh(h).astype(BF16)
            if channel_major:
                o_ref[0, :, rr * nsb:(rr + 1) * nsb] = _dot_nt(w2_ref[...], act).astype(BF16)
            else:
                o_ref[0, rr * nsb:(rr + 1) * nsb, :] = _dot(act, w2_ref[...]).astype(BF16)


def _compress_weights(pos, w1, w2):
    eye_g = jnp.eye(N_KV_GROUPS, dtype=F32)
    w1r = w1.reshape(CMP_BLOCK, HEAD_DIM, HEAD_DIM)
    w1r = jnp.pad(w1r, ((0, 0), (0, HEAD_PAD - HEAD_DIM), (0, HEAD_PAD - HEAD_DIM)))
    w1c = w1r[:, None, :, None, :] * eye_g[None, :, None, :, None]
    w1c = w1c.reshape(CMP_BLOCK, KV_PAD, KV_PAD)
    wlo = w1c[:CMP_STRIDE].reshape(CHUNK_FLAT, KV_PAD).astype(BF16)
    whi = w1c[CMP_STRIDE:].reshape(CHUNK_FLAT, KV_PAD).astype(BF16)
    posp = jnp.pad(pos, ((0, 0), (0, HEAD_PAD - HEAD_DIM)))
    posp = jnp.tile(posp[:, None, :], (1, N_KV_GROUPS, 1))
    plo = posp[:CMP_STRIDE].reshape(1, CHUNK_FLAT)
    phi = posp[CMP_STRIDE:].reshape(1, CHUNK_FLAT)
    w2p = jnp.pad(w2, ((0, HEAD_PAD - HEAD_DIM), (0, HEAD_PAD - HEAD_DIM)))
    w2c = (w2p[None, :, None, :] * eye_g[:, None, :, None]).reshape(KV_PAD, KV_PAD).astype(BF16)
    return plo, phi, wlo, whi, w2c


def _compress(kv, pos, w1, w2, channel_major):
    bsz, s, _ = kv.shape
    nsb = s // SLC_BLOCK
    ncp = CMP_PER_SLC * nsb
    plo, phi, wlo, whi, w2c = _compress_weights(pos, w1, w2)
    if channel_major:
        w2c = w2c.T
    out_dims = (KV_PAD, ncp) if channel_major else (ncp, KV_PAD)
    x = kv.reshape(bsz, nsb, CMP_PER_SLC * CHUNK_FLAT)
    const = lambda b, r: (0, 0)
    return pl.pallas_call(
        functools.partial(_compress_kernel, channel_major=channel_major),
        grid=(bsz, CMP_PER_SLC),
        in_specs=[
            pl.BlockSpec((1, nsb, CHUNK_FLAT), lambda b, r: (b, 0, r)),
            pl.BlockSpec((1, CHUNK_FLAT), const),
            pl.BlockSpec((1, CHUNK_FLAT), const),
            pl.BlockSpec((CHUNK_FLAT, KV_PAD), const),
            pl.BlockSpec((CHUNK_FLAT, KV_PAD), const),
            pl.BlockSpec((KV_PAD, KV_PAD), const),
        ],
        out_specs=pl.BlockSpec((1,) + out_dims, lambda b, r: (b, 0, 0)),
        out_shape=jax.ShapeDtypeStruct((bsz,) + out_dims, BF16),
        scratch_shapes=[pltpu.VMEM((CMP_PER_SLC, nsb, KV_PAD), F32),
                        pltpu.VMEM((CMP_PER_SLC, nsb + 8, KV_PAD), F32)],
        compiler_params=_params("parallel", "arbitrary"),
        name="compress",
    )(x, plo, phi, wlo, whi, w2c)


TQ = 128
COLS = HEADS_PER_GROUP * TQ


def _load_q(q_ref):
    return jnp.concatenate(
        [q_ref[0, :, h * HEAD_PAD:(h + 1) * HEAD_PAD] for h in range(HEADS_PER_GROUP)], axis=0)


def _per_head(row):
    return jnp.concatenate([row] * HEADS_PER_GROUP, axis=1)


def _store_gated(o_ref, g_ref, o_t, branch):
    for h in range(HEADS_PER_GROUP):
        c = h * N_BRANCHES + branch
        gate = g_ref[0, :, c:c + 1]
        o_ref[0, :, h * HEAD_PAD:(h + 1) * HEAD_PAD] = o_t[:, h * TQ:(h + 1) * TQ].T * gate


def _attn_specs():
    q_spec = pl.BlockSpec((1, TQ, GROUP_Q), lambda b, g, i: (b, i, g))
    g_spec = pl.BlockSpec((1, TQ, HEAD_PAD), lambda b, g, i: (b, i, g))
    o_spec = pl.BlockSpec((1, TQ, GROUP_Q), lambda b, g, i: (b, i, g))
    return q_spec, g_spec, o_spec


def _cmp_kernel(q_ref, kc_ref, vct_ref, g_ref, cend_ref, place_ref, o_ref, selq_ref):
    ncp = kc_ref.shape[1]
    nsb = ncp // CMP_PER_SLC
    t0 = pl.program_id(2) * TQ
    s = _dot_nt(kc_ref[0], _load_q(q_ref))
    s = s + _per_head(jnp.where(cend_ref[...] <= t0, 0.0, NEG_BIG))
    m = jnp.max(s, axis=0, keepdims=True)
    e = jnp.exp2(s - m)
    l = jnp.sum(e, axis=0, keepdims=True)
    tcol = t0 + lax.broadcasted_iota(jnp.int32, (1, TQ), 1)
    any_visible = _per_head(jnp.where(tcol >= CMP_BLOCK - 1, 1.0, 0.0))
    p = e * (any_visible / jnp.maximum(l, 1e-30))

    o_t = _dot(vct_ref[0], p.astype(BF16))
    _store_gated(o_ref, g_ref, o_t, 0)

    imp = p[:, 0:TQ]
    for h in range(1, HEADS_PER_GROUP):
        imp = imp + p[:, h * TQ:(h + 1) * TQ]
    p0, p1, p2, p3 = (imp[r * nsb:(r + 1) * nsb, :] for r in range(CMP_PER_SLC))
    blk = lax.broadcasted_iota(jnp.int32, (nsb, TQ), 0)
    p3_prev = jnp.where(blk == 0, 0.0, pltpu.roll(p3, 1, axis=0))
    imp_slc = 0.5 * p3_prev + p0 + p1 + p2 + 0.5 * p3

    jt = (t0 + lax.broadcasted_iota(jnp.int32, (nsb, TQ), 1)) >> SLC_BLOCK_LOG2
    forced = (blk == 0) | (blk == jt) | (blk == jt - 1)
    free = (blk <= jt) & jnp.logical_not(forced)
    score = jnp.where(free, imp_slc, NEG_BIG)

    def pick(_, sc):
        mx = jnp.max(sc, axis=0, keepdims=True)
        first = jnp.min(jnp.where(sc == mx, blk, nsb), axis=0, keepdims=True)
        return jnp.where(blk == first, -jnp.inf, sc)

    picked = lax.fori_loop(0, min(N_SELECT - N_FORCED, nsb), pick, score) == -jnp.inf
    swept = ((picked & free) | forced) & (blk < (t0 >> SLC_BLOCK_LOG2))
    bias = jnp.where(swept, 0.0, NEG_BIG).T.astype(BF16)
    selq_ref[0, 0] = _dot(bias, place_ref[...]).astype(BF16)


def _mask_lane_placement(nsb):
    j = np.arange(nsb)
    place = np.zeros((nsb, (nsb // MASK_LANES) * HEAD_PAD), np.float32)
    place[j, (j // MASK_LANES) * HEAD_PAD + HEAD_DIM + j % MASK_LANES] = 1.0
    return jnp.asarray(place, BF16)


def _cmp_end_minus_token(nsb):
    row = np.arange(CMP_PER_SLC * nsb)
    end = (row % nsb) * SLC_BLOCK + (row // nsb) * CMP_STRIDE + CMP_BLOCK - 1
    return jnp.asarray(end[:, None] - np.arange(TQ)[None, :], jnp.int32)


def _cmp_attention(q, kcmp, vcmp_t, gates):
    bsz, s, _ = q.shape
    nsb = s // SLC_BLOCK
    ncp = CMP_PER_SLC * nsb
    selq_w = (nsb // MASK_LANES) * HEAD_PAD
    q_spec, g_spec, o_spec = _attn_specs()
    const = lambda b, g, i: (0, 0)
    return pl.pallas_call(
        _cmp_kernel,
        grid=(bsz, N_KV_GROUPS, s // TQ),
        in_specs=[q_spec,
                  pl.BlockSpec((1, ncp, HEAD_PAD), lambda b, g, i: (b, 0, g)),
                  pl.BlockSpec((1, HEAD_PAD, ncp), lambda b, g, i: (b, g, 0)),
                  g_spec,
                  pl.BlockSpec((ncp, TQ), const),
                  pl.BlockSpec((nsb, selq_w), const)],
        out_specs=[o_spec, pl.BlockSpec((1, 1, TQ, selq_w), lambda b, g, i: (b, g, i, 0))],
        out_shape=[jax.ShapeDtypeStruct((bsz, s, Q_PAD), F32),
                   jax.ShapeDtypeStruct((bsz, N_KV_GROUPS, s, selq_w), BF16)],
        compiler_params=_params("parallel", "parallel", "arbitrary"),
        name="cmp_attn_topk",
    )(q, kcmp, vcmp_t, gates, _cmp_end_minus_token(nsb), _mask_lane_placement(nsb))


SLC_TK = 512
SUPER_KEYS = MASK_LANES * SLC_BLOCK
TILES_PER_SUPER = SUPER_KEYS // SLC_TK


def _slc_kernel(q_ref, k_ref, vt_ref, selq_ref, g_ref, o_ref, qm_scr, s_scr):
    n_super = selq_ref.shape[3] // HEAD_PAD
    t0 = pl.multiple_of(pl.program_id(2) * TQ, TQ)
    n_full = t0 // SUPER_KEYS
    tail_tiles = (t0 - n_full * SUPER_KEYS + SLC_TK - 1) // SLC_TK

    for st in range(n_super):
        @pl.when(st * SUPER_KEYS < t0)
        def _():
            slab = selq_ref[0, 0, :, st * HEAD_PAD:(st + 1) * HEAD_PAD]
            for h in range(HEADS_PER_GROUP):
                qm_scr[st, h * TQ:(h + 1) * TQ, :] = (
                    q_ref[0, :, h * HEAD_PAD:(h + 1) * HEAD_PAD] + slab)

    s = _dot_nt(k_ref[0, pl.ds(t0, TQ), :], _load_q(q_ref))
    kk = lax.broadcasted_iota(jnp.int32, (TQ, TQ), 0)
    tt = lax.broadcasted_iota(jnp.int32, (TQ, TQ), 1)
    s = s + _per_head(jnp.where(kk <= tt, 0.0, NEG_BIG))
    m = jnp.max(s, axis=0, keepdims=True)
    acc = _dot(vt_ref[0, :, pl.ds(t0, TQ)], jnp.exp2(s - m).astype(BF16))

    def sweep(st, carry, n_keys):
        m, acc = carry
        k0 = pl.multiple_of(st * SUPER_KEYS, SUPER_KEYS)
        sc = _dot_nt(k_ref[0, pl.ds(k0, n_keys), :], qm_scr[st])
        s_scr[0:n_keys, :] = sc
        m_new = jnp.maximum(m, jnp.max(sc, axis=0, keepdims=True))
        acc = jnp.exp2(m - m_new) * acc
        for c in range(0, n_keys, SLC_TK):
            p = jnp.exp2(s_scr[c:c + SLC_TK, :] - m_new).astype(BF16)
            acc = acc + _dot(vt_ref[0, :, pl.ds(k0 + c, SLC_TK)], p)
        return m_new, acc

    carry = lax.fori_loop(0, n_full, functools.partial(sweep, n_keys=SUPER_KEYS), (m, acc))
    branches = [lambda c: c] + [functools.partial(sweep, n_full, n_keys=n * SLC_TK)
                                for n in range(1, TILES_PER_SUPER + 1)]
    _, acc = lax.switch(tail_tiles, branches, carry)
    _store_gated(o_ref, g_ref, acc / jnp.maximum(acc[ONES_ROW:ONES_ROW + 1, :], 1e-30), 1)


def _slc_attention(q, ks, vs_t, selq, gates):
    bsz, s, _ = q.shape
    n_super = s // SUPER_KEYS
    q_spec, g_spec, o_spec = _attn_specs()
    return pl.pallas_call(
        _slc_kernel,
        grid=(bsz, N_KV_GROUPS, s // TQ),
        in_specs=[q_spec,
                  pl.BlockSpec((1, s, HEAD_PAD), lambda b, g, i: (b, 0, g)),
                  pl.BlockSpec((1, HEAD_PAD, s), lambda b, g, i: (b, g, 0)),
                  pl.BlockSpec((1, 1, TQ, n_super * HEAD_PAD), lambda b, g, i: (b, g, i, 0)),
                  g_spec],
        out_specs=o_spec,
        out_shape=jax.ShapeDtypeStruct((bsz, s, Q_PAD), F32),
        scratch_shapes=[pltpu.VMEM((n_super, COLS, HEAD_PAD), BF16),
                        pltpu.VMEM((SUPER_KEYS, COLS), F32)],
        compiler_params=_params("parallel", "parallel", "arbitrary"),
        name="slc_attn",
    )(q, ks, vs_t, selq, gates)


WIN_KEYS = WINDOW + TQ


def _win_kernel(q_ref, k_ref, vt_ref, g_ref, o_ref):
    t0 = pl.program_id(2) * TQ
    k0 = pl.multiple_of(jnp.maximum(t0 - WINDOW, 0), TQ)
    s = _dot_nt(k_ref[0, pl.ds(k0, WIN_KEYS), :], _load_q(q_ref))
    diff = ((t0 + lax.broadcasted_iota(jnp.int32, (WIN_KEYS, TQ), 1))
            - (k0 + lax.broadcasted_iota(jnp.int32, (WIN_KEYS, TQ), 0)))
    s = s + _per_head(jnp.where((diff >= 0) & (diff < WINDOW), 0.0, NEG_BIG))
    m = jnp.max(s, axis=0, keepdims=True)
    e = jnp.exp2(s - m).astype(BF16)
    o_t = _dot(vt_ref[0, :, pl.ds(k0, WIN_KEYS)], e)
    _store_gated(o_ref, g_ref, o_t / jnp.maximum(o_t[ONES_ROW:ONES_ROW + 1, :], 1e-30), 2)


def _win_attention(q, kw, vw_t, gates):
    bsz, s, _ = q.shape
    q_spec, g_spec, o_spec = _attn_specs()
    return pl.pallas_call(
        _win_kernel,
        grid=(bsz, N_KV_GROUPS, s // TQ),
        in_specs=[q_spec,
                  pl.BlockSpec((1, s, HEAD_PAD), lambda b, g, i: (b, 0, g)),
                  pl.BlockSpec((1, HEAD_PAD, s), lambda b, g, i: (b, g, 0)),
                  g_spec],
        out_specs=o_spec,
        out_shape=jax.ShapeDtypeStruct((bsz, s, Q_PAD), F32),
        compiler_params=_params("parallel", "parallel", "arbitrary"),
        name="win_attn",
    )(q, kw, vw_t, gates)


MIX_TM = 512
POOL_HALO = 16


def _mix_out_kernel(x_ref, u_ref, halo_ref, oc_ref, os_ref, ow_ref, pw_ref, ps_ref,
                    wo_ref, g_ref, b_ref, o_ref, ext_scr):
    i = pl.program_id(1)
    u = u_ref[0]
    halo = jnp.where(i == 0, 0.0, halo_ref[0])
    ext_scr[0:POOL_HALO, :] = halo
    ext_scr[POOL_HALO:POOL_HALO + MIX_TM, :] = u

    lane = lax.broadcasted_iota(jnp.int32, (MIX_TM, POOL_WIDTH), 1)
    tpos = i * MIX_TM + lax.broadcasted_iota(jnp.int32, (MIX_TM, POOL_WIDTH), 0)
    grp = lane >> (POOL_GROUP_DIM.bit_length() - 1)
    run = u
    win_sum = jnp.zeros_like(u)
    cnt = jnp.zeros_like(u)
    done = 1
    for gidx, w in enumerate(POOL_WINDOWS):
        for kback in range(done, w):
            run = run + ext_scr[POOL_HALO - kback:POOL_HALO - kback + MIX_TM, :]
        done = w
        win_sum = jnp.where(grp == gidx, run, win_sum)
        cnt = jnp.where(grp == gidx, jnp.minimum(tpos + 1, w).astype(F32), cnt)
    pooled = win_sum / cnt - u
    mixed = _dot(pooled.astype(BF16), pw_ref[...]) * ps_ref[...]

    y_nsa = oc_ref[0] + os_ref[0] + ow_ref[0]
    y = (_dot(mixed.astype(BF16), wo_ref[0:POOL_WIDTH, :])
         + _dot(y_nsa.astype(BF16), wo_ref[POOL_WIDTH:, :]))
    z = ALPHA * x_ref[0] + y
    o_ref[0] = _layer_norm(z, g_ref[...], b_ref[...])


def _mix_out(x, u, o_cmp, o_slc, o_win, pool_w, pool_scale, w_out, g, b):
    bsz, s, _ = x.shape
    n_grp = len(POOL_WINDOWS)
    eye = jnp.eye(n_grp, dtype=F32)
    pw = (pool_w[:, :, None, :] * eye[:, None, :, None]).reshape(POOL_WIDTH, POOL_WIDTH).astype(BF16)
    wo_nsa = w_out[POOL_WIDTH:].reshape(N_Q_HEADS, HEAD_DIM, D_MODEL)
    wo_nsa = jnp.pad(wo_nsa, ((0, 0), (0, HEAD_PAD - HEAD_DIM), (0, 0))).reshape(Q_PAD, D_MODEL)
    wo = jnp.concatenate([w_out[:POOL_WIDTH], wo_nsa], axis=0).astype(BF16)

    def tile(width):
        return pl.BlockSpec((1, MIX_TM, width), lambda bb, i: (bb, i, 0))

    halo_blocks = MIX_TM // POOL_HALO
    const = lambda bb, i: (0, 0)
    return pl.pallas_call(
        _mix_out_kernel,
        grid=(bsz, s // MIX_TM),
        in_specs=[
            tile(D_MODEL), tile(POOL_WIDTH),
            pl.BlockSpec((1, POOL_HALO, POOL_WIDTH),
                         lambda bb, i: (bb, jnp.maximum(i * halo_blocks - 1, 0), 0)),
            tile(Q_PAD), tile(Q_PAD), tile(Q_PAD),
            pl.BlockSpec((POOL_WIDTH, POOL_WIDTH), const),
            pl.BlockSpec((1, POOL_WIDTH), const),
            pl.BlockSpec((POOL_WIDTH + Q_PAD, D_MODEL), const),
            pl.BlockSpec((1, D_MODEL), const),
            pl.BlockSpec((1, D_MODEL), const),
        ],
        out_specs=tile(D_MODEL),
        out_shape=jax.ShapeDtypeStruct((bsz, s, D_MODEL), F32),
        scratch_shapes=[pltpu.VMEM((POOL_HALO + MIX_TM, POOL_WIDTH), F32)],
        compiler_params=_params("parallel", "arbitrary"),
        name="mix_out_ln",
    )(x, u, u, o_cmp, o_slc, o_win, pw, pool_scale.reshape(1, -1), wo,
      g.reshape(1, -1), b.reshape(1, -1))


def kernel(x, ln1_g, ln1_b, ffn1_w_gate, ffn1_w_up, ffn1_w_down, w_in, b_gate, pool_w, pool_scale, cmp_pos_k, cmp_k_w1, cmp_k_w2, cmp_pos_v, cmp_v_w1, cmp_v_w2, w_out, ln2_g, ln2_b, ffn2_w_gate, ffn2_w_up, ffn2_w_down, ln3_g, ln3_b):
    bsz, s, d = x.shape
    assert d == D_MODEL and s % max(SUPER_KEYS, FFN_TM, MIX_TM, PROJ_TM) == 0
    for l in range(DEPTH):
        x = _ffn_ln(x.reshape(bsz * s, d), ffn1_w_gate[l], ffn1_w_up[l], ffn1_w_down[l],
                    ln1_g[l], ln1_b[l]).reshape(bsz, s, d)
        u, q, kc, vc, ks, kw, gates, vs_t, vw_t = _proj(x, w_in[l], b_gate[l])
        kcmp = _compress(kc, cmp_pos_k[l], cmp_k_w1[l], cmp_k_w2[l], channel_major=False)
        vcmp_t = _compress(vc, cmp_pos_v[l], cmp_v_w1[l], cmp_v_w2[l], channel_major=True)
        o_cmp, selq = _cmp_attention(q, kcmp, vcmp_t, gates)
        o_slc = _slc_attention(q, ks, vs_t, selq, gates)
        o_win = _win_attention(q, kw, vw_t, gates)
        x = _mix_out(x, u, o_cmp, o_slc, o_win, pool_w[l], pool_scale[l], w_out[l],
                     ln2_g[l], ln2_b[l])
        x = _ffn_ln(x.reshape(bsz * s, d), ffn2_w_gate[l], ffn2_w_up[l], ffn2_w_down[l],
                    ln3_g[l], ln3_b[l]).reshape(bsz, s, d)
    return x
```

```python
import functools

import numpy as np
import jax
import jax.numpy as jnp
from jax import lax
from jax.experimental import pallas as pl
from jax.experimental.pallas import tpu as pltpu

D_MODEL = 1024
DEPTH = 2
POOL_WIDTH = 256
POOL_WINDOWS = (2, 4, 8, 16)
POOL_GROUP_DIM = 64
N_Q_HEADS = 8
HEAD_DIM = 96
N_KV_GROUPS = 2
HEADS_PER_GROUP = 4
N_BRANCHES = 3
CMP_STRIDE = 16
CMP_BLOCK = 32
SLC_BLOCK = 64
SLC_BLOCK_LOG2 = 6
N_SELECT = 16
N_FORCED = 3
WINDOW = 512
D_FF = 2816
ALPHA = (2.0 * DEPTH) ** 0.25
LN_EPS = 1e-5
NEG_BIG = -1e30
SEL_BIG = 1e30
QK_SCALE = HEAD_DIM ** -0.5
LOG2_E = 1.4426950408889634
ONES_ROW = HEAD_DIM

LANES = 128
HEAD_PAD = LANES
Q_PAD = N_Q_HEADS * HEAD_PAD
KV_PAD = N_KV_GROUPS * HEAD_PAD
GROUP_Q = HEADS_PER_GROUP * HEAD_PAD
CMP_PER_SLC = SLC_BLOCK // CMP_STRIDE
CHUNK_FLAT = CMP_STRIDE * KV_PAD

VMEM_LIMIT = 56 * 1024 * 1024

F32 = jnp.float32
BF16 = jnp.bfloat16

_C_U = 0
_C_Q = _C_U + POOL_WIDTH
_C_KC = _C_Q + Q_PAD
_C_VC = _C_KC + KV_PAD
_C_KS = _C_VC + KV_PAD
_C_KW = _C_KS + KV_PAD
_C_G = _C_KW + KV_PAD
_C_END = _C_G + KV_PAD


def _params(*sem):
    return pltpu.CompilerParams(dimension_semantics=sem, vmem_limit_bytes=VMEM_LIMIT)


def _layer_norm(z, g, b):
    mu = jnp.mean(z, axis=-1, keepdims=True)
    zc = z - mu
    var = jnp.mean(zc * zc, axis=-1, keepdims=True)
    return zc * lax.rsqrt(var + LN_EPS) * g + b


def _dot(a, b):
    return jnp.dot(a, b, preferred_element_type=F32)


def _dot_nt(a, b):
    return lax.dot_general(a, b, (((1,), (1,)), ((), ())), preferred_element_type=F32)


FFN_TM = 1024
MXU_TILE = 256
FFN_CHUNKS = (6 * MXU_TILE, 5 * MXU_TILE)
assert sum(FFN_CHUNKS) == D_FF


def _ffn_ln_kernel(x_ref, wg_ref, wu_ref, wd_ref, g_ref, b_ref, o_ref):
    x = x_ref[...]
    xb = x.astype(BF16)
    acc = None
    lo = 0
    for width in FFN_CHUNKS:
        hg = _dot(xb, wg_ref[:, lo:lo + width])
        hu = _dot(xb, wu_ref[:, lo:lo + width])
        h = (hg * jax.nn.sigmoid(hg)) * hu
        part = _dot(h.astype(BF16), wd_ref[lo:lo + width, :])
        acc = part if acc is None else acc + part
        lo += width
    z = ALPHA * x + 0.5 * acc
    o_ref[...] = _layer_norm(z, g_ref[...], b_ref[...])


def _ffn_ln(x2d, wg, wu, wd, g, b):
    t = x2d.shape[0]
    const = lambda i: (0, 0)
    return pl.pallas_call(
        _ffn_ln_kernel,
        grid=(t // FFN_TM,),
        in_specs=[
            pl.BlockSpec((FFN_TM, D_MODEL), lambda i: (i, 0)),
            pl.BlockSpec((D_MODEL, D_FF), const, pipeline_mode=pl.Buffered(1)),
            pl.BlockSpec((D_MODEL, D_FF), const, pipeline_mode=pl.Buffered(1)),
            pl.BlockSpec((D_FF, D_MODEL), const, pipeline_mode=pl.Buffered(1)),
            pl.BlockSpec((1, D_MODEL), const),
            pl.BlockSpec((1, D_MODEL), const),
        ],
        out_specs=pl.BlockSpec((FFN_TM, D_MODEL), lambda i: (i, 0)),
        out_shape=jax.ShapeDtypeStruct((t, D_MODEL), F32),
        compiler_params=_params("parallel"),
        name="ffn_ln",
    )(x2d, wg.astype(BF16), wu.astype(BF16), wd.astype(BF16), g.reshape(1, -1), b.reshape(1, -1))


PROJ_TM = 512
MASK_LANES = HEAD_PAD - HEAD_DIM


def _proj_kernel(x_ref, w_ref, wvt_ref, bg_ref, u_ref, q_ref, kc_ref, vc_ref, ks_ref, kw_ref,
                 gt_ref, vst_ref, vwt_ref):
    xb = x_ref[0].astype(BF16)

    def mm(lo, n):
        return _dot(xb, w_ref[:, lo:lo + n])

    u_ref[0] = mm(_C_U, POOL_WIDTH)
    q_ref[0] = (mm(_C_Q, Q_PAD) * (QK_SCALE * LOG2_E)).astype(BF16)
    kc_ref[0] = mm(_C_KC, KV_PAD)
    vc_ref[0] = mm(_C_VC, KV_PAD)
    tpos = pl.program_id(1) * PROJ_TM + lax.broadcasted_iota(jnp.int32, (PROJ_TM, KV_PAD), 0)
    lane = lax.broadcasted_iota(jnp.int32, (PROJ_TM, KV_PAD), 1) & (HEAD_PAD - 1)
    hot = lane == HEAD_DIM + ((tpos >> SLC_BLOCK_LOG2) & (MASK_LANES - 1))
    ks_ref[0] = jnp.where(hot, 1.0, mm(_C_KS, KV_PAD)).astype(BF16)
    kw_ref[0] = mm(_C_KW, KV_PAD).astype(BF16)
    gt_ref[0] = jax.nn.sigmoid(mm(_C_G, KV_PAD) + bg_ref[...])
    chan = lax.broadcasted_iota(jnp.int32, (KV_PAD, PROJ_TM), 0) & (HEAD_PAD - 1)
    vst_ref[0] = jnp.where(chan == ONES_ROW, 1.0, _dot_nt(wvt_ref[0:KV_PAD, :], xb)).astype(BF16)
    vwt_ref[0] = jnp.where(chan == ONES_ROW, 1.0,
                           _dot_nt(wvt_ref[KV_PAD:2 * KV_PAD, :], xb)).astype(BF16)


def _pad_heads(w, n_heads):
    lead = w.shape[:-1]
    w = w.reshape(lead + (n_heads, HEAD_DIM))
    w = jnp.pad(w, [(0, 0)] * len(lead) + [(0, 0), (0, HEAD_PAD - HEAD_DIM)])
    return w.reshape(lead + (n_heads * HEAD_PAD,))


def _pad_gate_cols(w):
    lead = w.shape[:-1]
    per_group = HEADS_PER_GROUP * N_BRANCHES
    w = w.reshape(lead + (N_KV_GROUPS, per_group))
    w = jnp.pad(w, [(0, 0)] * len(lead) + [(0, 0), (0, HEAD_PAD - per_group)])
    return w.reshape(lead + (KV_PAD,))


def _proj(x, w_in, b_gate):
    bsz, s, _ = x.shape
    cuts = np.cumsum([POOL_WIDTH, N_Q_HEADS * HEAD_DIM] + [N_KV_GROUPS * HEAD_DIM] * 6)
    parts = jnp.split(w_in, [int(c) for c in cuts], axis=-1)
    cols = [parts[0], _pad_heads(parts[1], N_Q_HEADS)]
    cols += [_pad_heads(parts[i], N_KV_GROUPS) for i in (2, 3, 4, 6)]
    cols.append(_pad_gate_cols(parts[8]))
    w = jnp.concatenate(cols, axis=-1).astype(BF16)
    wvt = jnp.concatenate([_pad_heads(parts[5], N_KV_GROUPS),
                           _pad_heads(parts[7], N_KV_GROUPS)], axis=-1).T.astype(BF16)
    bg = _pad_gate_cols(b_gate).reshape(1, KV_PAD)

    def tile(width):
        return pl.BlockSpec((1, PROJ_TM, width), lambda b, i: (b, i, 0))

    def out(width, dtype):
        return jax.ShapeDtypeStruct((bsz, s, width), dtype)

    vt_spec = pl.BlockSpec((1, KV_PAD, PROJ_TM), lambda b, i: (b, 0, i))
    vt_out = jax.ShapeDtypeStruct((bsz, KV_PAD, s), BF16)
    return pl.pallas_call(
        _proj_kernel,
        grid=(bsz, s // PROJ_TM),
        in_specs=[
            tile(D_MODEL),
            pl.BlockSpec((D_MODEL, _C_END), lambda b, i: (0, 0), pipeline_mode=pl.Buffered(1)),
            pl.BlockSpec((2 * KV_PAD, D_MODEL), lambda b, i: (0, 0), pipeline_mode=pl.Buffered(1)),
            pl.BlockSpec((1, KV_PAD), lambda b, i: (0, 0)),
        ],
        out_specs=[tile(POOL_WIDTH), tile(Q_PAD), tile(KV_PAD), tile(KV_PAD), tile(KV_PAD),
                   tile(KV_PAD), tile(KV_PAD), vt_spec, vt_spec],
        out_shape=[out(POOL_WIDTH, F32), out(Q_PAD, BF16), out(KV_PAD, F32), out(KV_PAD, F32),
                   out(KV_PAD, BF16), out(KV_PAD, BF16), out(KV_PAD, F32), vt_out, vt_out],
        compiler_params=_params("parallel", "parallel"),
        name="in_proj",
    )(x, w, wvt, bg)


def _gelu_tanh(x):
    c = np.float32(np.sqrt(2.0 / np.pi))
    return x * (0.5 * (1.0 + jnp.tanh(c * (x + 0.044715 * (x * x * x)))))


def _compress_kernel(x_ref, plo_ref, phi_ref, wlo_ref, whi_ref, w2_ref, o_ref, a_scr, b_scr,
                     *, channel_major):
    r = pl.program_id(1)
    nsb = x_ref.shape[1]
    xr = x_ref[0]
    a_scr[r] = _dot((xr + plo_ref[...]).astype(BF16), wlo_ref[...])
    b_scr[r, 0:nsb, :] = _dot((xr + phi_ref[...]).astype(BF16), whi_ref[...])

    @pl.when(r == 0)
    def _():
        b_scr[0, nsb:nsb + 8, :] = jnp.zeros((8, KV_PAD), F32)

    @pl.when(r == CMP_PER_SLC - 1)
    def _():
        for rr in range(CMP_PER_SLC):
            if rr < CMP_PER_SLC - 1:
                h = a_scr[rr] + b_scr[rr + 1, 0:nsb, :]
            else:
                h = a_scr[rr] + b_scr[0, 1:nsb + 1, :]
            act = _gelu_tanh(h).astype(BF16)
            if channel_major:
                o_ref[0, :, rr * nsb:(rr + 1) * nsb] = _dot_nt(w2_ref[...], act).astype(BF16)
            else:
                o_ref[0, rr * nsb:(rr + 1) * nsb, :] = _dot(act, w2_ref[...]).astype(BF16)


def _compress_weights(pos, w1, w2):
    eye_g = jnp.eye(N_KV_GROUPS, dtype=F32)
    w1r = w1.reshape(CMP_BLOCK, HEAD_DIM, HEAD_DIM)
    w1r = jnp.pad(w1r, ((0, 0), (0, HEAD_PAD - HEAD_DIM), (0, HEAD_PAD - HEAD_DIM)))
    w1c = w1r[:, None, :, None, :] * eye_g[None, :, None, :, None]
    w1c = w1c.reshape(CMP_BLOCK, KV_PAD, KV_PAD)
    wlo = w1c[:CMP_STRIDE].reshape(CHUNK_FLAT, KV_PAD).astype(BF16)
    whi = w1c[CMP_STRIDE:].reshape(CHUNK_FLAT, KV_PAD).astype(BF16)
    posp = jnp.pad(pos, ((0, 0), (0, HEAD_PAD - HEAD_DIM)))
    posp = jnp.tile(posp[:, None, :], (1, N_KV_GROUPS, 1))
    plo = posp[:CMP_STRIDE].reshape(1, CHUNK_FLAT)
    phi = posp[CMP_STRIDE:].reshape(1, CHUNK_FLAT)
    w2p = jnp.pad(w2, ((0, HEAD_PAD - HEAD_DIM), (0, HEAD_PAD - HEAD_DIM)))
    w2c = (w2p[None, :, None, :] * eye_g[:, None, :, None]).reshape(KV_PAD, KV_PAD).astype(BF16)
    return plo, phi, wlo, whi, w2c


def _compress(kv, pos, w1, w2, channel_major):
    bsz, s, _ = kv.shape
    nsb = s // SLC_BLOCK
    ncp = CMP_PER_SLC * nsb
    plo, phi, wlo, whi, w2c = _compress_weights(pos, w1, w2)
    if channel_major:
        w2c = w2c.T
    out_dims = (KV_PAD, ncp) if channel_major else (ncp, KV_PAD)
    x = kv.reshape(bsz, nsb, CMP_PER_SLC * CHUNK_FLAT)
    const = lambda b, r: (0, 0)
    return pl.pallas_call(
        functools.partial(_compress_kernel, channel_major=channel_major),
        grid=(bsz, CMP_PER_SLC),
        in_specs=[
            pl.BlockSpec((1, nsb, CHUNK_FLAT), lambda b, r: (b, 0, r)),
            pl.BlockSpec((1, CHUNK_FLAT), const),
            pl.BlockSpec((1, CHUNK_FLAT), const),
            pl.BlockSpec((CHUNK_FLAT, KV_PAD), const),
            pl.BlockSpec((CHUNK_FLAT, KV_PAD), const),
            pl.BlockSpec((KV_PAD, KV_PAD), const),
        ],
        out_specs=pl.BlockSpec((1,) + out_dims, lambda b, r: (b, 0, 0)),
        out_shape=jax.ShapeDtypeStruct((bsz,) + out_dims, BF16),
        scratch_shapes=[pltpu.VMEM((CMP_PER_SLC, nsb, KV_PAD), F32),
                        pltpu.VMEM((CMP_PER_SLC, nsb + 8, KV_PAD), F32)],
        compiler_params=_params("parallel", "arbitrary"),
        name="compress",
    )(x, plo, phi, wlo, whi, w2c)


TQ = 128
COLS = HEADS_PER_GROUP * TQ


def _load_q(q_ref):
    return jnp.concatenate(
        [q_ref[0, :, h * HEAD_PAD:(h + 1) * HEAD_PAD] for h in range(HEADS_PER_GROUP)], axis=0)


def _per_head(row):
    return jnp.concatenate([row] * HEADS_PER_GROUP, axis=1)


def _store_gated(o_ref, g_ref, o_t, branch):
    tq = o_ref.shape[1]
    for h in range(HEADS_PER_GROUP):
        c = h * N_BRANCHES + branch
        gate = g_ref[0, :, c:c + 1]
        o_ref[0, :, h * HEAD_PAD:(h + 1) * HEAD_PAD] = o_t[:, h * tq:(h + 1) * tq].T * gate


def _attn_specs(tq=TQ):
    q_spec = pl.BlockSpec((1, tq, GROUP_Q), lambda b, g, i: (b, i, g))
    g_spec = pl.BlockSpec((1, tq, HEAD_PAD), lambda b, g, i: (b, i, g))
    o_spec = pl.BlockSpec((1, tq, GROUP_Q), lambda b, g, i: (b, i, g))
    return q_spec, g_spec, o_spec


def _cmp_kernel(q_ref, kc_ref, vct_ref, g_ref, cend_ref, place_ref, o_ref, selq_ref):
    nsb = kc_ref.shape[1] // CMP_PER_SLC
    t0 = pl.program_id(2) * TQ
    half = nsb // 2
    if half % LANES == 0:
        @pl.when(t0 + TQ <= half * SLC_BLOCK)
        def _():
            _cmp_body(q_ref, kc_ref, vct_ref, g_ref, cend_ref, place_ref, o_ref, selq_ref, half)

        @pl.when(t0 + TQ > half * SLC_BLOCK)
        def _():
            _cmp_body(q_ref, kc_ref, vct_ref, g_ref, cend_ref, place_ref, o_ref, selq_ref, nsb)
    else:
        _cmp_body(q_ref, kc_ref, vct_ref, g_ref, cend_ref, place_ref, o_ref, selq_ref, nsb)


def _cmp_body(q_ref, kc_ref, vct_ref, g_ref, cend_ref, place_ref, o_ref, selq_ref, nvis):
    nsb_all = kc_ref.shape[1] // CMP_PER_SLC
    nsb = nvis
    t0 = pl.program_id(2) * TQ

    def slabs(ref_rows):
        if nvis == nsb_all:
            return ref_rows(0, CMP_PER_SLC * nsb_all)
        return jnp.concatenate([ref_rows(r * nsb_all, nvis) for r in range(CMP_PER_SLC)], axis=0)

    kc = slabs(lambda lo, n: kc_ref[0, lo:lo + n, :])
    cend = slabs(lambda lo, n: cend_ref[lo:lo + n, :])
    if nvis == nsb_all:
        vct = vct_ref[0]
    else:
        vct = jnp.concatenate([vct_ref[0, :, r * nsb_all:r * nsb_all + nvis]
                               for r in range(CMP_PER_SLC)], axis=1)

    s = _dot_nt(kc, _load_q(q_ref))
    s = s + _per_head(jnp.where(cend <= t0, 0.0, NEG_BIG))
    m = jnp.max(s, axis=0, keepdims=True)
    e = jnp.exp2(s - m)
    l = jnp.sum(e, axis=0, keepdims=True)
    tcol = t0 + lax.broadcasted_iota(jnp.int32, (1, TQ), 1)
    any_visible = _per_head(jnp.where(tcol >= CMP_BLOCK - 1, 1.0, 0.0))
    p = e * (any_visible / jnp.maximum(l, 1e-30))

    o_t = _dot(vct, p.astype(BF16))
    _store_gated(o_ref, g_ref, o_t, 0)

    imp = p[:, 0:TQ]
    for h in range(1, HEADS_PER_GROUP):
        imp = imp + p[:, h * TQ:(h + 1) * TQ]
    p0, p1, p2, p3 = (imp[r * nsb:(r + 1) * nsb, :] for r in range(CMP_PER_SLC))
    blk = lax.broadcasted_iota(jnp.int32, (nsb, TQ), 0)
    p3_prev = jnp.where(blk == 0, 0.0, pltpu.roll(p3, 1, axis=0))
    imp_slc = 0.5 * p3_prev + p0 + p1 + p2 + 0.5 * p3

    jt = (t0 + lax.broadcasted_iota(jnp.int32, (nsb, TQ), 1)) >> SLC_BLOCK_LOG2
    forced = (blk == 0) | (blk == jt) | (blk == jt - 1)
    free = (blk <= jt) & jnp.logical_not(forced)
    score = jnp.where(free, imp_slc, NEG_BIG)

    def pick(_, sc):
        mx = jnp.max(sc, axis=0, keepdims=True)
        first = jnp.min(jnp.where(sc == mx, blk, nsb), axis=0, keepdims=True)
        return jnp.where(blk == first, -jnp.inf, sc)

    picked = lax.fori_loop(0, min(N_SELECT - N_FORCED, nsb), pick, score) == -jnp.inf
    swept = ((picked & free) | forced) & (blk < (t0 >> SLC_BLOCK_LOG2))
    bias = jnp.where(swept, 0.0, NEG_BIG)
    if nvis < nsb_all:
        bias = jnp.concatenate([bias, jnp.full((nsb_all - nvis, TQ), NEG_BIG, F32)], axis=0)
    selq_ref[0, 0] = _dot(bias.T.astype(BF16), place_ref[...]).astype(BF16)


def _mask_lane_placement(nsb):
    j = np.arange(nsb)
    place = np.zeros((nsb, (nsb // MASK_LANES) * HEAD_PAD), np.float32)
    place[j, (j // MASK_LANES) * HEAD_PAD + HEAD_DIM + j % MASK_LANES] = 1.0
    return jnp.asarray(place, BF16)


def _cmp_end_minus_token(nsb):
    row = np.arange(CMP_PER_SLC * nsb)
    end = (row % nsb) * SLC_BLOCK + (row // nsb) * CMP_STRIDE + CMP_BLOCK - 1
    return jnp.asarray(end[:, None] - np.arange(TQ)[None, :], jnp.int32)


def _cmp_attention(q, kcmp, vcmp_t, gates):
    bsz, s, _ = q.shape
    nsb = s // SLC_BLOCK
    ncp = CMP_PER_SLC * nsb
    selq_w = (nsb // MASK_LANES) * HEAD_PAD
    q_spec, g_spec, o_spec = _attn_specs()
    const = lambda b, g, i: (0, 0)
    return pl.pallas_call(
        _cmp_kernel,
        grid=(bsz, N_KV_GROUPS, s // TQ),
        in_specs=[q_spec,
                  pl.BlockSpec((1, ncp, HEAD_PAD), lambda b, g, i: (b, 0, g)),
                  pl.BlockSpec((1, HEAD_PAD, ncp), lambda b, g, i: (b, g, 0)),
                  g_spec,
                  pl.BlockSpec((ncp, TQ), const),
                  pl.BlockSpec((nsb, selq_w), const)],
        out_specs=[o_spec, pl.BlockSpec((1, 1, TQ, selq_w), lambda b, g, i: (b, g, i, 0))],
        out_shape=[jax.ShapeDtypeStruct((bsz, s, Q_PAD), F32),
                   jax.ShapeDtypeStruct((bsz, N_KV_GROUPS, s, selq_w), BF16)],
        compiler_params=_params("parallel", "parallel", "arbitrary"),
        name="cmp_attn_topk",
    )(q, kcmp, vcmp_t, gates, _cmp_end_minus_token(nsb), _mask_lane_placement(nsb))


SLC_TK = 512
SUPER_KEYS = MASK_LANES * SLC_BLOCK
TILES_PER_SUPER = SUPER_KEYS // SLC_TK


def _slc_kernel(q_ref, k_ref, vt_ref, selq_ref, g_ref, o_ref, qm_scr, s_scr):
    n_super = selq_ref.shape[3] // HEAD_PAD
    t0 = pl.multiple_of(pl.program_id(2) * TQ, TQ)
    n_full = t0 // SUPER_KEYS
    tail_tiles = (t0 - n_full * SUPER_KEYS + SLC_TK - 1) // SLC_TK

    for st in range(n_super):
        @pl.when(st * SUPER_KEYS < t0)
        def _():
            slab = selq_ref[0, 0, :, st * HEAD_PAD:(st + 1) * HEAD_PAD]
            for h in range(HEADS_PER_GROUP):
                qm_scr[st, h * TQ:(h + 1) * TQ, :] = (
                    q_ref[0, :, h * HEAD_PAD:(h + 1) * HEAD_PAD] + slab)

    s = _dot_nt(k_ref[0, pl.ds(t0, TQ), :], _load_q(q_ref))
    kk = lax.broadcasted_iota(jnp.int32, (TQ, TQ), 0)
    tt = lax.broadcasted_iota(jnp.int32, (TQ, TQ), 1)
    s = s + _per_head(jnp.where(kk <= tt, 0.0, NEG_BIG))
    m = jnp.max(s, axis=0, keepdims=True)
    acc = _dot(vt_ref[0, :, pl.ds(t0, TQ)], jnp.exp2(s - m).astype(BF16))

    def sweep(st, carry, n_keys):
        m, acc = carry
        k0 = pl.multiple_of(st * SUPER_KEYS, SUPER_KEYS)
        sc = _dot_nt(k_ref[0, pl.ds(k0, n_keys), :], qm_scr[st])
        s_scr[0:n_keys, :] = sc
        m_new = jnp.maximum(m, jnp.max(sc, axis=0, keepdims=True))
        acc = jnp.exp2(m - m_new) * acc
        for c in range(0, n_keys, SLC_TK):
            p = jnp.exp2(s_scr[c:c + SLC_TK, :] - m_new).astype(BF16)
            acc = acc + _dot(vt_ref[0, :, pl.ds(k0 + c, SLC_TK)], p)
        return m_new, acc

    carry = lax.fori_loop(0, n_full, functools.partial(sweep, n_keys=SUPER_KEYS), (m, acc))
    branches = [lambda c: c] + [functools.partial(sweep, n_full, n_keys=n * SLC_TK)
                                for n in range(1, TILES_PER_SUPER + 1)]
    _, acc = lax.switch(tail_tiles, branches, carry)
    _store_gated(o_ref, g_ref, acc / jnp.maximum(acc[ONES_ROW:ONES_ROW + 1, :], 1e-30), 1)


def _slc_attention(q, ks, vs_t, selq, gates):
    bsz, s, _ = q.shape
    n_super = s // SUPER_KEYS
    q_spec, g_spec, o_spec = _attn_specs()
    return pl.pallas_call(
        _slc_kernel,
        grid=(bsz, N_KV_GROUPS, s // TQ),
        in_specs=[q_spec,
                  pl.BlockSpec((1, s, HEAD_PAD), lambda b, g, i: (b, 0, g)),
                  pl.BlockSpec((1, HEAD_PAD, s), lambda b, g, i: (b, g, 0)),
                  pl.BlockSpec((1, 1, TQ, n_super * HEAD_PAD), lambda b, g, i: (b, g, i, 0)),
                  g_spec],
        out_specs=o_spec,
        out_shape=jax.ShapeDtypeStruct((bsz, s, Q_PAD), F32),
        scratch_shapes=[pltpu.VMEM((n_super, COLS, HEAD_PAD), BF16),
                        pltpu.VMEM((SUPER_KEYS, COLS), F32)],
        compiler_params=_params("parallel", "parallel", "arbitrary"),
        name="slc_attn",
    )(q, ks, vs_t, selq, gates)


WIN_TQ = 256
WIN_KEYS = WINDOW + WIN_TQ


def _win_kernel(q_ref, k_ref, vt_ref, g_ref, o_ref):
    t0 = pl.program_id(2) * WIN_TQ
    k0 = pl.multiple_of(jnp.maximum(t0 - WINDOW, 0), WIN_TQ)
    s = _dot_nt(k_ref[0, pl.ds(k0, WIN_KEYS), :], _load_q(q_ref))
    diff = ((t0 + lax.broadcasted_iota(jnp.int32, (WIN_KEYS, WIN_TQ), 1))
            - (k0 + lax.broadcasted_iota(jnp.int32, (WIN_KEYS, WIN_TQ), 0)))
    s = s + _per_head(jnp.where((diff >= 0) & (diff < WINDOW), 0.0, NEG_BIG))
    m = jnp.max(s, axis=0, keepdims=True)
    e = jnp.exp2(s - m).astype(BF16)
    o_t = _dot(vt_ref[0, :, pl.ds(k0, WIN_KEYS)], e)
    _store_gated(o_ref, g_ref, o_t / jnp.maximum(o_t[ONES_ROW:ONES_ROW + 1, :], 1e-30), 2)


def _win_attention(q, kw, vw_t, gates):
    bsz, s, _ = q.shape
    q_spec, g_spec, o_spec = _attn_specs(WIN_TQ)
    return pl.pallas_call(
        _win_kernel,
        grid=(bsz, N_KV_GROUPS, s // WIN_TQ),
        in_specs=[q_spec,
                  pl.BlockSpec((1, s, HEAD_PAD), lambda b, g, i: (b, 0, g)),
                  pl.BlockSpec((1, HEAD_PAD, s), lambda b, g, i: (b, g, 0)),
                  g_spec],
        out_specs=o_spec,
        out_shape=jax.ShapeDtypeStruct((bsz, s, Q_PAD), F32),
        compiler_params=_params("parallel", "parallel", "arbitrary"),
        name="win_attn",
    )(q, kw, vw_t, gates)


MIX_TM = 512
POOL_HALO = 16


def _mix_out_kernel(x_ref, u_ref, halo_ref, oc_ref, os_ref, ow_ref, pw_ref, ps_ref,
                    wo_ref, g_ref, b_ref, o_ref, ext_scr):
    i = pl.program_id(1)
    u = u_ref[0]
    halo = jnp.where(i == 0, 0.0, halo_ref[0])
    ext_scr[0:POOL_HALO, :] = halo
    ext_scr[POOL_HALO:POOL_HALO + MIX_TM, :] = u

    lane = lax.broadcasted_iota(jnp.int32, (MIX_TM, POOL_WIDTH), 1)
    tpos = i * MIX_TM + lax.broadcasted_iota(jnp.int32, (MIX_TM, POOL_WIDTH), 0)
    grp = lane >> (POOL_GROUP_DIM.bit_length() - 1)
    run = u
    win_sum = jnp.zeros_like(u)
    cnt = jnp.zeros_like(u)
    done = 1
    for gidx, w in enumerate(POOL_WINDOWS):
        for kback in range(done, w):
            run = run + ext_scr[POOL_HALO - kback:POOL_HALO - kback + MIX_TM, :]
        done = w
        win_sum = jnp.where(grp == gidx, run, win_sum)
        cnt = jnp.where(grp == gidx, jnp.minimum(tpos + 1, w).astype(F32), cnt)
    pooled = win_sum / cnt - u
    mixed = _dot(pooled.astype(BF16), pw_ref[...]) * ps_ref[...]

    y_nsa = oc_ref[0] + os_ref[0] + ow_ref[0]
    y = (_dot(mixed.astype(BF16), wo_ref[0:POOL_WIDTH, :])
         + _dot(y_nsa.astype(BF16), wo_ref[POOL_WIDTH:, :]))
    z = ALPHA * x_ref[0] + y
    o_ref[0] = _layer_norm(z, g_ref[...], b_ref[...])


def _mix_out(x, u, o_cmp, o_slc, o_win, pool_w, pool_scale, w_out, g, b):
    bsz, s, _ = x.shape
    n_grp = len(POOL_WINDOWS)
    eye = jnp.eye(n_grp, dtype=F32)
    pw = (pool_w[:, :, None, :] * eye[:, None, :, None]).reshape(POOL_WIDTH, POOL_WIDTH).astype(BF16)
    wo_nsa = w_out[POOL_WIDTH:].reshape(N_Q_HEADS, HEAD_DIM, D_MODEL)
    wo_nsa = jnp.pad(wo_nsa, ((0, 0), (0, HEAD_PAD - HEAD_DIM), (0, 0))).reshape(Q_PAD, D_MODEL)
    wo = jnp.concatenate([w_out[:POOL_WIDTH], wo_nsa], axis=0).astype(BF16)

    def tile(width):
        return pl.BlockSpec((1, MIX_TM, width), lambda bb, i: (bb, i, 0))

    halo_blocks = MIX_TM // POOL_HALO
    const = lambda bb, i: (0, 0)
    return pl.pallas_call(
        _mix_out_kernel,
        grid=(bsz, s // MIX_TM),
        in_specs=[
            tile(D_MODEL), tile(POOL_WIDTH),
            pl.BlockSpec((1, POOL_HALO, POOL_WIDTH),
                         lambda bb, i: (bb, jnp.maximum(i * halo_blocks - 1, 0), 0)),
            tile(Q_PAD), tile(Q_PAD), tile(Q_PAD),
            pl.BlockSpec((POOL_WIDTH, POOL_WIDTH), const),
            pl.BlockSpec((1, POOL_WIDTH), const),
            pl.BlockSpec((POOL_WIDTH + Q_PAD, D_MODEL), const),
            pl.BlockSpec((1, D_MODEL), const),
            pl.BlockSpec((1, D_MODEL), const),
        ],
        out_specs=tile(D_MODEL),
        out_shape=jax.ShapeDtypeStruct((bsz, s, D_MODEL), F32),
        scratch_shapes=[pltpu.VMEM((POOL_HALO + MIX_TM, POOL_WIDTH), F32)],
        compiler_params=_params("parallel", "arbitrary"),
        name="mix_out_ln",
    )(x, u, u, o_cmp, o_slc, o_win, pw, pool_scale.reshape(1, -1), wo,
      g.reshape(1, -1), b.reshape(1, -1))


def kernel(x, ln1_g, ln1_b, ffn1_w_gate, ffn1_w_up, ffn1_w_down, w_in, b_gate, pool_w, pool_scale, cmp_pos_k, cmp_k_w1, cmp_k_w2, cmp_pos_v, cmp_v_w1, cmp_v_w2, w_out, ln2_g, ln2_b, ffn2_w_gate, ffn2_w_up, ffn2_w_down, ln3_g, ln3_b):
    bsz, s, d = x.shape
    assert d == D_MODEL and s % max(SUPER_KEYS, FFN_TM, MIX_TM, PROJ_TM) == 0
    for l in range(DEPTH):
        x = _ffn_ln(x.reshape(bsz * s, d), ffn1_w_gate[l], ffn1_w_up[l], ffn1_w_down[l],
                    ln1_g[l], ln1_b[l]).reshape(bsz, s, d)
        u, q, kc, vc, ks, kw, gates, vs_t, vw_t = _proj(x, w_in[l], b_gate[l])
        kcmp = _compress(kc, cmp_pos_k[l], cmp_k_w1[l], cmp_k_w2[l], channel_major=False)
        vcmp_t = _compress(vc, cmp_pos_v[l], cmp_v_w1[l], cmp_v_w2[l], channel_major=True)
        o_cmp, selq = _cmp_attention(q, kcmp, vcmp_t, gates)
        o_slc = _slc_attention(q, ks, vs_t, selq, gates)
        o_win = _win_attention(q, kw, vw_t, gates)
        x = _mix_out(x, u, o_cmp, o_slc, o_win, pool_w[l], pool_scale[l], w_out[l],
                     ln2_g[l], ln2_b[l])
        x = _ffn_ln(x.reshape(bsz * s, d), ffn2_w_gate[l], ffn2_w_up[l], ffn2_w_down[l],
                    ln3_g[l], ln3_b[l]).reshape(bsz, s, d)
    return x
```

```python
import functools

import numpy as np
import jax
import jax.numpy as jnp
from jax import lax
from jax.experimental import pallas as pl
from jax.experimental.pallas import tpu as pltpu

D_MODEL = 1024
DEPTH = 2
POOL_WIDTH = 256
POOL_WINDOWS = (2, 4, 8, 16)
POOL_GROUP_DIM = 64
N_Q_HEADS = 8
HEAD_DIM = 96
N_KV_GROUPS = 2
HEADS_PER_GROUP = 4
N_BRANCHES = 3
CMP_STRIDE = 16
CMP_BLOCK = 32
SLC_BLOCK = 64
SLC_BLOCK_LOG2 = 6
N_SELECT = 16
N_FORCED = 3
WINDOW = 512
D_FF = 2816
ALPHA = (2.0 * DEPTH) ** 0.25
LN_EPS = 1e-5
NEG_BIG = -1e30
SEL_BIG = 1e30
QK_SCALE = HEAD_DIM ** -0.5
LOG2_E = 1.4426950408889634
ONES_ROW = HEAD_DIM

LANES = 128
HEAD_PAD = LANES
Q_PAD = N_Q_HEADS * HEAD_PAD
KV_PAD = N_KV_GROUPS * HEAD_PAD
GROUP_Q = HEADS_PER_GROUP * HEAD_PAD
CMP_PER_SLC = SLC_BLOCK // CMP_STRIDE
CHUNK_FLAT = CMP_STRIDE * KV_PAD

VMEM_LIMIT = 56 * 1024 * 1024

F32 = jnp.float32
BF16 = jnp.bfloat16

_C_U = 0
_C_Q = _C_U + POOL_WIDTH
_C_KC = _C_Q + Q_PAD
_C_VC = _C_KC + KV_PAD
_C_KS = _C_VC + KV_PAD
_C_KW = _C_KS + KV_PAD
_C_G = _C_KW + KV_PAD
_C_END = _C_G + KV_PAD


def _params(*sem):
    return pltpu.CompilerParams(dimension_semantics=sem, vmem_limit_bytes=VMEM_LIMIT)


def _layer_norm(z, g, b):
    mu = jnp.mean(z, axis=-1, keepdims=True)
    zc = z - mu
    var = jnp.mean(zc * zc, axis=-1, keepdims=True)
    return zc * lax.rsqrt(var + LN_EPS) * g + b


def _dot(a, b):
    return jnp.dot(a, b, preferred_element_type=F32)


def _dot_nt(a, b):
    return lax.dot_general(a, b, (((1,), (1,)), ((), ())), preferred_element_type=F32)


FFN_TM = 1024
MXU_TILE = 256
FFN_CHUNKS = (6 * MXU_TILE, 5 * MXU_TILE)
assert sum(FFN_CHUNKS) == D_FF


def _ffn_ln_kernel(x_ref, wg_ref, wu_ref, wd_ref, g_ref, b_ref, o_ref):
    x = x_ref[...]
    xb = x.astype(BF16)
    acc = None
    lo = 0
    for width in FFN_CHUNKS:
        hg = _dot(xb, wg_ref[:, lo:lo + width])
        hu = _dot(xb, wu_ref[:, lo:lo + width])
        h = (hg * jax.nn.sigmoid(hg)) * hu
        part = _dot(h.astype(BF16), wd_ref[lo:lo + width, :])
        acc = part if acc is None else acc + part
        lo += width
    z = ALPHA * x + 0.5 * acc
    o_ref[...] = _layer_norm(z, g_ref[...], b_ref[...])


def _ffn_ln(x2d, wg, wu, wd, g, b):
    t = x2d.shape[0]
    const = lambda i: (0, 0)
    return pl.pallas_call(
        _ffn_ln_kernel,
        grid=(t // FFN_TM,),
        in_specs=[
            pl.BlockSpec((FFN_TM, D_MODEL), lambda i: (i, 0)),
            pl.BlockSpec((D_MODEL, D_FF), const, pipeline_mode=pl.Buffered(1)),
            pl.BlockSpec((D_MODEL, D_FF), const, pipeline_mode=pl.Buffered(1)),
            pl.BlockSpec((D_FF, D_MODEL), const, pipeline_mode=pl.Buffered(1)),
            pl.BlockSpec((1, D_MODEL), const),
            pl.BlockSpec((1, D_MODEL), const),
        ],
        out_specs=pl.BlockSpec((FFN_TM, D_MODEL), lambda i: (i, 0)),
        out_shape=jax.ShapeDtypeStruct((t, D_MODEL), F32),
        compiler_params=_params("parallel"),
        name="ffn_ln",
    )(x2d, wg.astype(BF16), wu.astype(BF16), wd.astype(BF16), g.reshape(1, -1), b.reshape(1, -1))


PROJ_TM = 512
MASK_LANES = HEAD_PAD - HEAD_DIM


def _proj_kernel(x_ref, w_ref, wvt_ref, bg_ref, u_ref, q_ref, kc_ref, vc_ref, ks_ref, kw_ref,
                 gt_ref, vst_ref, vwt_ref):
    xb = x_ref[0].astype(BF16)

    def mm(lo, n):
        return _dot(xb, w_ref[:, lo:lo + n])

    u_ref[0] = mm(_C_U, POOL_WIDTH)
    q_ref[0] = (mm(_C_Q, Q_PAD) * (QK_SCALE * LOG2_E)).astype(BF16)
    kc_ref[0] = mm(_C_KC, KV_PAD)
    vc_ref[0] = mm(_C_VC, KV_PAD)
    tpos = pl.program_id(1) * PROJ_TM + lax.broadcasted_iota(jnp.int32, (PROJ_TM, KV_PAD), 0)
    lane = lax.broadcasted_iota(jnp.int32, (PROJ_TM, KV_PAD), 1) & (HEAD_PAD - 1)
    hot = lane == HEAD_DIM + ((tpos >> SLC_BLOCK_LOG2) & (MASK_LANES - 1))
    ks_ref[0] = jnp.where(hot, 1.0, mm(_C_KS, KV_PAD)).astype(BF16)
    kw_ref[0] = mm(_C_KW, KV_PAD).astype(BF16)
    gt_ref[0] = jax.nn.sigmoid(mm(_C_G, KV_PAD) + bg_ref[...])
    chan = lax.broadcasted_iota(jnp.int32, (KV_PAD, PROJ_TM), 0) & (HEAD_PAD - 1)
    vst_ref[0] = jnp.where(chan == ONES_ROW, 1.0, _dot_nt(wvt_ref[0:KV_PAD, :], xb)).astype(BF16)
    vwt_ref[0] = jnp.where(chan == ONES_ROW, 1.0,
                           _dot_nt(wvt_ref[KV_PAD:2 * KV_PAD, :], xb)).astype(BF16)


def _pad_heads(w, n_heads):
    lead = w.shape[:-1]
    w = w.reshape(lead + (n_heads, HEAD_DIM))
    w = jnp.pad(w, [(0, 0)] * len(lead) + [(0, 0), (0, HEAD_PAD - HEAD_DIM)])
    return w.reshape(lead + (n_heads * HEAD_PAD,))


def _pad_gate_cols(w):
    lead = w.shape[:-1]
    per_group = HEADS_PER_GROUP * N_BRANCHES
    w = w.reshape(lead + (N_KV_GROUPS, per_group))
    w = jnp.pad(w, [(0, 0)] * len(lead) + [(0, 0), (0, HEAD_PAD - per_group)])
    return w.reshape(lead + (KV_PAD,))


def _proj(x, w_in, b_gate):
    bsz, s, _ = x.shape
    cuts = np.cumsum([POOL_WIDTH, N_Q_HEADS * HEAD_DIM] + [N_KV_GROUPS * HEAD_DIM] * 6)
    parts = jnp.split(w_in, [int(c) for c in cuts], axis=-1)
    cols = [parts[0], _pad_heads(parts[1], N_Q_HEADS)]
    cols += [_pad_heads(parts[i], N_KV_GROUPS) for i in (2, 3, 4, 6)]
    cols.append(_pad_gate_cols(parts[8]))
    w = jnp.concatenate(cols, axis=-1).astype(BF16)
    wvt = jnp.concatenate([_pad_heads(parts[5], N_KV_GROUPS),
                           _pad_heads(parts[7], N_KV_GROUPS)], axis=-1).T.astype(BF16)
    bg = _pad_gate_cols(b_gate).reshape(1, KV_PAD)

    def tile(width):
        return pl.BlockSpec((1, PROJ_TM, width), lambda b, i: (b, i, 0))

    def out(width, dtype):
        return jax.ShapeDtypeStruct((bsz, s, width), dtype)

    vt_spec = pl.BlockSpec((1, KV_PAD, PROJ_TM), lambda b, i: (b, 0, i))
    vt_out = jax.ShapeDtypeStruct((bsz, KV_PAD, s), BF16)
    return pl.pallas_call(
        _proj_kernel,
        grid=(bsz, s // PROJ_TM),
        in_specs=[
            tile(D_MODEL),
            pl.BlockSpec((D_MODEL, _C_END), lambda b, i: (0, 0), pipeline_mode=pl.Buffered(1)),
            pl.BlockSpec((2 * KV_PAD, D_MODEL), lambda b, i: (0, 0), pipeline_mode=pl.Buffered(1)),
            pl.BlockSpec((1, KV_PAD), lambda b, i: (0, 0)),
        ],
        out_specs=[tile(POOL_WIDTH), tile(Q_PAD), tile(KV_PAD), tile(KV_PAD), tile(KV_PAD),
                   tile(KV_PAD), tile(KV_PAD), vt_spec, vt_spec],
        out_shape=[out(POOL_WIDTH, F32), out(Q_PAD, BF16), out(KV_PAD, F32), out(KV_PAD, F32),
                   out(KV_PAD, BF16), out(KV_PAD, BF16), out(KV_PAD, F32), vt_out, vt_out],
        compiler_params=_params("parallel", "parallel"),
        name="in_proj",
    )(x, w, wvt, bg)


def _gelu_tanh(x):
    c = np.float32(np.sqrt(2.0 / np.pi))
    return x * (0.5 * (1.0 + jnp.tanh(c * (x + 0.044715 * (x * x * x)))))


def _compress_kernel(x_ref, plo_ref, phi_ref, wlo_ref, whi_ref, w2_ref, o_ref, a_scr, b_scr,
                     *, channel_major):
    r = pl.program_id(1)
    nsb = x_ref.shape[1]
    xr = x_ref[0]
    a_scr[r] = _dot((xr + plo_ref[...]).astype(BF16), wlo_ref[...])
    b_scr[r, 0:nsb, :] = _dot((xr + phi_ref[...]).astype(BF16), whi_ref[...])

    @pl.when(r == 0)
    def _():
        b_scr[0, nsb:nsb + 8, :] = jnp.zeros((8, KV_PAD), F32)

    @pl.when(r == CMP_PER_SLC - 1)
    def _():
        for rr in range(CMP_PER_SLC):
            if rr < CMP_PER_SLC - 1:
                h = a_scr[rr] + b_scr[rr + 1, 0:nsb, :]
            else:
                h = a_scr[rr] + b_scr[0, 1:nsb + 1, :]
            act = _gelu_tanh(h).astype(BF16)
            if channel_major:
                o_ref[0, :, rr * nsb:(rr + 1) * nsb] = _dot_nt(w2_ref[...], act).astype(BF16)
            else:
                o_ref[0, rr * nsb:(rr + 1) * nsb, :] = _dot(act, w2_ref[...]).astype(BF16)


def _compress_weights(pos, w1, w2):
    eye_g = jnp.eye(N_KV_GROUPS, dtype=F32)
    w1r = w1.reshape(CMP_BLOCK, HEAD_DIM, HEAD_DIM)
    w1r = jnp.pad(w1r, ((0, 0), (0, HEAD_PAD - HEAD_DIM), (0, HEAD_PAD - HEAD_DIM)))
    w1c = w1r[:, None, :, None, :] * eye_g[None, :, None, :, None]
    w1c = w1c.reshape(CMP_BLOCK, KV_PAD, KV_PAD)
    wlo = w1c[:CMP_STRIDE].reshape(CHUNK_FLAT, KV_PAD).astype(BF16)
    whi = w1c[CMP_STRIDE:].reshape(CHUNK_FLAT, KV_PAD).astype(BF16)
    posp = jnp.pad(pos, ((0, 0), (0, HEAD_PAD - HEAD_DIM)))
    posp = jnp.tile(posp[:, None, :], (1, N_KV_GROUPS, 1))
    plo = posp[:CMP_STRIDE].reshape(1, CHUNK_FLAT)
    phi = posp[CMP_STRIDE:].reshape(1, CHUNK_FLAT)
    w2p = jnp.pad(w2, ((0, HEAD_PAD - HEAD_DIM), (0, HEAD_PAD - HEAD_DIM)))
    w2c = (w2p[None, :, None, :] * eye_g[:, None, :, None]).reshape(KV_PAD, KV_PAD).astype(BF16)
    return plo, phi, wlo, whi, w2c


def _compress(kv, pos, w1, w2, channel_major):
    bsz, s, _ = kv.shape
    nsb = s // SLC_BLOCK
    ncp = CMP_PER_SLC * nsb
    plo, phi, wlo, whi, w2c = _compress_weights(pos, w1, w2)
    if channel_major:
        w2c = w2c.T
    out_dims = (KV_PAD, ncp) if channel_major else (ncp, KV_PAD)
    x = kv.reshape(bsz, nsb, CMP_PER_SLC * CHUNK_FLAT)
    const = lambda b, r: (0, 0)
    return pl.pallas_call(
        functools.partial(_compress_kernel, channel_major=channel_major),
        grid=(bsz, CMP_PER_SLC),
        in_specs=[
            pl.BlockSpec((1, nsb, CHUNK_FLAT), lambda b, r: (b, 0, r)),
            pl.BlockSpec((1, CHUNK_FLAT), const),
            pl.BlockSpec((1, CHUNK_FLAT), const),
            pl.BlockSpec((CHUNK_FLAT, KV_PAD), const),
            pl.BlockSpec((CHUNK_FLAT, KV_PAD), const),
            pl.BlockSpec((KV_PAD, KV_PAD), const),
        ],
        out_specs=pl.BlockSpec((1,) + out_dims, lambda b, r: (b, 0, 0)),
        out_shape=jax.ShapeDtypeStruct((bsz,) + out_dims, BF16),
        scratch_shapes=[pltpu.VMEM((CMP_PER_SLC, nsb, KV_PAD), F32),
                        pltpu.VMEM((CMP_PER_SLC, nsb + 8, KV_PAD), F32)],
        compiler_params=_params("parallel", "arbitrary"),
        name="compress",
    )(x, plo, phi, wlo, whi, w2c)


TQ = 128
COLS = HEADS_PER_GROUP * TQ
SLC_TQ = 256


def _load_q(q_ref):
    return jnp.concatenate(
        [q_ref[0, :, h * HEAD_PAD:(h + 1) * HEAD_PAD] for h in range(HEADS_PER_GROUP)], axis=0)


def _per_head(row):
    return jnp.concatenate([row] * HEADS_PER_GROUP, axis=1)


def _store_gated(o_ref, g_ref, o_t, branch):
    tq = o_ref.shape[1]
    for h in range(HEADS_PER_GROUP):
        c = h * N_BRANCHES + branch
        gate = g_ref[0, :, c:c + 1]
        o_ref[0, :, h * HEAD_PAD:(h + 1) * HEAD_PAD] = o_t[:, h * tq:(h + 1) * tq].T * gate


def _attn_specs(tq=TQ):
    q_spec = pl.BlockSpec((1, tq, GROUP_Q), lambda b, g, i: (b, i, g))
    g_spec = pl.BlockSpec((1, tq, HEAD_PAD), lambda b, g, i: (b, i, g))
    o_spec = pl.BlockSpec((1, tq, GROUP_Q), lambda b, g, i: (b, i, g))
    return q_spec, g_spec, o_spec


def _cmp_kernel(q_ref, kc_ref, vct_ref, g_ref, cend_ref, place_ref, o_ref, selq_ref, seld_ref):
    nsb = kc_ref.shape[1] // CMP_PER_SLC
    t0 = pl.program_id(2) * TQ
    half = nsb // 2
    if half % LANES == 0:
        @pl.when(t0 + TQ <= half * SLC_BLOCK)
        def _():
            _cmp_body(q_ref, kc_ref, vct_ref, g_ref, cend_ref, place_ref, o_ref, selq_ref, seld_ref, half)

        @pl.when(t0 + TQ > half * SLC_BLOCK)
        def _():
            _cmp_body(q_ref, kc_ref, vct_ref, g_ref, cend_ref, place_ref, o_ref, selq_ref, seld_ref, nsb)
    else:
        _cmp_body(q_ref, kc_ref, vct_ref, g_ref, cend_ref, place_ref, o_ref, selq_ref, seld_ref, nsb)


def _cmp_body(q_ref, kc_ref, vct_ref, g_ref, cend_ref, place_ref, o_ref, selq_ref, seld_ref, nvis):
    nsb_all = kc_ref.shape[1] // CMP_PER_SLC
    nsb = nvis
    t0 = pl.program_id(2) * TQ

    def slabs(ref_rows):
        if nvis == nsb_all:
            return ref_rows(0, CMP_PER_SLC * nsb_all)
        return jnp.concatenate([ref_rows(r * nsb_all, nvis) for r in range(CMP_PER_SLC)], axis=0)

    kc = slabs(lambda lo, n: kc_ref[0, lo:lo + n, :])
    cend = slabs(lambda lo, n: cend_ref[lo:lo + n, :])
    if nvis == nsb_all:
        vct = vct_ref[0]
    else:
        vct = jnp.concatenate([vct_ref[0, :, r * nsb_all:r * nsb_all + nvis]
                               for r in range(CMP_PER_SLC)], axis=1)

    s = _dot_nt(kc, _load_q(q_ref))
    s = s + _per_head(jnp.where(cend <= t0, 0.0, NEG_BIG))
    m = jnp.max(s, axis=0, keepdims=True)
    e = jnp.exp2(s - m)
    l = jnp.sum(e, axis=0, keepdims=True)
    tcol = t0 + lax.broadcasted_iota(jnp.int32, (1, TQ), 1)
    any_visible = _per_head(jnp.where(tcol >= CMP_BLOCK - 1, 1.0, 0.0))
    p = e * (any_visible / jnp.maximum(l, 1e-30))

    o_t = _dot(vct, p.astype(BF16))
    _store_gated(o_ref, g_ref, o_t, 0)

    imp = p[:, 0:TQ]
    for h in range(1, HEADS_PER_GROUP):
        imp = imp + p[:, h * TQ:(h + 1) * TQ]
    p0, p1, p2, p3 = (imp[r * nsb:(r + 1) * nsb, :] for r in range(CMP_PER_SLC))
    blk = lax.broadcasted_iota(jnp.int32, (nsb, TQ), 0)
    p3_prev = jnp.where(blk == 0, 0.0, pltpu.roll(p3, 1, axis=0))
    imp_slc = 0.5 * p3_prev + p0 + p1 + p2 + 0.5 * p3

    jt = (t0 + lax.broadcasted_iota(jnp.int32, (nsb, TQ), 1)) >> SLC_BLOCK_LOG2
    forced = (blk == 0) | (blk == jt) | (blk == jt - 1)
    free = (blk <= jt) & jnp.logical_not(forced)
    score = jnp.where(free, imp_slc, NEG_BIG)

    def pick(_, sc):
        mx = jnp.max(sc, axis=0, keepdims=True)
        first = jnp.min(jnp.where(sc == mx, blk, nsb), axis=0, keepdims=True)
        return jnp.where(blk == first, -jnp.inf, sc)

    picked = lax.fori_loop(0, min(N_SELECT - N_FORCED, nsb), pick, score) == -jnp.inf
    selected = (picked & free) | forced
    own0 = (t0 // SLC_TQ) * (SLC_TQ // SLC_BLOCK)

    def mask_lanes(keep, place):
        bias = jnp.where(keep, 0.0, NEG_BIG)
        if nvis < nsb_all:
            bias = jnp.concatenate([bias, jnp.full((nsb_all - nvis, TQ), NEG_BIG, F32)], axis=0)
        return _dot(bias.T.astype(BF16), place).astype(BF16)

    selq_ref[0, 0] = mask_lanes(selected & (blk < own0), place_ref[...])
    jrow = lax.broadcasted_iota(jnp.int32, (nsb_all, HEAD_PAD), 0)
    lane = lax.broadcasted_iota(jnp.int32, (nsb_all, HEAD_PAD), 1)
    in_super = (jrow >> (MASK_LANES.bit_length() - 1)) == (own0 // MASK_LANES)
    place_own = jnp.where(in_super & (lane == HEAD_DIM + (jrow & (MASK_LANES - 1))), 1.0, 0.0)
    seld_ref[0, 0] = mask_lanes(selected, place_own.astype(BF16))


def _mask_lane_placement(nsb):
    j = np.arange(nsb)
    place = np.zeros((nsb, (nsb // MASK_LANES) * HEAD_PAD), np.float32)
    place[j, (j // MASK_LANES) * HEAD_PAD + HEAD_DIM + j % MASK_LANES] = 1.0
    return jnp.asarray(place, BF16)


def _cmp_end_minus_token(nsb):
    row = np.arange(CMP_PER_SLC * nsb)
    end = (row % nsb) * SLC_BLOCK + (row // nsb) * CMP_STRIDE + CMP_BLOCK - 1
    return jnp.asarray(end[:, None] - np.arange(TQ)[None, :], jnp.int32)


def _cmp_attention(q, kcmp, vcmp_t, gates):
    bsz, s, _ = q.shape
    nsb = s // SLC_BLOCK
    ncp = CMP_PER_SLC * nsb
    selq_w = (nsb // MASK_LANES) * HEAD_PAD
    q_spec, g_spec, o_spec = _attn_specs()
    const = lambda b, g, i: (0, 0)
    return pl.pallas_call(
        _cmp_kernel,
        grid=(bsz, N_KV_GROUPS, s // TQ),
        in_specs=[q_spec,
                  pl.BlockSpec((1, ncp, HEAD_PAD), lambda b, g, i: (b, 0, g)),
                  pl.BlockSpec((1, HEAD_PAD, ncp), lambda b, g, i: (b, g, 0)),
                  g_spec,
                  pl.BlockSpec((ncp, TQ), const),
                  pl.BlockSpec((nsb, selq_w), const)],
        out_specs=[o_spec, pl.BlockSpec((1, 1, TQ, selq_w), lambda b, g, i: (b, g, i, 0)),
                   pl.BlockSpec((1, 1, TQ, HEAD_PAD), lambda b, g, i: (b, g, i, 0))],
        out_shape=[jax.ShapeDtypeStruct((bsz, s, Q_PAD), F32),
                   jax.ShapeDtypeStruct((bsz, N_KV_GROUPS, s, selq_w), BF16),
                   jax.ShapeDtypeStruct((bsz, N_KV_GROUPS, s, HEAD_PAD), BF16)],
        compiler_params=_params("parallel", "parallel", "arbitrary"),
        name="cmp_attn_topk",
    )(q, kcmp, vcmp_t, gates, _cmp_end_minus_token(nsb), _mask_lane_placement(nsb))


SLC_TK = 512
SUPER_KEYS = MASK_LANES * SLC_BLOCK
TILES_PER_SUPER = SUPER_KEYS // SLC_TK


def _slc_kernel(q_ref, k_ref, vt_ref, selq_ref, seld_ref, g_ref, o_ref, qm_scr, s_scr):
    n_super = selq_ref.shape[3] // HEAD_PAD
    t0 = pl.multiple_of(pl.program_id(2) * SLC_TQ, SLC_TQ)
    n_full = t0 // SUPER_KEYS
    tail_tiles = (t0 - n_full * SUPER_KEYS + SLC_TK - 1) // SLC_TK

    def masked_q(slab):
        return jnp.concatenate([q_ref[0, :, h * HEAD_PAD:(h + 1) * HEAD_PAD] + slab
                                for h in range(HEADS_PER_GROUP)], axis=0)

    for st in range(n_super):
        @pl.when(st * SUPER_KEYS < t0)
        def _():
            qm_scr[st] = masked_q(selq_ref[0, 0, :, st * HEAD_PAD:(st + 1) * HEAD_PAD])

    s = _dot_nt(k_ref[0, pl.ds(t0, SLC_TQ), :], masked_q(seld_ref[0, 0]))
    kk = lax.broadcasted_iota(jnp.int32, (SLC_TQ, SLC_TQ), 0)
    tt = lax.broadcasted_iota(jnp.int32, (SLC_TQ, SLC_TQ), 1)
    s = s + _per_head(jnp.where(kk <= tt, 0.0, NEG_BIG))
    m = jnp.max(s, axis=0, keepdims=True)
    acc = _dot(vt_ref[0, :, pl.ds(t0, SLC_TQ)], jnp.exp2(s - m).astype(BF16))

    def sweep(st, carry, n_keys):
        m, acc = carry
        k0 = pl.multiple_of(st * SUPER_KEYS, SUPER_KEYS)
        sc = _dot_nt(k_ref[0, pl.ds(k0, n_keys), :], qm_scr[st])
        s_scr[0:n_keys, :] = sc
        m_new = jnp.maximum(m, jnp.max(sc, axis=0, keepdims=True))
        acc = jnp.exp2(m - m_new) * acc
        for c in range(0, n_keys, SLC_TK):
            p = jnp.exp2(s_scr[c:c + SLC_TK, :] - m_new).astype(BF16)
            acc = acc + _dot(vt_ref[0, :, pl.ds(k0 + c, SLC_TK)], p)
        return m_new, acc

    carry = lax.fori_loop(0, n_full, functools.partial(sweep, n_keys=SUPER_KEYS), (m, acc))
    branches = [lambda c: c] + [functools.partial(sweep, n_full, n_keys=n * SLC_TK)
                                for n in range(1, TILES_PER_SUPER + 1)]
    _, acc = lax.switch(tail_tiles, branches, carry)
    _store_gated(o_ref, g_ref, acc / jnp.maximum(acc[ONES_ROW:ONES_ROW + 1, :], 1e-30), 1)


def _slc_attention(q, ks, vs_t, selq, seld, gates):
    bsz, s, _ = q.shape
    n_super = s // SUPER_KEYS
    q_spec, g_spec, o_spec = _attn_specs(SLC_TQ)
    return pl.pallas_call(
        _slc_kernel,
        grid=(bsz, N_KV_GROUPS, s // SLC_TQ),
        in_specs=[q_spec,
                  pl.BlockSpec((1, s, HEAD_PAD), lambda b, g, i: (b, 0, g)),
                  pl.BlockSpec((1, HEAD_PAD, s), lambda b, g, i: (b, g, 0)),
                  pl.BlockSpec((1, 1, SLC_TQ, n_super * HEAD_PAD), lambda b, g, i: (b, g, i, 0)),
                  pl.BlockSpec((1, 1, SLC_TQ, HEAD_PAD), lambda b, g, i: (b, g, i, 0)),
                  g_spec],
        out_specs=o_spec,
        out_shape=jax.ShapeDtypeStruct((bsz, s, Q_PAD), F32),
        scratch_shapes=[pltpu.VMEM((n_super, HEADS_PER_GROUP * SLC_TQ, HEAD_PAD), BF16),
                        pltpu.VMEM((SUPER_KEYS, HEADS_PER_GROUP * SLC_TQ), F32)],
        compiler_params=_params("parallel", "parallel", "arbitrary"),
        name="slc_attn",
    )(q, ks, vs_t, selq, seld, gates)


WIN_TQ = 256
WIN_KEYS = WINDOW + WIN_TQ


def _win_kernel(q_ref, k_ref, vt_ref, g_ref, o_ref):
    t0 = pl.program_id(2) * WIN_TQ
    k0 = pl.multiple_of(jnp.maximum(t0 - WINDOW, 0), WIN_TQ)
    s = _dot_nt(k_ref[0, pl.ds(k0, WIN_KEYS), :], _load_q(q_ref))
    diff = ((t0 + lax.broadcasted_iota(jnp.int32, (WIN_KEYS, WIN_TQ), 1))
            - (k0 + lax.broadcasted_iota(jnp.int32, (WIN_KEYS, WIN_TQ), 0)))
    s = s + _per_head(jnp.where((diff >= 0) & (diff < WINDOW), 0.0, NEG_BIG))
    m = jnp.max(s, axis=0, keepdims=True)
    e = jnp.exp2(s - m).astype(BF16)
    o_t = _dot(vt_ref[0, :, pl.ds(k0, WIN_KEYS)], e)
    _store_gated(o_ref, g_ref, o_t / jnp.maximum(o_t[ONES_ROW:ONES_ROW + 1, :], 1e-30), 2)


def _win_attention(q, kw, vw_t, gates):
    bsz, s, _ = q.shape
    q_spec, g_spec, o_spec = _attn_specs(WIN_TQ)
    return pl.pallas_call(
        _win_kernel,
        grid=(bsz, N_KV_GROUPS, s // WIN_TQ),
        in_specs=[q_spec,
                  pl.BlockSpec((1, s, HEAD_PAD), lambda b, g, i: (b, 0, g)),
                  pl.BlockSpec((1, HEAD_PAD, s), lambda b, g, i: (b, g, 0)),
                  g_spec],
        out_specs=o_spec,
        out_shape=jax.ShapeDtypeStruct((bsz, s, Q_PAD), F32),
        compiler_params=_params("parallel", "parallel", "arbitrary"),
        name="win_attn",
    )(q, kw, vw_t, gates)


MIX_TM = 512
POOL_HALO = 16


def _mix_out_kernel(x_ref, u_ref, halo_ref, oc_ref, os_ref, ow_ref, pw_ref, ps_ref,
                    wo_ref, g_ref, b_ref, o_ref, ext_scr):
    i = pl.program_id(1)
    u = u_ref[0]
    halo = jnp.where(i == 0, 0.0, halo_ref[0])
    ext_scr[0:POOL_HALO, :] = halo
    ext_scr[POOL_HALO:POOL_HALO + MIX_TM, :] = u

    lane = lax.broadcasted_iota(jnp.int32, (MIX_TM, POOL_WIDTH), 1)
    tpos = i * MIX_TM + lax.broadcasted_iota(jnp.int32, (MIX_TM, POOL_WIDTH), 0)
    grp = lane >> (POOL_GROUP_DIM.bit_length() - 1)
    run = u
    win_sum = jnp.zeros_like(u)
    cnt = jnp.zeros_like(u)
    done = 1
    for gidx, w in enumerate(POOL_WINDOWS):
        for kback in range(done, w):
            run = run + ext_scr[POOL_HALO - kback:POOL_HALO - kback + MIX_TM, :]
        done = w
        win_sum = jnp.where(grp == gidx, run, win_sum)
        cnt = jnp.where(grp == gidx, jnp.minimum(tpos + 1, w).astype(F32), cnt)
    pooled = win_sum / cnt - u
    mixed = _dot(pooled.astype(BF16), pw_ref[...]) * ps_ref[...]

    y_nsa = oc_ref[0] + os_ref[0] + ow_ref[0]
    y = (_dot(mixed.astype(BF16), wo_ref[0:POOL_WIDTH, :])
         + _dot(y_nsa.astype(BF16), wo_ref[POOL_WIDTH:, :]))
    z = ALPHA * x_ref[0] + y
    o_ref[0] = _layer_norm(z, g_ref[...], b_ref[...])


def _mix_out(x, u, o_cmp, o_slc, o_win, pool_w, pool_scale, w_out, g, b):
    bsz, s, _ = x.shape
    n_grp = len(POOL_WINDOWS)
    eye = jnp.eye(n_grp, dtype=F32)
    pw = (pool_w[:, :, None, :] * eye[:, None, :, None]).reshape(POOL_WIDTH, POOL_WIDTH).astype(BF16)
    wo_nsa = w_out[POOL_WIDTH:].reshape(N_Q_HEADS, HEAD_DIM, D_MODEL)
    wo_nsa = jnp.pad(wo_nsa, ((0, 0), (0, HEAD_PAD - HEAD_DIM), (0, 0))).reshape(Q_PAD, D_MODEL)
    wo = jnp.concatenate([w_out[:POOL_WIDTH], wo_nsa], axis=0).astype(BF16)

    def tile(width):
        return pl.BlockSpec((1, MIX_TM, width), lambda bb, i: (bb, i, 0))

    halo_blocks = MIX_TM // POOL_HALO
    const = lambda bb, i: (0, 0)
    return pl.pallas_call(
        _mix_out_kernel,
        grid=(bsz, s // MIX_TM),
        in_specs=[
            tile(D_MODEL), tile(POOL_WIDTH),
            pl.BlockSpec((1, POOL_HALO, POOL_WIDTH),
                         lambda bb, i: (bb, jnp.maximum(i * halo_blocks - 1, 0), 0)),
            tile(Q_PAD), tile(Q_PAD), tile(Q_PAD),
            pl.BlockSpec((POOL_WIDTH, POOL_WIDTH), const),
            pl.BlockSpec((1, POOL_WIDTH), const),
            pl.BlockSpec((POOL_WIDTH + Q_PAD, D_MODEL), const),
            pl.BlockSpec((1, D_MODEL), const),
            pl.BlockSpec((1, D_MODEL), const),
        ],
        out_specs=tile(D_MODEL),
        out_shape=jax.ShapeDtypeStruct((bsz, s, D_MODEL), F32),
        scratch_shapes=[pltpu.VMEM((POOL_HALO + MIX_TM, POOL_WIDTH), F32)],
        compiler_params=_params("parallel", "arbitrary"),
        name="mix_out_ln",
    )(x, u, u, o_cmp, o_slc, o_win, pw, pool_scale.reshape(1, -1), wo,
      g.reshape(1, -1), b.reshape(1, -1))


def kernel(x, ln1_g, ln1_b, ffn1_w_gate, ffn1_w_up, ffn1_w_down, w_in, b_gate, pool_w, pool_scale, cmp_pos_k, cmp_k_w1, cmp_k_w2, cmp_pos_v, cmp_v_w1, cmp_v_w2, w_out, ln2_g, ln2_b, ffn2_w_gate, ffn2_w_up, ffn2_w_down, ln3_g, ln3_b):
    bsz, s, d = x.shape
    assert d == D_MODEL and s % max(SUPER_KEYS, FFN_TM, MIX_TM, PROJ_TM) == 0
    for l in range(DEPTH):
        x = _ffn_ln(x.reshape(bsz * s, d), ffn1_w_gate[l], ffn1_w_up[l], ffn1_w_down[l],
                    ln1_g[l], ln1_b[l]).reshape(bsz, s, d)
        u, q, kc, vc, ks, kw, gates, vs_t, vw_t = _proj(x, w_in[l], b_gate[l])
        kcmp = _compress(kc, cmp_pos_k[l], cmp_k_w1[l], cmp_k_w2[l], channel_major=False)
        vcmp_t = _compress(vc, cmp_pos_v[l], cmp_v_w1[l], cmp_v_w2[l], channel_major=True)
        o_cmp, selq, seld = _cmp_attention(q, kcmp, vcmp_t, gates)
        o_slc = _slc_attention(q, ks, vs_t, selq, seld, gates)
        o_win = _win_attention(q, kw, vw_t, gates)
        x = _mix_out(x, u, o_cmp, o_slc, o_win, pool_w[l], pool_scale[l], w_out[l],
                     ln2_g[l], ln2_b[l])
        x = _ffn_ln(x.reshape(bsz * s, d), ffn2_w_gate[l], ffn2_w_up[l], ffn2_w_down[l],
                    ln3_g[l], ln3_b[l]).reshape(bsz, s, d)
    return x
```

```python
import functools

import numpy as np
import jax
import jax.numpy as jnp
from jax import lax
from jax.experimental import pallas as pl
from jax.experimental.pallas import tpu as pltpu

D_MODEL = 1024
DEPTH = 2
POOL_WIDTH = 256
POOL_WINDOWS = (2, 4, 8, 16)
POOL_GROUP_DIM = 64
N_Q_HEADS = 8
HEAD_DIM = 96
N_KV_GROUPS = 2
HEADS_PER_GROUP = 4
N_BRANCHES = 3
CMP_STRIDE = 16
CMP_BLOCK = 32
SLC_BLOCK = 64
SLC_BLOCK_LOG2 = 6
N_SELECT = 16
N_FORCED = 3
WINDOW = 512
D_FF = 2816
ALPHA = (2.0 * DEPTH) ** 0.25
LN_EPS = 1e-5
NEG_BIG = -1e30
SEL_BIG = 1e30
QK_SCALE = HEAD_DIM ** -0.5
LOG2_E = 1.4426950408889634
ONES_ROW = HEAD_DIM

LANES = 128
HEAD_PAD = LANES
Q_PAD = N_Q_HEADS * HEAD_PAD
KV_PAD = N_KV_GROUPS * HEAD_PAD
GROUP_Q = HEADS_PER_GROUP * HEAD_PAD
CMP_PER_SLC = SLC_BLOCK // CMP_STRIDE
CHUNK_FLAT = CMP_STRIDE * KV_PAD

VMEM_LIMIT = 56 * 1024 * 1024

F32 = jnp.float32
BF16 = jnp.bfloat16

_C_U = 0
_C_Q = _C_U + POOL_WIDTH
_C_KC = _C_Q + Q_PAD
_C_VC = _C_KC + KV_PAD
_C_KS = _C_VC + KV_PAD
_C_KW = _C_KS + KV_PAD
_C_G = _C_KW + KV_PAD
_C_END = _C_G + KV_PAD


def _params(*sem):
    return pltpu.CompilerParams(dimension_semantics=sem, vmem_limit_bytes=VMEM_LIMIT)


def _layer_norm(z, g, b):
    mu = jnp.mean(z, axis=-1, keepdims=True)
    zc = z - mu
    var = jnp.mean(zc * zc, axis=-1, keepdims=True)
    return zc * lax.rsqrt(var + LN_EPS) * g + b


def _dot(a, b):
    return jnp.dot(a, b, preferred_element_type=F32)


def _dot_nt(a, b):
    return lax.dot_general(a, b, (((1,), (1,)), ((), ())), preferred_element_type=F32)


FFN_TM = 1024
MXU_TILE = 256
FFN_CHUNKS = (6 * MXU_TILE, 5 * MXU_TILE)
assert sum(FFN_CHUNKS) == D_FF


def _ffn_ln_kernel(x_ref, wg_ref, wu_ref, wd_ref, g_ref, b_ref, o_ref):
    x = x_ref[...]
    xb = x.astype(BF16)
    acc = None
    lo = 0
    for width in FFN_CHUNKS:
        hg = _dot(xb, wg_ref[:, lo:lo + width])
        hu = _dot(xb, wu_ref[:, lo:lo + width])
        h = (hg * jax.nn.sigmoid(hg)) * hu
        part = _dot(h.astype(BF16), wd_ref[lo:lo + width, :])
        acc = part if acc is None else acc + part
        lo += width
    z = ALPHA * x + 0.5 * acc
    o_ref[...] = _layer_norm(z, g_ref[...], b_ref[...])


def _ffn_ln(x2d, wg, wu, wd, g, b):
    t = x2d.shape[0]
    const = lambda i: (0, 0)
    return pl.pallas_call(
        _ffn_ln_kernel,
        grid=(t // FFN_TM,),
        in_specs=[
            pl.BlockSpec((FFN_TM, D_MODEL), lambda i: (i, 0)),
            pl.BlockSpec((D_MODEL, D_FF), const, pipeline_mode=pl.Buffered(1)),
            pl.BlockSpec((D_MODEL, D_FF), const, pipeline_mode=pl.Buffered(1)),
            pl.BlockSpec((D_FF, D_MODEL), const, pipeline_mode=pl.Buffered(1)),
            pl.BlockSpec((1, D_MODEL), const),
            pl.BlockSpec((1, D_MODEL), const),
        ],
        out_specs=pl.BlockSpec((FFN_TM, D_MODEL), lambda i: (i, 0)),
        out_shape=jax.ShapeDtypeStruct((t, D_MODEL), F32),
        compiler_params=_params("parallel"),
        name="ffn_ln",
    )(x2d, wg.astype(BF16), wu.astype(BF16), wd.astype(BF16), g.reshape(1, -1), b.reshape(1, -1))


PROJ_TM = 512
MASK_LANES = HEAD_PAD - HEAD_DIM


def _proj_kernel(x_ref, w_ref, wvt_ref, bg_ref, u_ref, q_ref, kc_ref, vc_ref, ks_ref, kw_ref,
                 gt_ref, vst_ref, vwt_ref):
    xb = x_ref[0].astype(BF16)

    def mm(lo, n):
        return _dot(xb, w_ref[:, lo:lo + n])

    u_ref[0] = mm(_C_U, POOL_WIDTH)
    q_ref[0] = (mm(_C_Q, Q_PAD) * (QK_SCALE * LOG2_E)).astype(BF16)
    kc_ref[0] = mm(_C_KC, KV_PAD)
    vc_ref[0] = mm(_C_VC, KV_PAD)
    tpos = pl.program_id(1) * PROJ_TM + lax.broadcasted_iota(jnp.int32, (PROJ_TM, KV_PAD), 0)
    lane = lax.broadcasted_iota(jnp.int32, (PROJ_TM, KV_PAD), 1) & (HEAD_PAD - 1)
    hot = lane == HEAD_DIM + ((tpos >> SLC_BLOCK_LOG2) & (MASK_LANES - 1))
    ks_ref[0] = jnp.where(hot, 1.0, mm(_C_KS, KV_PAD)).astype(BF16)
    kw_ref[0] = mm(_C_KW, KV_PAD).astype(BF16)
    gt_ref[0] = jax.nn.sigmoid(mm(_C_G, KV_PAD) + bg_ref[...])
    chan = lax.broadcasted_iota(jnp.int32, (KV_PAD, PROJ_TM), 0) & (HEAD_PAD - 1)
    vst_ref[0] = jnp.where(chan == ONES_ROW, 1.0, _dot_nt(wvt_ref[0:KV_PAD, :], xb)).astype(BF16)
    vwt_ref[0] = jnp.where(chan == ONES_ROW, 1.0,
                           _dot_nt(wvt_ref[KV_PAD:2 * KV_PAD, :], xb)).astype(BF16)


def _pad_heads(w, n_heads):
    lead = w.shape[:-1]
    w = w.reshape(lead + (n_heads, HEAD_DIM))
    w = jnp.pad(w, [(0, 0)] * len(lead) + [(0, 0), (0, HEAD_PAD - HEAD_DIM)])
    return w.reshape(lead + (n_heads * HEAD_PAD,))


def _pad_gate_cols(w):
    lead = w.shape[:-1]
    per_group = HEADS_PER_GROUP * N_BRANCHES
    w = w.reshape(lead + (N_KV_GROUPS, per_group))
    w = jnp.pad(w, [(0, 0)] * len(lead) + [(0, 0), (0, HEAD_PAD - per_group)])
    return w.reshape(lead + (KV_PAD,))


def _proj(x, w_in, b_gate):
    bsz, s, _ = x.shape
    cuts = np.cumsum([POOL_WIDTH, N_Q_HEADS * HEAD_DIM] + [N_KV_GROUPS * HEAD_DIM] * 6)
    parts = jnp.split(w_in, [int(c) for c in cuts], axis=-1)
    cols = [parts[0], _pad_heads(parts[1], N_Q_HEADS)]
    cols += [_pad_heads(parts[i], N_KV_GROUPS) for i in (2, 3, 4, 6)]
    cols.append(_pad_gate_cols(parts[8]))
    w = jnp.concatenate(cols, axis=-1).astype(BF16)
    wvt = jnp.concatenate([_pad_heads(parts[5], N_KV_GROUPS),
                           _pad_heads(parts[7], N_KV_GROUPS)], axis=-1).T.astype(BF16)
    bg = _pad_gate_cols(b_gate).reshape(1, KV_PAD)

    def tile(width):
        return pl.BlockSpec((1, PROJ_TM, width), lambda b, i: (b, i, 0))

    def out(width, dtype):
        return jax.ShapeDtypeStruct((bsz, s, width), dtype)

    vt_spec = pl.BlockSpec((1, KV_PAD, PROJ_TM), lambda b, i: (b, 0, i))
    vt_out = jax.ShapeDtypeStruct((bsz, KV_PAD, s), BF16)
    return pl.pallas_call(
        _proj_kernel,
        grid=(bsz, s // PROJ_TM),
        in_specs=[
            tile(D_MODEL),
            pl.BlockSpec((D_MODEL, _C_END), lambda b, i: (0, 0), pipeline_mode=pl.Buffered(1)),
            pl.BlockSpec((2 * KV_PAD, D_MODEL), lambda b, i: (0, 0), pipeline_mode=pl.Buffered(1)),
            pl.BlockSpec((1, KV_PAD), lambda b, i: (0, 0)),
        ],
        out_specs=[tile(POOL_WIDTH), tile(Q_PAD), tile(KV_PAD), tile(KV_PAD), tile(KV_PAD),
                   tile(KV_PAD), tile(KV_PAD), vt_spec, vt_spec],
        out_shape=[out(POOL_WIDTH, F32), out(Q_PAD, BF16), out(KV_PAD, F32), out(KV_PAD, F32),
                   out(KV_PAD, BF16), out(KV_PAD, BF16), out(KV_PAD, F32), vt_out, vt_out],
        compiler_params=_params("parallel", "parallel"),
        name="in_proj",
    )(x, w, wvt, bg)


def _gelu_tanh(x):
    c = np.float32(np.sqrt(2.0 / np.pi))
    return x * (0.5 * (1.0 + jnp.tanh(c * (x + 0.044715 * (x * x * x)))))


def _compress_kernel(x_ref, plo_ref, phi_ref, wlo_ref, whi_ref, w2_ref, o_ref, a_scr, b_scr,
                     *, channel_major):
    r = pl.program_id(1)
    nsb = x_ref.shape[1]
    xr = x_ref[0]
    a_scr[r] = _dot((xr + plo_ref[...]).astype(BF16), wlo_ref[...])
    b_scr[r, 0:nsb, :] = _dot((xr + phi_ref[...]).astype(BF16), whi_ref[...])

    @pl.when(r == 0)
    def _():
        b_scr[0, nsb:nsb + 8, :] = jnp.zeros((8, KV_PAD), F32)

    @pl.when(r == CMP_PER_SLC - 1)
    def _():
        for rr in range(CMP_PER_SLC):
            if rr < CMP_PER_SLC - 1:
                h = a_scr[rr] + b_scr[rr + 1, 0:nsb, :]
            else:
                h = a_scr[rr] + b_scr[0, 1:nsb + 1, :]
            act = _gelu_tanh(h).astype(BF16)
            if channel_major:
                o_ref[0, :, rr * nsb:(rr + 1) * nsb] = _dot_nt(w2_ref[...], act).astype(BF16)
            else:
                o_ref[0, rr * nsb:(rr + 1) * nsb, :] = _dot(act, w2_ref[...]).astype(BF16)


def _compress_weights(pos, w1, w2):
    eye_g = jnp.eye(N_KV_GROUPS, dtype=F32)
    w1r = w1.reshape(CMP_BLOCK, HEAD_DIM, HEAD_DIM)
    w1r = jnp.pad(w1r, ((0, 0), (0, HEAD_PAD - HEAD_DIM), (0, HEAD_PAD - HEAD_DIM)))
    w1c = w1r[:, None, :, None, :] * eye_g[None, :, None, :, None]
    w1c = w1c.reshape(CMP_BLOCK, KV_PAD, KV_PAD)
    wlo = w1c[:CMP_STRIDE].reshape(CHUNK_FLAT, KV_PAD).astype(BF16)
    whi = w1c[CMP_STRIDE:].reshape(CHUNK_FLAT, KV_PAD).astype(BF16)
    posp = jnp.pad(pos, ((0, 0), (0, HEAD_PAD - HEAD_DIM)))
    posp = jnp.tile(posp[:, None, :], (1, N_KV_GROUPS, 1))
    plo = posp[:CMP_STRIDE].reshape(1, CHUNK_FLAT)
    phi = posp[CMP_STRIDE:].reshape(1, CHUNK_FLAT)
    w2p = jnp.pad(w2, ((0, HEAD_PAD - HEAD_DIM), (0, HEAD_PAD - HEAD_DIM)))
    w2c = (w2p[None, :, None, :] * eye_g[:, None, :, None]).reshape(KV_PAD, KV_PAD).astype(BF16)
    return plo, phi, wlo, whi, w2c


def _compress(kv, pos, w1, w2, channel_major):
    bsz, s, _ = kv.shape
    nsb = s // SLC_BLOCK
    ncp = CMP_PER_SLC * nsb
    plo, phi, wlo, whi, w2c = _compress_weights(pos, w1, w2)
    if channel_major:
        w2c = w2c.T
    out_dims = (KV_PAD, ncp) if channel_major else (ncp, KV_PAD)
    x = kv.reshape(bsz, nsb, CMP_PER_SLC * CHUNK_FLAT)
    const = lambda b, r: (0, 0)
    return pl.pallas_call(
        functools.partial(_compress_kernel, channel_major=channel_major),
        grid=(bsz, CMP_PER_SLC),
        in_specs=[
            pl.BlockSpec((1, nsb, CHUNK_FLAT), lambda b, r: (b, 0, r)),
            pl.BlockSpec((1, CHUNK_FLAT), const),
            pl.BlockSpec((1, CHUNK_FLAT), const),
            pl.BlockSpec((CHUNK_FLAT, KV_PAD), const),
            pl.BlockSpec((CHUNK_FLAT, KV_PAD), const),
            pl.BlockSpec((KV_PAD, KV_PAD), const),
        ],
        out_specs=pl.BlockSpec((1,) + out_dims, lambda b, r: (b, 0, 0)),
        out_shape=jax.ShapeDtypeStruct((bsz,) + out_dims, BF16),
        scratch_shapes=[pltpu.VMEM((CMP_PER_SLC, nsb, KV_PAD), F32),
                        pltpu.VMEM((CMP_PER_SLC, nsb + 8, KV_PAD), F32)],
        compiler_params=_params("parallel", "arbitrary"),
        name="compress",
    )(x, plo, phi, wlo, whi, w2c)


TQ = 128
COLS = HEADS_PER_GROUP * TQ
SLC_TQ = 512


def _load_q(q_ref):
    return jnp.concatenate(
        [q_ref[0, :, h * HEAD_PAD:(h + 1) * HEAD_PAD] for h in range(HEADS_PER_GROUP)], axis=0)


def _per_head(row):
    return jnp.concatenate([row] * HEADS_PER_GROUP, axis=1)


def _store_gated(o_ref, g_ref, o_t, branch):
    tq = o_ref.shape[1]
    for h in range(HEADS_PER_GROUP):
        c = h * N_BRANCHES + branch
        gate = g_ref[0, :, c:c + 1]
        o_ref[0, :, h * HEAD_PAD:(h + 1) * HEAD_PAD] = o_t[:, h * tq:(h + 1) * tq].T * gate


def _attn_specs(tq=TQ):
    q_spec = pl.BlockSpec((1, tq, GROUP_Q), lambda b, g, i: (b, i, g))
    g_spec = pl.BlockSpec((1, tq, HEAD_PAD), lambda b, g, i: (b, i, g))
    o_spec = pl.BlockSpec((1, tq, GROUP_Q), lambda b, g, i: (b, i, g))
    return q_spec, g_spec, o_spec


def _cmp_kernel(q_ref, kc_ref, vct_ref, g_ref, cend_ref, place_ref, o_ref, selq_ref, seld_ref):
    nsb = kc_ref.shape[1] // CMP_PER_SLC
    t0 = pl.program_id(2) * TQ
    half = nsb // 2
    if half % LANES == 0:
        @pl.when(t0 + TQ <= half * SLC_BLOCK)
        def _():
            _cmp_body(q_ref, kc_ref, vct_ref, g_ref, cend_ref, place_ref, o_ref, selq_ref, seld_ref, half)

        @pl.when(t0 + TQ > half * SLC_BLOCK)
        def _():
            _cmp_body(q_ref, kc_ref, vct_ref, g_ref, cend_ref, place_ref, o_ref, selq_ref, seld_ref, nsb)
    else:
        _cmp_body(q_ref, kc_ref, vct_ref, g_ref, cend_ref, place_ref, o_ref, selq_ref, seld_ref, nsb)


def _cmp_body(q_ref, kc_ref, vct_ref, g_ref, cend_ref, place_ref, o_ref, selq_ref, seld_ref, nvis):
    nsb_all = kc_ref.shape[1] // CMP_PER_SLC
    nsb = nvis
    t0 = pl.program_id(2) * TQ

    def slabs(ref_rows):
        if nvis == nsb_all:
            return ref_rows(0, CMP_PER_SLC * nsb_all)
        return jnp.concatenate([ref_rows(r * nsb_all, nvis) for r in range(CMP_PER_SLC)], axis=0)

    kc = slabs(lambda lo, n: kc_ref[0, lo:lo + n, :])
    cend = slabs(lambda lo, n: cend_ref[lo:lo + n, :])
    if nvis == nsb_all:
        vct = vct_ref[0]
    else:
        vct = jnp.concatenate([vct_ref[0, :, r * nsb_all:r * nsb_all + nvis]
                               for r in range(CMP_PER_SLC)], axis=1)

    s = _dot_nt(kc, _load_q(q_ref))
    s = s + _per_head(jnp.where(cend <= t0, 0.0, NEG_BIG))
    m = jnp.max(s, axis=0, keepdims=True)
    e = jnp.exp2(s - m)
    l = jnp.sum(e, axis=0, keepdims=True)
    tcol = t0 + lax.broadcasted_iota(jnp.int32, (1, TQ), 1)
    any_visible = _per_head(jnp.where(tcol >= CMP_BLOCK - 1, 1.0, 0.0))
    p = e * (any_visible / jnp.maximum(l, 1e-30))

    o_t = _dot(vct, p.astype(BF16))
    _store_gated(o_ref, g_ref, o_t, 0)

    imp = p[:, 0:TQ]
    for h in range(1, HEADS_PER_GROUP):
        imp = imp + p[:, h * TQ:(h + 1) * TQ]
    p0, p1, p2, p3 = (imp[r * nsb:(r + 1) * nsb, :] for r in range(CMP_PER_SLC))
    blk = lax.broadcasted_iota(jnp.int32, (nsb, TQ), 0)
    p3_prev = jnp.where(blk == 0, 0.0, pltpu.roll(p3, 1, axis=0))
    imp_slc = 0.5 * p3_prev + p0 + p1 + p2 + 0.5 * p3

    jt = (t0 + lax.broadcasted_iota(jnp.int32, (nsb, TQ), 1)) >> SLC_BLOCK_LOG2
    forced = (blk == 0) | (blk == jt) | (blk == jt - 1)
    free = (blk <= jt) & jnp.logical_not(forced)
    score = jnp.where(free, imp_slc, NEG_BIG)

    def pick(_, sc):
        mx = jnp.max(sc, axis=0, keepdims=True)
        first = jnp.min(jnp.where(sc == mx, blk, nsb), axis=0, keepdims=True)
        return jnp.where(blk == first, -jnp.inf, sc)

    picked = lax.fori_loop(0, min(N_SELECT - N_FORCED, nsb), pick, score, unroll=True) == -jnp.inf
    selected = (picked & free) | forced
    own0 = (t0 // SLC_TQ) * (SLC_TQ // SLC_BLOCK)

    def mask_lanes(keep, place):
        bias = jnp.where(keep, 0.0, NEG_BIG)
        if nvis < nsb_all:
            bias = jnp.concatenate([bias, jnp.full((nsb_all - nvis, TQ), NEG_BIG, F32)], axis=0)
        return _dot(bias.T.astype(BF16), place).astype(BF16)

    selq_ref[0, 0] = mask_lanes(selected & (blk < own0), place_ref[...])
    jrow = lax.broadcasted_iota(jnp.int32, (nsb_all, HEAD_PAD), 0)
    lane = lax.broadcasted_iota(jnp.int32, (nsb_all, HEAD_PAD), 1)
    in_super = (jrow >> (MASK_LANES.bit_length() - 1)) == (own0 // MASK_LANES)
    place_own = jnp.where(in_super & (lane == HEAD_DIM + (jrow & (MASK_LANES - 1))), 1.0, 0.0)
    seld_ref[0, 0] = mask_lanes(selected, place_own.astype(BF16))


def _mask_lane_placement(nsb):
    j = np.arange(nsb)
    place = np.zeros((nsb, (nsb // MASK_LANES) * HEAD_PAD), np.float32)
    place[j, (j // MASK_LANES) * HEAD_PAD + HEAD_DIM + j % MASK_LANES] = 1.0
    return jnp.asarray(place, BF16)


def _cmp_end_minus_token(nsb):
    row = np.arange(CMP_PER_SLC * nsb)
    end = (row % nsb) * SLC_BLOCK + (row // nsb) * CMP_STRIDE + CMP_BLOCK - 1
    return jnp.asarray(end[:, None] - np.arange(TQ)[None, :], jnp.int32)


def _cmp_attention(q, kcmp, vcmp_t, gates):
    bsz, s, _ = q.shape
    nsb = s // SLC_BLOCK
    ncp = CMP_PER_SLC * nsb
    selq_w = (nsb // MASK_LANES) * HEAD_PAD
    q_spec, g_spec, o_spec = _attn_specs()
    const = lambda b, g, i: (0, 0)
    return pl.pallas_call(
        _cmp_kernel,
        grid=(bsz, N_KV_GROUPS, s // TQ),
        in_specs=[q_spec,
                  pl.BlockSpec((1, ncp, HEAD_PAD), lambda b, g, i: (b, 0, g)),
                  pl.BlockSpec((1, HEAD_PAD, ncp), lambda b, g, i: (b, g, 0)),
                  g_spec,
                  pl.BlockSpec((ncp, TQ), const),
                  pl.BlockSpec((nsb, selq_w), const)],
        out_specs=[o_spec, pl.BlockSpec((1, 1, TQ, selq_w), lambda b, g, i: (b, g, i, 0)),
                   pl.BlockSpec((1, 1, TQ, HEAD_PAD), lambda b, g, i: (b, g, i, 0))],
        out_shape=[jax.ShapeDtypeStruct((bsz, s, Q_PAD), F32),
                   jax.ShapeDtypeStruct((bsz, N_KV_GROUPS, s, selq_w), BF16),
                   jax.ShapeDtypeStruct((bsz, N_KV_GROUPS, s, HEAD_PAD), BF16)],
        compiler_params=_params("parallel", "parallel", "arbitrary"),
        name="cmp_attn_topk",
    )(q, kcmp, vcmp_t, gates, _cmp_end_minus_token(nsb), _mask_lane_placement(nsb))


SLC_TK = 512
SUPER_KEYS = MASK_LANES * SLC_BLOCK
TILES_PER_SUPER = SUPER_KEYS // SLC_TK


def _slc_kernel(q_ref, k_ref, vt_ref, selq_ref, seld_ref, g_ref, o_ref, qm_scr, s_scr):
    n_super = selq_ref.shape[3] // HEAD_PAD
    t0 = pl.multiple_of(pl.program_id(2) * SLC_TQ, SLC_TQ)
    n_full = t0 // SUPER_KEYS
    tail_tiles = (t0 - n_full * SUPER_KEYS + SLC_TK - 1) // SLC_TK

    def masked_q(slab):
        return jnp.concatenate([q_ref[0, :, h * HEAD_PAD:(h + 1) * HEAD_PAD] + slab
                                for h in range(HEADS_PER_GROUP)], axis=0)

    for st in range(n_super):
        @pl.when(st * SUPER_KEYS < t0)
        def _():
            qm_scr[st] = masked_q(selq_ref[0, 0, :, st * HEAD_PAD:(st + 1) * HEAD_PAD])

    s = _dot_nt(k_ref[0, pl.ds(t0, SLC_TQ), :], masked_q(seld_ref[0, 0]))
    kk = lax.broadcasted_iota(jnp.int32, (SLC_TQ, SLC_TQ), 0)
    tt = lax.broadcasted_iota(jnp.int32, (SLC_TQ, SLC_TQ), 1)
    s = s + _per_head(jnp.where(kk <= tt, 0.0, NEG_BIG))
    m = jnp.max(s, axis=0, keepdims=True)
    acc = _dot(vt_ref[0, :, pl.ds(t0, SLC_TQ)], jnp.exp2(s - m).astype(BF16))

    def sweep(st, carry, n_keys):
        m, acc = carry
        k0 = pl.multiple_of(st * SUPER_KEYS, SUPER_KEYS)
        sc = _dot_nt(k_ref[0, pl.ds(k0, n_keys), :], qm_scr[st])
        s_scr[0:n_keys, :] = sc
        m_new = jnp.maximum(m, jnp.max(sc, axis=0, keepdims=True))
        acc = jnp.exp2(m - m_new) * acc
        for c in range(0, n_keys, SLC_TK):
            p = jnp.exp2(s_scr[c:c + SLC_TK, :] - m_new).astype(BF16)
            acc = acc + _dot(vt_ref[0, :, pl.ds(k0 + c, SLC_TK)], p)
        return m_new, acc

    carry = lax.fori_loop(0, n_full, functools.partial(sweep, n_keys=SUPER_KEYS), (m, acc))
    branches = [lambda c: c] + [functools.partial(sweep, n_full, n_keys=n * SLC_TK)
                                for n in range(1, TILES_PER_SUPER + 1)]
    _, acc = lax.switch(tail_tiles, branches, carry)
    _store_gated(o_ref, g_ref, acc / jnp.maximum(acc[ONES_ROW:ONES_ROW + 1, :], 1e-30), 1)


def _slc_attention(q, ks, vs_t, selq, seld, gates):
    bsz, s, _ = q.shape
    n_super = s // SUPER_KEYS
    q_spec, g_spec, o_spec = _attn_specs(SLC_TQ)
    return pl.pallas_call(
        _slc_kernel,
        grid=(bsz, N_KV_GROUPS, s // SLC_TQ),
        in_specs=[q_spec,
                  pl.BlockSpec((1, s, HEAD_PAD), lambda b, g, i: (b, 0, g)),
                  pl.BlockSpec((1, HEAD_PAD, s), lambda b, g, i: (b, g, 0)),
                  pl.BlockSpec((1, 1, SLC_TQ, n_super * HEAD_PAD), lambda b, g, i: (b, g, i, 0)),
                  pl.BlockSpec((1, 1, SLC_TQ, HEAD_PAD), lambda b, g, i: (b, g, i, 0)),
                  g_spec],
        out_specs=o_spec,
        out_shape=jax.ShapeDtypeStruct((bsz, s, Q_PAD), F32),
        scratch_shapes=[pltpu.VMEM((n_super, HEADS_PER_GROUP * SLC_TQ, HEAD_PAD), BF16),
                        pltpu.VMEM((SUPER_KEYS, HEADS_PER_GROUP * SLC_TQ), F32)],
        compiler_params=_params("parallel", "parallel", "arbitrary"),
        name="slc_attn",
    )(q, ks, vs_t, selq, seld, gates)


WIN_TQ = 256
WIN_KEYS = WINDOW + WIN_TQ


def _win_kernel(q_ref, k_ref, vt_ref, g_ref, o_ref):
    t0 = pl.program_id(2) * WIN_TQ
    k0 = pl.multiple_of(jnp.maximum(t0 - WINDOW, 0), WIN_TQ)
    s = _dot_nt(k_ref[0, pl.ds(k0, WIN_KEYS), :], _load_q(q_ref))
    diff = ((t0 + lax.broadcasted_iota(jnp.int32, (WIN_KEYS, WIN_TQ), 1))
            - (k0 + lax.broadcasted_iota(jnp.int32, (WIN_KEYS, WIN_TQ), 0)))
    s = s + _per_head(jnp.where((diff >= 0) & (diff < WINDOW), 0.0, NEG_BIG))
    m = jnp.max(s, axis=0, keepdims=True)
    e = jnp.exp2(s - m).astype(BF16)
    o_t = _dot(vt_ref[0, :, pl.ds(k0, WIN_KEYS)], e)
    _store_gated(o_ref, g_ref, o_t / jnp.maximum(o_t[ONES_ROW:ONES_ROW + 1, :], 1e-30), 2)


def _win_attention(q, kw, vw_t, gates):
    bsz, s, _ = q.shape
    q_spec, g_spec, o_spec = _attn_specs(WIN_TQ)
    return pl.pallas_call(
        _win_kernel,
        grid=(bsz, N_KV_GROUPS, s // WIN_TQ),
        in_specs=[q_spec,
                  pl.BlockSpec((1, s, HEAD_PAD), lambda b, g, i: (b, 0, g)),
                  pl.BlockSpec((1, HEAD_PAD, s), lambda b, g, i: (b, g, 0)),
                  g_spec],
        out_specs=o_spec,
        out_shape=jax.ShapeDtypeStruct((bsz, s, Q_PAD), F32),
        compiler_params=_params("parallel", "parallel", "arbitrary"),
        name="win_attn",
    )(q, kw, vw_t, gates)


MIX_TM = 512
POOL_HALO = 16


def _mix_out_kernel(x_ref, u_ref, halo_ref, oc_ref, os_ref, ow_ref, pw_ref, ps_ref,
                    wo_ref, g_ref, b_ref, o_ref, ext_scr):
    i = pl.program_id(1)
    u = u_ref[0]
    halo = jnp.where(i == 0, 0.0, halo_ref[0])
    ext_scr[0:POOL_HALO, :] = halo
    ext_scr[POOL_HALO:POOL_HALO + MIX_TM, :] = u

    lane = lax.broadcasted_iota(jnp.int32, (MIX_TM, POOL_WIDTH), 1)
    tpos = i * MIX_TM + lax.broadcasted_iota(jnp.int32, (MIX_TM, POOL_WIDTH), 0)
    grp = lane >> (POOL_GROUP_DIM.bit_length() - 1)
    run = u
    win_sum = jnp.zeros_like(u)
    cnt = jnp.zeros_like(u)
    done = 1
    for gidx, w in enumerate(POOL_WINDOWS):
        for kback in range(done, w):
            run = run + ext_scr[POOL_HALO - kback:POOL_HALO - kback + MIX_TM, :]
        done = w
        win_sum = jnp.where(grp == gidx, run, win_sum)
        cnt = jnp.where(grp == gidx, jnp.minimum(tpos + 1, w).astype(F32), cnt)
    pooled = win_sum / cnt - u
    mixed = _dot(pooled.astype(BF16), pw_ref[...]) * ps_ref[...]

    y_nsa = oc_ref[0] + os_ref[0] + ow_ref[0]
    y = (_dot(mixed.astype(BF16), wo_ref[0:POOL_WIDTH, :])
         + _dot(y_nsa.astype(BF16), wo_ref[POOL_WIDTH:, :]))
    z = ALPHA * x_ref[0] + y
    o_ref[0] = _layer_norm(z, g_ref[...], b_ref[...])


def _mix_out(x, u, o_cmp, o_slc, o_win, pool_w, pool_scale, w_out, g, b):
    bsz, s, _ = x.shape
    n_grp = len(POOL_WINDOWS)
    eye = jnp.eye(n_grp, dtype=F32)
    pw = (pool_w[:, :, None, :] * eye[:, None, :, None]).reshape(POOL_WIDTH, POOL_WIDTH).astype(BF16)
    wo_nsa = w_out[POOL_WIDTH:].reshape(N_Q_HEADS, HEAD_DIM, D_MODEL)
    wo_nsa = jnp.pad(wo_nsa, ((0, 0), (0, HEAD_PAD - HEAD_DIM), (0, 0))).reshape(Q_PAD, D_MODEL)
    wo = jnp.concatenate([w_out[:POOL_WIDTH], wo_nsa], axis=0).astype(BF16)

    def tile(width):
        return pl.BlockSpec((1, MIX_TM, width), lambda bb, i: (bb, i, 0))

    halo_blocks = MIX_TM // POOL_HALO
    const = lambda bb, i: (0, 0)
    return pl.pallas_call(
        _mix_out_kernel,
        grid=(bsz, s // MIX_TM),
        in_specs=[
            tile(D_MODEL), tile(POOL_WIDTH),
            pl.BlockSpec((1, POOL_HALO, POOL_WIDTH),
                         lambda bb, i: (bb, jnp.maximum(i * halo_blocks - 1, 0), 0)),
            tile(Q_PAD), tile(Q_PAD), tile(Q_PAD),
            pl.BlockSpec((POOL_WIDTH, POOL_WIDTH), const),
            pl.BlockSpec((1, POOL_WIDTH), const),
            pl.BlockSpec((POOL_WIDTH + Q_PAD, D_MODEL), const),
            pl.BlockSpec((1, D_MODEL), const),
            pl.BlockSpec((1, D_MODEL), const),
        ],
        out_specs=tile(D_MODEL),
        out_shape=jax.ShapeDtypeStruct((bsz, s, D_MODEL), F32),
        scratch_shapes=[pltpu.VMEM((POOL_HALO + MIX_TM, POOL_WIDTH), F32)],
        compiler_params=_params("parallel", "arbitrary"),
        name="mix_out_ln",
    )(x, u, u, o_cmp, o_slc, o_win, pw, pool_scale.reshape(1, -1), wo,
      g.reshape(1, -1), b.reshape(1, -1))


def kernel(x, ln1_g, ln1_b, ffn1_w_gate, ffn1_w_up, ffn1_w_down, w_in, b_gate, pool_w, pool_scale, cmp_pos_k, cmp_k_w1, cmp_k_w2, cmp_pos_v, cmp_v_w1, cmp_v_w2, w_out, ln2_g, ln2_b, ffn2_w_gate, ffn2_w_up, ffn2_w_down, ln3_g, ln3_b):
    bsz, s, d = x.shape
    assert d == D_MODEL and s % max(SUPER_KEYS, FFN_TM, MIX_TM, PROJ_TM) == 0
    for l in range(DEPTH):
        x = _ffn_ln(x.reshape(bsz * s, d), ffn1_w_gate[l], ffn1_w_up[l], ffn1_w_down[l],
                    ln1_g[l], ln1_b[l]).reshape(bsz, s, d)
        u, q, kc, vc, ks, kw, gates, vs_t, vw_t = _proj(x, w_in[l], b_gate[l])
        kcmp = _compress(kc, cmp_pos_k[l], cmp_k_w1[l], cmp_k_w2[l], channel_major=False)
        vcmp_t = _compress(vc, cmp_pos_v[l], cmp_v_w1[l], cmp_v_w2[l], channel_major=True)
        o_cmp, selq, seld = _cmp_attention(q, kcmp, vcmp_t, gates)
        o_slc = _slc_attention(q, ks, vs_t, selq, seld, gates)
        o_win = _win_attention(q, kw, vw_t, gates)
        x = _mix_out(x, u, o_cmp, o_slc, o_win, pool_w[l], pool_scale[l], w_out[l],
                     ln2_g[l], ln2_b[l])
        x = _ffn_ln(x.reshape(bsz * s, d), ffn2_w_gate[l], ffn2_w_up[l], ffn2_w_down[l],
                    ln3_g[l], ln3_b[l]).reshape(bsz, s, d)
    return x
```

```python
import functools

import numpy as np
import jax
import jax.numpy as jnp
from jax import lax
from jax.experimental import pallas as pl
from jax.experimental.pallas import tpu as pltpu

D_MODEL = 1024
DEPTH = 2
POOL_WIDTH = 256
POOL_WINDOWS = (2, 4, 8, 16)
POOL_GROUP_DIM = 64
N_Q_HEADS = 8
HEAD_DIM = 96
N_KV_GROUPS = 2
HEADS_PER_GROUP = 4
N_BRANCHES = 3
CMP_STRIDE = 16
CMP_BLOCK = 32
SLC_BLOCK = 64
SLC_BLOCK_LOG2 = 6
N_SELECT = 16
N_FORCED = 3
WINDOW = 512
D_FF = 2816
ALPHA = (2.0 * DEPTH) ** 0.25
LN_EPS = 1e-5
NEG_BIG = -1e30
SEL_BIG = 1e30
QK_SCALE = HEAD_DIM ** -0.5
LOG2_E = 1.4426950408889634
ONES_ROW = HEAD_DIM

LANES = 128
HEAD_PAD = LANES
Q_PAD = N_Q_HEADS * HEAD_PAD
KV_PAD = N_KV_GROUPS * HEAD_PAD
GROUP_Q = HEADS_PER_GROUP * HEAD_PAD
CMP_PER_SLC = SLC_BLOCK // CMP_STRIDE
CHUNK_FLAT = CMP_STRIDE * KV_PAD

VMEM_LIMIT = 56 * 1024 * 1024

F32 = jnp.float32
BF16 = jnp.bfloat16

_C_U = 0
_C_Q = _C_U + POOL_WIDTH
_C_KC = _C_Q + Q_PAD
_C_VC = _C_KC + KV_PAD
_C_KS = _C_VC + KV_PAD
_C_KW = _C_KS + KV_PAD
_C_G = _C_KW + KV_PAD
_C_END = _C_G + KV_PAD


def _params(*sem):
    return pltpu.CompilerParams(dimension_semantics=sem, vmem_limit_bytes=VMEM_LIMIT)


def _layer_norm(z, g, b):
    mu = jnp.mean(z, axis=-1, keepdims=True)
    zc = z - mu
    var = jnp.mean(zc * zc, axis=-1, keepdims=True)
    return zc * lax.rsqrt(var + LN_EPS) * g + b


def _dot(a, b):
    return jnp.dot(a, b, preferred_element_type=F32)


def _dot_nt(a, b):
    return lax.dot_general(a, b, (((1,), (1,)), ((), ())), preferred_element_type=F32)


FFN_TM = 1024
MXU_TILE = 256
FFN_CHUNKS = (6 * MXU_TILE, 5 * MXU_TILE)
assert sum(FFN_CHUNKS) == D_FF


def _ffn_ln_kernel(x_ref, wg_ref, wu_ref, wd_ref, g_ref, b_ref, o_ref):
    x = x_ref[...]
    xb = x.astype(BF16)
    acc = None
    lo = 0
    for width in FFN_CHUNKS:
        hg = _dot(xb, wg_ref[:, lo:lo + width])
        hu = _dot(xb, wu_ref[:, lo:lo + width])
        h = (hg * jax.nn.sigmoid(hg)) * hu
        part = _dot(h.astype(BF16), wd_ref[lo:lo + width, :])
        acc = part if acc is None else acc + part
        lo += width
    z = ALPHA * x + 0.5 * acc
    o_ref[...] = _layer_norm(z, g_ref[...], b_ref[...])


def _ffn_ln(x2d, wg, wu, wd, g, b):
    t = x2d.shape[0]
    const = lambda i: (0, 0)
    return pl.pallas_call(
        _ffn_ln_kernel,
        grid=(t // FFN_TM,),
        in_specs=[
            pl.BlockSpec((FFN_TM, D_MODEL), lambda i: (i, 0)),
            pl.BlockSpec((D_MODEL, D_FF), const, pipeline_mode=pl.Buffered(1)),
            pl.BlockSpec((D_MODEL, D_FF), const, pipeline_mode=pl.Buffered(1)),
            pl.BlockSpec((D_FF, D_MODEL), const, pipeline_mode=pl.Buffered(1)),
            pl.BlockSpec((1, D_MODEL), const),
            pl.BlockSpec((1, D_MODEL), const),
        ],
        out_specs=pl.BlockSpec((FFN_TM, D_MODEL), lambda i: (i, 0)),
        out_shape=jax.ShapeDtypeStruct((t, D_MODEL), F32),
        compiler_params=_params("parallel"),
        name="ffn_ln",
    )(x2d, wg.astype(BF16), wu.astype(BF16), wd.astype(BF16), g.reshape(1, -1), b.reshape(1, -1))


PROJ_TM = 512
MASK_LANES = HEAD_PAD - HEAD_DIM


def _proj_kernel(x_ref, w_ref, wvt_ref, bg_ref, u_ref, q_ref, kc_ref, vc_ref, ks_ref, kw_ref,
                 gt_ref, vst_ref, vwt_ref):
    xb = x_ref[0].astype(BF16)

    def mm(lo, n):
        return _dot(xb, w_ref[:, lo:lo + n])

    u_ref[0] = mm(_C_U, POOL_WIDTH)
    q_ref[0] = (mm(_C_Q, Q_PAD) * (QK_SCALE * LOG2_E)).astype(BF16)
    kc_ref[0] = mm(_C_KC, KV_PAD)
    vc_ref[0] = mm(_C_VC, KV_PAD)
    tpos = pl.program_id(1) * PROJ_TM + lax.broadcasted_iota(jnp.int32, (PROJ_TM, KV_PAD), 0)
    lane = lax.broadcasted_iota(jnp.int32, (PROJ_TM, KV_PAD), 1) & (HEAD_PAD - 1)
    hot = lane == HEAD_DIM + ((tpos >> SLC_BLOCK_LOG2) & (MASK_LANES - 1))
    ks_ref[0] = jnp.where(hot, 1.0, mm(_C_KS, KV_PAD)).astype(BF16)
    kw_ref[0] = mm(_C_KW, KV_PAD).astype(BF16)
    gt_ref[0] = jax.nn.sigmoid(mm(_C_G, KV_PAD) + bg_ref[...])
    chan = lax.broadcasted_iota(jnp.int32, (KV_PAD, PROJ_TM), 0) & (HEAD_PAD - 1)
    vst_ref[0] = jnp.where(chan == ONES_ROW, 1.0, _dot_nt(wvt_ref[0:KV_PAD, :], xb)).astype(BF16)
    vwt_ref[0] = jnp.where(chan == ONES_ROW, 1.0,
                           _dot_nt(wvt_ref[KV_PAD:2 * KV_PAD, :], xb)).astype(BF16)


def _pad_heads(w, n_heads):
    lead = w.shape[:-1]
    w = w.reshape(lead + (n_heads, HEAD_DIM))
    w = jnp.pad(w, [(0, 0)] * len(lead) + [(0, 0), (0, HEAD_PAD - HEAD_DIM)])
    return w.reshape(lead + (n_heads * HEAD_PAD,))


def _pad_gate_cols(w):
    lead = w.shape[:-1]
    per_group = HEADS_PER_GROUP * N_BRANCHES
    w = w.reshape(lead + (N_KV_GROUPS, per_group))
    w = jnp.pad(w, [(0, 0)] * len(lead) + [(0, 0), (0, HEAD_PAD - per_group)])
    return w.reshape(lead + (KV_PAD,))


def _proj(x, w_in, b_gate):
    bsz, s, _ = x.shape
    cuts = np.cumsum([POOL_WIDTH, N_Q_HEADS * HEAD_DIM] + [N_KV_GROUPS * HEAD_DIM] * 6)
    parts = jnp.split(w_in, [int(c) for c in cuts], axis=-1)
    cols = [parts[0], _pad_heads(parts[1], N_Q_HEADS)]
    cols += [_pad_heads(parts[i], N_KV_GROUPS) for i in (2, 3, 4, 6)]
    cols.append(_pad_gate_cols(parts[8]))
    w = jnp.concatenate(cols, axis=-1).astype(BF16)
    wvt = jnp.concatenate([_pad_heads(parts[5], N_KV_GROUPS),
                           _pad_heads(parts[7], N_KV_GROUPS)], axis=-1).T.astype(BF16)
    bg = _pad_gate_cols(b_gate).reshape(1, KV_PAD)

    def tile(width):
        return pl.BlockSpec((1, PROJ_TM, width), lambda b, i: (b, i, 0))

    def out(width, dtype):
        return jax.ShapeDtypeStruct((bsz, s, width), dtype)

    vt_spec = pl.BlockSpec((1, KV_PAD, PROJ_TM), lambda b, i: (b, 0, i))
    vt_out = jax.ShapeDtypeStruct((bsz, KV_PAD, s), BF16)
    return pl.pallas_call(
        _proj_kernel,
        grid=(bsz, s // PROJ_TM),
        in_specs=[
            tile(D_MODEL),
            pl.BlockSpec((D_MODEL, _C_END), lambda b, i: (0, 0), pipeline_mode=pl.Buffered(1)),
            pl.BlockSpec((2 * KV_PAD, D_MODEL), lambda b, i: (0, 0), pipeline_mode=pl.Buffered(1)),
            pl.BlockSpec((1, KV_PAD), lambda b, i: (0, 0)),
        ],
        out_specs=[tile(POOL_WIDTH), tile(Q_PAD), tile(KV_PAD), tile(KV_PAD), tile(KV_PAD),
                   tile(KV_PAD), tile(KV_PAD), vt_spec, vt_spec],
        out_shape=[out(POOL_WIDTH, F32), out(Q_PAD, BF16), out(KV_PAD, F32), out(KV_PAD, F32),
                   out(KV_PAD, BF16), out(KV_PAD, BF16), out(KV_PAD, F32), vt_out, vt_out],
        compiler_params=_params("parallel", "parallel"),
        name="in_proj",
    )(x, w, wvt, bg)


def _gelu_tanh(x):
    c = np.float32(np.sqrt(2.0 / np.pi))
    return x * (0.5 * (1.0 + jnp.tanh(c * (x + 0.044715 * (x * x * x)))))


def _compress_kernel(x_ref, plo_ref, phi_ref, wlo_ref, whi_ref, w2_ref, o_ref, a_scr, b_scr,
                     *, channel_major):
    r = pl.program_id(1)
    nsb = x_ref.shape[1]
    xr = x_ref[0]
    a_scr[r] = _dot((xr + plo_ref[...]).astype(BF16), wlo_ref[...])
    b_scr[r, 0:nsb, :] = _dot((xr + phi_ref[...]).astype(BF16), whi_ref[...])

    @pl.when(r == 0)
    def _():
        b_scr[0, nsb:nsb + 8, :] = jnp.zeros((8, KV_PAD), F32)

    @pl.when(r == CMP_PER_SLC - 1)
    def _():
        for rr in range(CMP_PER_SLC):
            if rr < CMP_PER_SLC - 1:
                h = a_scr[rr] + b_scr[rr + 1, 0:nsb, :]
            else:
                h = a_scr[rr] + b_scr[0, 1:nsb + 1, :]
            act = _gelu_tanh(h).astype(BF16)
            if channel_major:
                o_ref[0, :, rr * nsb:(rr + 1) * nsb] = _dot_nt(w2_ref[...], act).astype(BF16)
            else:
                o_ref[0, rr * nsb:(rr + 1) * nsb, :] = _dot(act, w2_ref[...]).astype(BF16)


def _compress_weights(pos, w1, w2):
    eye_g = jnp.eye(N_KV_GROUPS, dtype=F32)
    w1r = w1.reshape(CMP_BLOCK, HEAD_DIM, HEAD_DIM)
    w1r = jnp.pad(w1r, ((0, 0), (0, HEAD_PAD - HEAD_DIM), (0, HEAD_PAD - HEAD_DIM)))
    w1c = w1r[:, None, :, None, :] * eye_g[None, :, None, :, None]
    w1c = w1c.reshape(CMP_BLOCK, KV_PAD, KV_PAD)
    wlo = w1c[:CMP_STRIDE].reshape(CHUNK_FLAT, KV_PAD).astype(BF16)
    whi = w1c[CMP_STRIDE:].reshape(CHUNK_FLAT, KV_PAD).astype(BF16)
    posp = jnp.pad(pos, ((0, 0), (0, HEAD_PAD - HEAD_DIM)))
    posp = jnp.tile(posp[:, None, :], (1, N_KV_GROUPS, 1))
    plo = posp[:CMP_STRIDE].reshape(1, CHUNK_FLAT)
    phi = posp[CMP_STRIDE:].reshape(1, CHUNK_FLAT)
    w2p = jnp.pad(w2, ((0, HEAD_PAD - HEAD_DIM), (0, HEAD_PAD - HEAD_DIM)))
    w2c = (w2p[None, :, None, :] * eye_g[:, None, :, None]).reshape(KV_PAD, KV_PAD).astype(BF16)
    return plo, phi, wlo, whi, w2c


def _compress(kv, pos, w1, w2, channel_major):
    bsz, s, _ = kv.shape
    nsb = s // SLC_BLOCK
    ncp = CMP_PER_SLC * nsb
    plo, phi, wlo, whi, w2c = _compress_weights(pos, w1, w2)
    if channel_major:
        w2c = w2c.T
    out_dims = (KV_PAD, ncp) if channel_major else (ncp, KV_PAD)
    x = kv.reshape(bsz, nsb, CMP_PER_SLC * CHUNK_FLAT)
    const = lambda b, r: (0, 0)
    return pl.pallas_call(
        functools.partial(_compress_kernel, channel_major=channel_major),
        grid=(bsz, CMP_PER_SLC),
        in_specs=[
            pl.BlockSpec((1, nsb, CHUNK_FLAT), lambda b, r: (b, 0, r)),
            pl.BlockSpec((1, CHUNK_FLAT), const),
            pl.BlockSpec((1, CHUNK_FLAT), const),
            pl.BlockSpec((CHUNK_FLAT, KV_PAD), const),
            pl.BlockSpec((CHUNK_FLAT, KV_PAD), const),
            pl.BlockSpec((KV_PAD, KV_PAD), const),
        ],
        out_specs=pl.BlockSpec((1,) + out_dims, lambda b, r: (b, 0, 0)),
        out_shape=jax.ShapeDtypeStruct((bsz,) + out_dims, BF16),
        scratch_shapes=[pltpu.VMEM((CMP_PER_SLC, nsb, KV_PAD), F32),
                        pltpu.VMEM((CMP_PER_SLC, nsb + 8, KV_PAD), F32)],
        compiler_params=_params("parallel", "arbitrary"),
        name="compress",
    )(x, plo, phi, wlo, whi, w2c)


TQ = 512
SLC_TQ = 512


def _load_q(q_ref):
    return jnp.concatenate(
        [q_ref[0, :, h * HEAD_PAD:(h + 1) * HEAD_PAD] for h in range(HEADS_PER_GROUP)], axis=0)


def _per_head(row):
    return jnp.concatenate([row] * HEADS_PER_GROUP, axis=1)


def _store_gated(o_ref, g_ref, o_t, branch):
    tq = o_ref.shape[1]
    for h in range(HEADS_PER_GROUP):
        c = h * N_BRANCHES + branch
        gate = g_ref[0, :, c:c + 1]
        o_ref[0, :, h * HEAD_PAD:(h + 1) * HEAD_PAD] = (
            o_t[:, h * tq:(h + 1) * tq].T * gate).astype(o_ref.dtype)


def _attn_specs(tq=TQ):
    q_spec = pl.BlockSpec((1, tq, GROUP_Q), lambda b, g, i: (b, i, g))
    g_spec = pl.BlockSpec((1, tq, HEAD_PAD), lambda b, g, i: (b, i, g))
    o_spec = pl.BlockSpec((1, tq, GROUP_Q), lambda b, g, i: (b, i, g))
    return q_spec, g_spec, o_spec


def _cmp_kernel(q_ref, kc_ref, vct_ref, g_ref, cend_ref, place_ref, o_ref, selq_ref, seld_ref):
    nsb = kc_ref.shape[1] // CMP_PER_SLC
    t0 = pl.program_id(2) * TQ
    half = nsb // 2
    if half % LANES == 0:
        @pl.when(t0 + TQ <= half * SLC_BLOCK)
        def _():
            _cmp_body(q_ref, kc_ref, vct_ref, g_ref, cend_ref, place_ref, o_ref, selq_ref, seld_ref, half)

        @pl.when(t0 + TQ > half * SLC_BLOCK)
        def _():
            _cmp_body(q_ref, kc_ref, vct_ref, g_ref, cend_ref, place_ref, o_ref, selq_ref, seld_ref, nsb)
    else:
        _cmp_body(q_ref, kc_ref, vct_ref, g_ref, cend_ref, place_ref, o_ref, selq_ref, seld_ref, nsb)


def _cmp_body(q_ref, kc_ref, vct_ref, g_ref, cend_ref, place_ref, o_ref, selq_ref, seld_ref, nvis):
    nsb_all = kc_ref.shape[1] // CMP_PER_SLC
    nsb = nvis
    t0 = pl.program_id(2) * TQ

    def slabs(ref_rows):
        if nvis == nsb_all:
            return ref_rows(0, CMP_PER_SLC * nsb_all)
        return jnp.concatenate([ref_rows(r * nsb_all, nvis) for r in range(CMP_PER_SLC)], axis=0)

    kc = slabs(lambda lo, n: kc_ref[0, lo:lo + n, :])
    cend = slabs(lambda lo, n: cend_ref[lo:lo + n, :])
    if nvis == nsb_all:
        vct = vct_ref[0]
    else:
        vct = jnp.concatenate([vct_ref[0, :, r * nsb_all:r * nsb_all + nvis]
                               for r in range(CMP_PER_SLC)], axis=1)

    s = _dot_nt(kc, _load_q(q_ref))
    s = s + _per_head(jnp.where(cend <= t0, 0.0, NEG_BIG))
    m = jnp.max(s, axis=0, keepdims=True)
    e = jnp.exp2(s - m)
    l = jnp.sum(e, axis=0, keepdims=True)
    tcol = t0 + lax.broadcasted_iota(jnp.int32, (1, TQ), 1)
    any_visible = _per_head(jnp.where(tcol >= CMP_BLOCK - 1, 1.0, 0.0))
    p = e * (any_visible / jnp.maximum(l, 1e-30))

    o_t = _dot(vct, p.astype(BF16))
    _store_gated(o_ref, g_ref, o_t, 0)

    imp = p[:, 0:TQ]
    for h in range(1, HEADS_PER_GROUP):
        imp = imp + p[:, h * TQ:(h + 1) * TQ]
    p0, p1, p2, p3 = (imp[r * nsb:(r + 1) * nsb, :] for r in range(CMP_PER_SLC))
    blk = lax.broadcasted_iota(jnp.int32, (nsb, TQ), 0)
    p3_prev = jnp.where(blk == 0, 0.0, pltpu.roll(p3, 1, axis=0))
    imp_slc = 0.5 * p3_prev + p0 + p1 + p2 + 0.5 * p3

    jt = (t0 + lax.broadcasted_iota(jnp.int32, (nsb, TQ), 1)) >> SLC_BLOCK_LOG2
    forced = (blk == 0) | (blk == jt) | (blk == jt - 1)
    free = (blk <= jt) & jnp.logical_not(forced)
    score = jnp.where(free, imp_slc, NEG_BIG)

    def pick(_, sc):
        mx = jnp.max(sc, axis=0, keepdims=True)
        first = jnp.min(jnp.where(sc == mx, blk, nsb), axis=0, keepdims=True)
        return jnp.where(blk == first, -jnp.inf, sc)

    picked = lax.fori_loop(0, min(N_SELECT - N_FORCED, nsb), pick, score, unroll=True) == -jnp.inf
    selected = (picked & free) | forced
    own0 = (t0 // SLC_TQ) * (SLC_TQ // SLC_BLOCK)

    def mask_lanes(keep, place):
        bias = jnp.where(keep, 0.0, NEG_BIG)
        if nvis < nsb_all:
            bias = jnp.concatenate([bias, jnp.full((nsb_all - nvis, TQ), NEG_BIG, F32)], axis=0)
        return _dot(bias.T.astype(BF16), place).astype(BF16)

    selq_ref[0, 0] = mask_lanes(selected & (blk < own0), place_ref[...])
    jrow = lax.broadcasted_iota(jnp.int32, (nsb_all, HEAD_PAD), 0)
    lane = lax.broadcasted_iota(jnp.int32, (nsb_all, HEAD_PAD), 1)
    in_super = (jrow >> (MASK_LANES.bit_length() - 1)) == (own0 // MASK_LANES)
    place_own = jnp.where(in_super & (lane == HEAD_DIM + (jrow & (MASK_LANES - 1))), 1.0, 0.0)
    seld_ref[0, 0] = mask_lanes(selected, place_own.astype(BF16))


def _mask_lane_placement(nsb):
    j = np.arange(nsb)
    place = np.zeros((nsb, (nsb // MASK_LANES) * HEAD_PAD), np.float32)
    place[j, (j // MASK_LANES) * HEAD_PAD + HEAD_DIM + j % MASK_LANES] = 1.0
    return jnp.asarray(place, BF16)


def _cmp_end_minus_token(nsb):
    row = np.arange(CMP_PER_SLC * nsb)
    end = (row % nsb) * SLC_BLOCK + (row // nsb) * CMP_STRIDE + CMP_BLOCK - 1
    return jnp.asarray(end[:, None] - np.arange(TQ)[None, :], jnp.int32)


def _cmp_attention(q, kcmp, vcmp_t, gates):
    bsz, s, _ = q.shape
    nsb = s // SLC_BLOCK
    ncp = CMP_PER_SLC * nsb
    selq_w = (nsb // MASK_LANES) * HEAD_PAD
    q_spec, g_spec, o_spec = _attn_specs()
    const = lambda b, g, i: (0, 0)
    return pl.pallas_call(
        _cmp_kernel,
        grid=(bsz, N_KV_GROUPS, s // TQ),
        in_specs=[q_spec,
                  pl.BlockSpec((1, ncp, HEAD_PAD), lambda b, g, i: (b, 0, g)),
                  pl.BlockSpec((1, HEAD_PAD, ncp), lambda b, g, i: (b, g, 0)),
                  g_spec,
                  pl.BlockSpec((ncp, TQ), const),
                  pl.BlockSpec((nsb, selq_w), const)],
        out_specs=[o_spec, pl.BlockSpec((1, 1, TQ, selq_w), lambda b, g, i: (b, g, i, 0)),
                   pl.BlockSpec((1, 1, TQ, HEAD_PAD), lambda b, g, i: (b, g, i, 0))],
        out_shape=[jax.ShapeDtypeStruct((bsz, s, Q_PAD), BF16),
                   jax.ShapeDtypeStruct((bsz, N_KV_GROUPS, s, selq_w), BF16),
                   jax.ShapeDtypeStruct((bsz, N_KV_GROUPS, s, HEAD_PAD), BF16)],
        compiler_params=_params("parallel", "parallel", "arbitrary"),
        name="cmp_attn_topk",
    )(q, kcmp, vcmp_t, gates, _cmp_end_minus_token(nsb), _mask_lane_placement(nsb))


SLC_TK = 512
SUPER_KEYS = MASK_LANES * SLC_BLOCK
TILES_PER_SUPER = SUPER_KEYS // SLC_TK


def _slc_kernel(q_ref, k_ref, vt_ref, selq_ref, seld_ref, g_ref, o_ref, qm_scr, s_scr):
    n_super = selq_ref.shape[3] // HEAD_PAD
    t0 = pl.multiple_of(pl.program_id(2) * SLC_TQ, SLC_TQ)
    n_full = t0 // SUPER_KEYS
    tail_tiles = (t0 - n_full * SUPER_KEYS + SLC_TK - 1) // SLC_TK

    def masked_q(slab):
        return jnp.concatenate([q_ref[0, :, h * HEAD_PAD:(h + 1) * HEAD_PAD] + slab
                                for h in range(HEADS_PER_GROUP)], axis=0)

    for st in range(n_super):
        @pl.when(st * SUPER_KEYS < t0)
        def _():
            qm_scr[st] = masked_q(selq_ref[0, 0, :, st * HEAD_PAD:(st + 1) * HEAD_PAD])

    s = _dot_nt(k_ref[0, pl.ds(t0, SLC_TQ), :], masked_q(seld_ref[0, 0]))
    kk = lax.broadcasted_iota(jnp.int32, (SLC_TQ, SLC_TQ), 0)
    tt = lax.broadcasted_iota(jnp.int32, (SLC_TQ, SLC_TQ), 1)
    s = s + _per_head(jnp.where(kk <= tt, 0.0, NEG_BIG))
    m = jnp.max(s, axis=0, keepdims=True)
    acc = _dot(vt_ref[0, :, pl.ds(t0, SLC_TQ)], jnp.exp2(s - m).astype(BF16))

    def sweep(st, carry, n_keys):
        m, acc = carry
        k0 = pl.multiple_of(st * SUPER_KEYS, SUPER_KEYS)
        sc = _dot_nt(k_ref[0, pl.ds(k0, n_keys), :], qm_scr[st])
        s_scr[0:n_keys, :] = sc
        m_new = jnp.maximum(m, jnp.max(sc, axis=0, keepdims=True))
        acc = jnp.exp2(m - m_new) * acc
        for c in range(0, n_keys, SLC_TK):
            p = jnp.exp2(s_scr[c:c + SLC_TK, :] - m_new).astype(BF16)
            acc = acc + _dot(vt_ref[0, :, pl.ds(k0 + c, SLC_TK)], p)
        return m_new, acc

    carry = lax.fori_loop(0, n_full, functools.partial(sweep, n_keys=SUPER_KEYS), (m, acc))
    branches = [lambda c: c] + [functools.partial(sweep, n_full, n_keys=n * SLC_TK)
                                for n in range(1, TILES_PER_SUPER + 1)]
    _, acc = lax.switch(tail_tiles, branches, carry)
    _store_gated(o_ref, g_ref, acc / jnp.maximum(acc[ONES_ROW:ONES_ROW + 1, :], 1e-30), 1)


def _slc_attention(q, ks, vs_t, selq, seld, gates):
    bsz, s, _ = q.shape
    n_super = s // SUPER_KEYS
    q_spec, g_spec, o_spec = _attn_specs(SLC_TQ)
    return pl.pallas_call(
        _slc_kernel,
        grid=(bsz, N_KV_GROUPS, s // SLC_TQ),
        in_specs=[q_spec,
                  pl.BlockSpec((1, s, HEAD_PAD), lambda b, g, i: (b, 0, g)),
                  pl.BlockSpec((1, HEAD_PAD, s), lambda b, g, i: (b, g, 0)),
                  pl.BlockSpec((1, 1, SLC_TQ, n_super * HEAD_PAD), lambda b, g, i: (b, g, i, 0)),
                  pl.BlockSpec((1, 1, SLC_TQ, HEAD_PAD), lambda b, g, i: (b, g, i, 0)),
                  g_spec],
        out_specs=o_spec,
        out_shape=jax.ShapeDtypeStruct((bsz, s, Q_PAD), BF16),
        scratch_shapes=[pltpu.VMEM((n_super, HEADS_PER_GROUP * SLC_TQ, HEAD_PAD), BF16),
                        pltpu.VMEM((SUPER_KEYS, HEADS_PER_GROUP * SLC_TQ), F32)],
        compiler_params=_params("parallel", "parallel", "arbitrary"),
        name="slc_attn",
    )(q, ks, vs_t, selq, seld, gates)


WIN_TQ = 256
WIN_KEYS = WINDOW + WIN_TQ


def _win_kernel(q_ref, k_ref, vt_ref, g_ref, o_ref):
    t0 = pl.program_id(2) * WIN_TQ
    k0 = pl.multiple_of(jnp.maximum(t0 - WINDOW, 0), WIN_TQ)
    s = _dot_nt(k_ref[0, pl.ds(k0, WIN_KEYS), :], _load_q(q_ref))
    diff = ((t0 + lax.broadcasted_iota(jnp.int32, (WIN_KEYS, WIN_TQ), 1))
            - (k0 + lax.broadcasted_iota(jnp.int32, (WIN_KEYS, WIN_TQ), 0)))
    s = s + _per_head(jnp.where((diff >= 0) & (diff < WINDOW), 0.0, NEG_BIG))
    m = jnp.max(s, axis=0, keepdims=True)
    e = jnp.exp2(s - m).astype(BF16)
    o_t = _dot(vt_ref[0, :, pl.ds(k0, WIN_KEYS)], e)
    _store_gated(o_ref, g_ref, o_t / jnp.maximum(o_t[ONES_ROW:ONES_ROW + 1, :], 1e-30), 2)


def _win_attention(q, kw, vw_t, gates):
    bsz, s, _ = q.shape
    q_spec, g_spec, o_spec = _attn_specs(WIN_TQ)
    return pl.pallas_call(
        _win_kernel,
        grid=(bsz, N_KV_GROUPS, s // WIN_TQ),
        in_specs=[q_spec,
                  pl.BlockSpec((1, s, HEAD_PAD), lambda b, g, i: (b, 0, g)),
                  pl.BlockSpec((1, HEAD_PAD, s), lambda b, g, i: (b, g, 0)),
                  g_spec],
        out_specs=o_spec,
        out_shape=jax.ShapeDtypeStruct((bsz, s, Q_PAD), BF16),
        compiler_params=_params("parallel", "parallel", "arbitrary"),
        name="win_attn",
    )(q, kw, vw_t, gates)


MIX_TM = 512
POOL_HALO = 16


def _mix_out_kernel(x_ref, u_ref, halo_ref, oc_ref, os_ref, ow_ref, pw_ref, ps_ref,
                    wo_ref, g_ref, b_ref, o_ref, ext_scr):
    i = pl.program_id(1)
    u = u_ref[0]
    halo = jnp.where(i == 0, 0.0, halo_ref[0])
    ext_scr[0:POOL_HALO, :] = halo
    ext_scr[POOL_HALO:POOL_HALO + MIX_TM, :] = u

    lane = lax.broadcasted_iota(jnp.int32, (MIX_TM, POOL_WIDTH), 1)
    tpos = i * MIX_TM + lax.broadcasted_iota(jnp.int32, (MIX_TM, POOL_WIDTH), 0)
    grp = lane >> (POOL_GROUP_DIM.bit_length() - 1)
    run = u
    win_sum = jnp.zeros_like(u)
    cnt = jnp.zeros_like(u)
    done = 1
    for gidx, w in enumerate(POOL_WINDOWS):
        for kback in range(done, w):
            run = run + ext_scr[POOL_HALO - kback:POOL_HALO - kback + MIX_TM, :]
        done = w
        win_sum = jnp.where(grp == gidx, run, win_sum)
        cnt = jnp.where(grp == gidx, jnp.minimum(tpos + 1, w).astype(F32), cnt)
    pooled = win_sum / cnt - u
    mixed = _dot(pooled.astype(BF16), pw_ref[...]) * ps_ref[...]

    y_nsa = oc_ref[0].astype(F32) + os_ref[0].astype(F32) + ow_ref[0].astype(F32)
    y = (_dot(mixed.astype(BF16), wo_ref[0:POOL_WIDTH, :])
         + _dot(y_nsa.astype(BF16), wo_ref[POOL_WIDTH:, :]))
    z = ALPHA * x_ref[0] + y
    o_ref[0] = _layer_norm(z, g_ref[...], b_ref[...])


def _mix_out(x, u, o_cmp, o_slc, o_win, pool_w, pool_scale, w_out, g, b):
    bsz, s, _ = x.shape
    n_grp = len(POOL_WINDOWS)
    eye = jnp.eye(n_grp, dtype=F32)
    pw = (pool_w[:, :, None, :] * eye[:, None, :, None]).reshape(POOL_WIDTH, POOL_WIDTH).astype(BF16)
    wo_nsa = w_out[POOL_WIDTH:].reshape(N_Q_HEADS, HEAD_DIM, D_MODEL)
    wo_nsa = jnp.pad(wo_nsa, ((0, 0), (0, HEAD_PAD - HEAD_DIM), (0, 0))).reshape(Q_PAD, D_MODEL)
    wo = jnp.concatenate([w_out[:POOL_WIDTH], wo_nsa], axis=0).astype(BF16)

    def tile(width):
        return pl.BlockSpec((1, MIX_TM, width), lambda bb, i: (bb, i, 0))

    halo_blocks = MIX_TM // POOL_HALO
    const = lambda bb, i: (0, 0)
    return pl.pallas_call(
        _mix_out_kernel,
        grid=(bsz, s // MIX_TM),
        in_specs=[
            tile(D_MODEL), tile(POOL_WIDTH),
            pl.BlockSpec((1, POOL_HALO, POOL_WIDTH),
                         lambda bb, i: (bb, jnp.maximum(i * halo_blocks - 1, 0), 0)),
            tile(Q_PAD), tile(Q_PAD), tile(Q_PAD),
            pl.BlockSpec((POOL_WIDTH, POOL_WIDTH), const),
            pl.BlockSpec((1, POOL_WIDTH), const),
            pl.BlockSpec((POOL_WIDTH + Q_PAD, D_MODEL), const),
            pl.BlockSpec((1, D_MODEL), const),
            pl.BlockSpec((1, D_MODEL), const),
        ],
        out_specs=tile(D_MODEL),
        out_shape=jax.ShapeDtypeStruct((bsz, s, D_MODEL), F32),
        scratch_shapes=[pltpu.VMEM((POOL_HALO + MIX_TM, POOL_WIDTH), F32)],
        compiler_params=_params("parallel", "arbitrary"),
        name="mix_out_ln",
    )(x, u, u, o_cmp, o_slc, o_win, pw, pool_scale.reshape(1, -1), wo,
      g.reshape(1, -1), b.reshape(1, -1))


def kernel(x, ln1_g, ln1_b, ffn1_w_gate, ffn1_w_up, ffn1_w_down, w_in, b_gate, pool_w, pool_scale, cmp_pos_k, cmp_k_w1, cmp_k_w2, cmp_pos_v, cmp_v_w1, cmp_v_w2, w_out, ln2_g, ln2_b, ffn2_w_gate, ffn2_w_up, ffn2_w_down, ln3_g, ln3_b):
    bsz, s, d = x.shape
    assert d == D_MODEL and s % max(SUPER_KEYS, FFN_TM, MIX_TM, PROJ_TM) == 0
    for l in range(DEPTH):
        x = _ffn_ln(x.reshape(bsz * s, d), ffn1_w_gate[l], ffn1_w_up[l], ffn1_w_down[l],
                    ln1_g[l], ln1_b[l]).reshape(bsz, s, d)
        u, q, kc, vc, ks, kw, gates, vs_t, vw_t = _proj(x, w_in[l], b_gate[l])
        kcmp = _compress(kc, cmp_pos_k[l], cmp_k_w1[l], cmp_k_w2[l], channel_major=False)
        vcmp_t = _compress(vc, cmp_pos_v[l], cmp_v_w1[l], cmp_v_w2[l], channel_major=True)
        o_cmp, selq, seld = _cmp_attention(q, kcmp, vcmp_t, gates)
        o_slc = _slc_attention(q, ks, vs_t, selq, seld, gates)
        o_win = _win_attention(q, kw, vw_t, gates)
        x = _mix_out(x, u, o_cmp, o_slc, o_win, pool_w[l], pool_scale[l], w_out[l],
                     ln2_g[l], ln2_b[l])
        x = _ffn_ln(x.reshape(bsz * s, d), ffn2_w_gate[l], ffn2_w_up[l], ffn2_w_down[l],
                    ln3_g[l], ln3_b[l]).reshape(bsz, s, d)
    return x
```

```python
import functools

import numpy as np
import jax
import jax.numpy as jnp
from jax import lax
from jax.experimental import pallas as pl
from jax.experimental.pallas import tpu as pltpu

D_MODEL = 1024
DEPTH = 2
POOL_WIDTH = 256
POOL_WINDOWS = (2, 4, 8, 16)
POOL_GROUP_DIM = 64
N_Q_HEADS = 8
HEAD_DIM = 96
N_KV_GROUPS = 2
HEADS_PER_GROUP = 4
N_BRANCHES = 3
CMP_STRIDE = 16
CMP_BLOCK = 32
SLC_BLOCK = 64
SLC_BLOCK_LOG2 = 6
N_SELECT = 16
N_FORCED = 3
WINDOW = 512
D_FF = 2816
ALPHA = (2.0 * DEPTH) ** 0.25
LN_EPS = 1e-5
NEG_BIG = -1e30
SEL_BIG = 1e30
QK_SCALE = HEAD_DIM ** -0.5
LOG2_E = 1.4426950408889634
ONES_ROW = HEAD_DIM

LANES = 128
HEAD_PAD = LANES
Q_PAD = N_Q_HEADS * HEAD_PAD
KV_PAD = N_KV_GROUPS * HEAD_PAD
GROUP_Q = HEADS_PER_GROUP * HEAD_PAD
CMP_PER_SLC = SLC_BLOCK // CMP_STRIDE
CHUNK_FLAT = CMP_STRIDE * KV_PAD

VMEM_LIMIT = 56 * 1024 * 1024

F32 = jnp.float32
BF16 = jnp.bfloat16

_C_U = 0
_C_Q = _C_U + POOL_WIDTH
_C_KC = _C_Q + Q_PAD
_C_VC = _C_KC + KV_PAD
_C_KS = _C_VC + KV_PAD
_C_KW = _C_KS + KV_PAD
_C_G = _C_KW + KV_PAD
_C_END = _C_G + KV_PAD


def _params(*sem):
    return pltpu.CompilerParams(dimension_semantics=sem, vmem_limit_bytes=VMEM_LIMIT)


def _layer_norm(z, g, b):
    mu = jnp.mean(z, axis=-1, keepdims=True)
    zc = z - mu
    var = jnp.mean(zc * zc, axis=-1, keepdims=True)
    return zc * lax.rsqrt(var + LN_EPS) * g + b


def _dot(a, b):
    return jnp.dot(a, b, preferred_element_type=F32)


def _dot_nt(a, b):
    return lax.dot_general(a, b, (((1,), (1,)), ((), ())), preferred_element_type=F32)


FFN_TM = 1024
MXU_TILE = 256
FFN_CHUNKS = (6 * MXU_TILE, 5 * MXU_TILE)
assert sum(FFN_CHUNKS) == D_FF


def _ffn_ln_kernel(x_ref, wg_ref, wu_ref, wd_ref, g_ref, b_ref, o_ref):
    x = x_ref[...]
    xb = x.astype(BF16)
    acc = None
    lo = 0
    for width in FFN_CHUNKS:
        hg = _dot(xb, wg_ref[:, lo:lo + width])
        hu = _dot(xb, wu_ref[:, lo:lo + width])
        h = (hg * jax.nn.sigmoid(hg)) * hu
        part = _dot(h.astype(BF16), wd_ref[lo:lo + width, :])
        acc = part if acc is None else acc + part
        lo += width
    z = ALPHA * x + 0.5 * acc
    o_ref[...] = _layer_norm(z, g_ref[...], b_ref[...])


def _ffn_ln(x2d, wg, wu, wd, g, b):
    t = x2d.shape[0]
    const = lambda i: (0, 0)
    return pl.pallas_call(
        _ffn_ln_kernel,
        grid=(t // FFN_TM,),
        in_specs=[
            pl.BlockSpec((FFN_TM, D_MODEL), lambda i: (i, 0)),
            pl.BlockSpec((D_MODEL, D_FF), const, pipeline_mode=pl.Buffered(1)),
            pl.BlockSpec((D_MODEL, D_FF), const, pipeline_mode=pl.Buffered(1)),
            pl.BlockSpec((D_FF, D_MODEL), const, pipeline_mode=pl.Buffered(1)),
            pl.BlockSpec((1, D_MODEL), const),
            pl.BlockSpec((1, D_MODEL), const),
        ],
        out_specs=pl.BlockSpec((FFN_TM, D_MODEL), lambda i: (i, 0)),
        out_shape=jax.ShapeDtypeStruct((t, D_MODEL), F32),
        compiler_params=_params("parallel"),
        name="ffn_ln",
    )(x2d, wg.astype(BF16), wu.astype(BF16), wd.astype(BF16), g.reshape(1, -1), b.reshape(1, -1))


PROJ_TM = 1024
MASK_LANES = HEAD_PAD - HEAD_DIM


def _proj_kernel(x_ref, w_ref, wvt_ref, bg_ref, u_ref, q_ref, kc_ref, vc_ref, ks_ref, kw_ref,
                 gt_ref, vst_ref, vwt_ref):
    xb = x_ref[0].astype(BF16)

    def mm(lo, n):
        return _dot(xb, w_ref[:, lo:lo + n])

    u_ref[0] = mm(_C_U, POOL_WIDTH)
    q_ref[0] = (mm(_C_Q, Q_PAD) * (QK_SCALE * LOG2_E)).astype(BF16)
    kc_ref[0] = mm(_C_KC, KV_PAD)
    vc_ref[0] = mm(_C_VC, KV_PAD)
    tpos = pl.program_id(1) * PROJ_TM + lax.broadcasted_iota(jnp.int32, (PROJ_TM, KV_PAD), 0)
    lane = lax.broadcasted_iota(jnp.int32, (PROJ_TM, KV_PAD), 1) & (HEAD_PAD - 1)
    hot = lane == HEAD_DIM + ((tpos >> SLC_BLOCK_LOG2) & (MASK_LANES - 1))
    ks_ref[0] = jnp.where(hot, 1.0, mm(_C_KS, KV_PAD)).astype(BF16)
    kw_ref[0] = mm(_C_KW, KV_PAD).astype(BF16)
    gt_ref[0] = jax.nn.sigmoid(mm(_C_G, KV_PAD) + bg_ref[...])
    chan = lax.broadcasted_iota(jnp.int32, (KV_PAD, PROJ_TM), 0) & (HEAD_PAD - 1)
    vst_ref[0] = jnp.where(chan == ONES_ROW, 1.0, _dot_nt(wvt_ref[0:KV_PAD, :], xb)).astype(BF16)
    vwt_ref[0] = jnp.where(chan == ONES_ROW, 1.0,
                           _dot_nt(wvt_ref[KV_PAD:2 * KV_PAD, :], xb)).astype(BF16)


def _pad_heads(w, n_heads):
    lead = w.shape[:-1]
    w = w.reshape(lead + (n_heads, HEAD_DIM))
    w = jnp.pad(w, [(0, 0)] * len(lead) + [(0, 0), (0, HEAD_PAD - HEAD_DIM)])
    return w.reshape(lead + (n_heads * HEAD_PAD,))


def _pad_gate_cols(w):
    lead = w.shape[:-1]
    per_group = HEADS_PER_GROUP * N_BRANCHES
    w = w.reshape(lead + (N_KV_GROUPS, per_group))
    w = jnp.pad(w, [(0, 0)] * len(lead) + [(0, 0), (0, HEAD_PAD - per_group)])
    return w.reshape(lead + (KV_PAD,))


def _proj(x, w_in, b_gate):
    bsz, s, _ = x.shape
    cuts = np.cumsum([POOL_WIDTH, N_Q_HEADS * HEAD_DIM] + [N_KV_GROUPS * HEAD_DIM] * 6)
    parts = jnp.split(w_in, [int(c) for c in cuts], axis=-1)
    cols = [parts[0], _pad_heads(parts[1], N_Q_HEADS)]
    cols += [_pad_heads(parts[i], N_KV_GROUPS) for i in (2, 3, 4, 6)]
    cols.append(_pad_gate_cols(parts[8]))
    w = jnp.concatenate(cols, axis=-1).astype(BF16)
    wvt = jnp.concatenate([_pad_heads(parts[5], N_KV_GROUPS),
                           _pad_heads(parts[7], N_KV_GROUPS)], axis=-1).T.astype(BF16)
    bg = _pad_gate_cols(b_gate).reshape(1, KV_PAD)

    def tile(width):
        return pl.BlockSpec((1, PROJ_TM, width), lambda b, i: (b, i, 0))

    def out(width, dtype):
        return jax.ShapeDtypeStruct((bsz, s, width), dtype)

    vt_spec = pl.BlockSpec((1, KV_PAD, PROJ_TM), lambda b, i: (b, 0, i))
    vt_out = jax.ShapeDtypeStruct((bsz, KV_PAD, s), BF16)
    return pl.pallas_call(
        _proj_kernel,
        grid=(bsz, s // PROJ_TM),
        in_specs=[
            tile(D_MODEL),
            pl.BlockSpec((D_MODEL, _C_END), lambda b, i: (0, 0), pipeline_mode=pl.Buffered(1)),
            pl.BlockSpec((2 * KV_PAD, D_MODEL), lambda b, i: (0, 0), pipeline_mode=pl.Buffered(1)),
            pl.BlockSpec((1, KV_PAD), lambda b, i: (0, 0)),
        ],
        out_specs=[tile(POOL_WIDTH), tile(Q_PAD), tile(KV_PAD), tile(KV_PAD), tile(KV_PAD),
                   tile(KV_PAD), tile(KV_PAD), vt_spec, vt_spec],
        out_shape=[out(POOL_WIDTH, F32), out(Q_PAD, BF16), out(KV_PAD, F32), out(KV_PAD, F32),
                   out(KV_PAD, BF16), out(KV_PAD, BF16), out(KV_PAD, F32), vt_out, vt_out],
        compiler_params=_params("parallel", "parallel"),
        name="in_proj",
    )(x, w, wvt, bg)


def _gelu_tanh(x):
    c = np.float32(np.sqrt(2.0 / np.pi))
    return x * (0.5 * (1.0 + jnp.tanh(c * (x + 0.044715 * (x * x * x)))))


def _compress_kernel(x_ref, plo_ref, phi_ref, wlo_ref, whi_ref, w2_ref, o_ref, a_scr, b_scr,
                     *, channel_major):
    r = pl.program_id(1)
    nsb = x_ref.shape[1]
    xr = x_ref[0]
    a_scr[r] = _dot((xr + plo_ref[...]).astype(BF16), wlo_ref[...])
    b_scr[r, 0:nsb, :] = _dot((xr + phi_ref[...]).astype(BF16), whi_ref[...])

    @pl.when(r == 0)
    def _():
        b_scr[0, nsb:nsb + 8, :] = jnp.zeros((8, KV_PAD), F32)

    @pl.when(r == CMP_PER_SLC - 1)
    def _():
        for rr in range(CMP_PER_SLC):
            if rr < CMP_PER_SLC - 1:
                h = a_scr[rr] + b_scr[rr + 1, 0:nsb, :]
            else:
                h = a_scr[rr] + b_scr[0, 1:nsb + 1, :]
            act = _gelu_tanh(h).astype(BF16)
            if channel_major:
                o_ref[0, :, rr * nsb:(rr + 1) * nsb] = _dot_nt(w2_ref[...], act).astype(BF16)
            else:
                o_ref[0, rr * nsb:(rr + 1) * nsb, :] = _dot(act, w2_ref[...]).astype(BF16)


def _compress_weights(pos, w1, w2):
    eye_g = jnp.eye(N_KV_GROUPS, dtype=F32)
    w1r = w1.reshape(CMP_BLOCK, HEAD_DIM, HEAD_DIM)
    w1r = jnp.pad(w1r, ((0, 0), (0, HEAD_PAD - HEAD_DIM), (0, HEAD_PAD - HEAD_DIM)))
    w1c = w1r[:, None, :, None, :] * eye_g[None, :, None, :, None]
    w1c = w1c.reshape(CMP_BLOCK, KV_PAD, KV_PAD)
    wlo = w1c[:CMP_STRIDE].reshape(CHUNK_FLAT, KV_PAD).astype(BF16)
    whi = w1c[CMP_STRIDE:].reshape(CHUNK_FLAT, KV_PAD).astype(BF16)
    posp = jnp.pad(pos, ((0, 0), (0, HEAD_PAD - HEAD_DIM)))
    posp = jnp.tile(posp[:, None, :], (1, N_KV_GROUPS, 1))
    plo = posp[:CMP_STRIDE].reshape(1, CHUNK_FLAT)
    phi = posp[CMP_STRIDE:].reshape(1, CHUNK_FLAT)
    w2p = jnp.pad(w2, ((0, HEAD_PAD - HEAD_DIM), (0, HEAD_PAD - HEAD_DIM)))
    w2c = (w2p[None, :, None, :] * eye_g[:, None, :, None]).reshape(KV_PAD, KV_PAD).astype(BF16)
    return plo, phi, wlo, whi, w2c


def _compress(kv, pos, w1, w2, channel_major):
    bsz, s, _ = kv.shape
    nsb = s // SLC_BLOCK
    ncp = CMP_PER_SLC * nsb
    plo, phi, wlo, whi, w2c = _compress_weights(pos, w1, w2)
    if channel_major:
        w2c = w2c.T
    out_dims = (KV_PAD, ncp) if channel_major else (ncp, KV_PAD)
    x = kv.reshape(bsz, nsb, CMP_PER_SLC * CHUNK_FLAT)
    const = lambda b, r: (0, 0)
    return pl.pallas_call(
        functools.partial(_compress_kernel, channel_major=channel_major),
        grid=(bsz, CMP_PER_SLC),
        in_specs=[
            pl.BlockSpec((1, nsb, CHUNK_FLAT), lambda b, r: (b, 0, r)),
            pl.BlockSpec((1, CHUNK_FLAT), const),
            pl.BlockSpec((1, CHUNK_FLAT), const),
            pl.BlockSpec((CHUNK_FLAT, KV_PAD), const),
            pl.BlockSpec((CHUNK_FLAT, KV_PAD), const),
            pl.BlockSpec((KV_PAD, KV_PAD), const),
        ],
        out_specs=pl.BlockSpec((1,) + out_dims, lambda b, r: (b, 0, 0)),
        out_shape=jax.ShapeDtypeStruct((bsz,) + out_dims, BF16),
        scratch_shapes=[pltpu.VMEM((CMP_PER_SLC, nsb, KV_PAD), F32),
                        pltpu.VMEM((CMP_PER_SLC, nsb + 8, KV_PAD), F32)],
        compiler_params=_params("parallel", "arbitrary"),
        name="compress",
    )(x, plo, phi, wlo, whi, w2c)


TQ = 512
SLC_TQ = 512


def _load_q(q_ref):
    return jnp.concatenate(
        [q_ref[0, :, h * HEAD_PAD:(h + 1) * HEAD_PAD] for h in range(HEADS_PER_GROUP)], axis=0)


def _per_head(row):
    return jnp.concatenate([row] * HEADS_PER_GROUP, axis=1)


def _store_gated(o_ref, g_ref, o_t, branch):
    tq = o_ref.shape[1]
    for h in range(HEADS_PER_GROUP):
        c = h * N_BRANCHES + branch
        gate = g_ref[0, :, c:c + 1]
        o_ref[0, :, h * HEAD_PAD:(h + 1) * HEAD_PAD] = (
            o_t[:, h * tq:(h + 1) * tq].T * gate).astype(o_ref.dtype)


def _attn_specs(tq=TQ):
    q_spec = pl.BlockSpec((1, tq, GROUP_Q), lambda b, g, i: (b, i, g))
    g_spec = pl.BlockSpec((1, tq, HEAD_PAD), lambda b, g, i: (b, i, g))
    o_spec = pl.BlockSpec((1, tq, GROUP_Q), lambda b, g, i: (b, i, g))
    return q_spec, g_spec, o_spec


def _cmp_kernel(q_ref, kc_ref, vct_ref, g_ref, cend_ref, place_ref, o_ref, selq_ref, seld_ref):
    nsb = kc_ref.shape[1] // CMP_PER_SLC
    t0 = pl.program_id(2) * TQ
    half = nsb // 2
    if half % LANES == 0:
        @pl.when(t0 + TQ <= half * SLC_BLOCK)
        def _():
            _cmp_body(q_ref, kc_ref, vct_ref, g_ref, cend_ref, place_ref, o_ref, selq_ref, seld_ref, half)

        @pl.when(t0 + TQ > half * SLC_BLOCK)
        def _():
            _cmp_body(q_ref, kc_ref, vct_ref, g_ref, cend_ref, place_ref, o_ref, selq_ref, seld_ref, nsb)
    else:
        _cmp_body(q_ref, kc_ref, vct_ref, g_ref, cend_ref, place_ref, o_ref, selq_ref, seld_ref, nsb)


def _cmp_body(q_ref, kc_ref, vct_ref, g_ref, cend_ref, place_ref, o_ref, selq_ref, seld_ref, nvis):
    nsb_all = kc_ref.shape[1] // CMP_PER_SLC
    nsb = nvis
    t0 = pl.program_id(2) * TQ

    def slabs(ref_rows):
        if nvis == nsb_all:
            return ref_rows(0, CMP_PER_SLC * nsb_all)
        return jnp.concatenate([ref_rows(r * nsb_all, nvis) for r in range(CMP_PER_SLC)], axis=0)

    kc = slabs(lambda lo, n: kc_ref[0, lo:lo + n, :])
    cend = slabs(lambda lo, n: cend_ref[lo:lo + n, :])
    if nvis == nsb_all:
        vct = vct_ref[0]
    else:
        vct = jnp.concatenate([vct_ref[0, :, r * nsb_all:r * nsb_all + nvis]
                               for r in range(CMP_PER_SLC)], axis=1)

    s = _dot_nt(kc, _load_q(q_ref))
    s = s + _per_head(jnp.where(cend <= t0, 0.0, NEG_BIG))
    m = jnp.max(s, axis=0, keepdims=True)
    e = jnp.exp2(s - m)
    l = jnp.sum(e, axis=0, keepdims=True)
    tcol = t0 + lax.broadcasted_iota(jnp.int32, (1, TQ), 1)
    any_visible = _per_head(jnp.where(tcol >= CMP_BLOCK - 1, 1.0, 0.0))
    p = e * (any_visible / jnp.maximum(l, 1e-30))

    o_t = _dot(vct, p.astype(BF16))
    _store_gated(o_ref, g_ref, o_t, 0)

    imp = p[:, 0:TQ]
    for h in range(1, HEADS_PER_GROUP):
        imp = imp + p[:, h * TQ:(h + 1) * TQ]
    p0, p1, p2, p3 = (imp[r * nsb:(r + 1) * nsb, :] for r in range(CMP_PER_SLC))
    blk = lax.broadcasted_iota(jnp.int32, (nsb, TQ), 0)
    p3_prev = jnp.where(blk == 0, 0.0, pltpu.roll(p3, 1, axis=0))
    imp_slc = 0.5 * p3_prev + p0 + p1 + p2 + 0.5 * p3

    jt = (t0 + lax.broadcasted_iota(jnp.int32, (nsb, TQ), 1)) >> SLC_BLOCK_LOG2
    forced = (blk == 0) | (blk == jt) | (blk == jt - 1)
    free = (blk <= jt) & jnp.logical_not(forced)
    score = jnp.where(free, imp_slc, NEG_BIG)

    def pick(_, sc):
        mx = jnp.max(sc, axis=0, keepdims=True)
        first = jnp.min(jnp.where(sc == mx, blk, nsb), axis=0, keepdims=True)
        return jnp.where(blk == first, -jnp.inf, sc)

    picked = lax.fori_loop(0, min(N_SELECT - N_FORCED, nsb), pick, score, unroll=True) == -jnp.inf
    selected = (picked & free) | forced
    own0 = (t0 // SLC_TQ) * (SLC_TQ // SLC_BLOCK)

    def mask_lanes(keep, place):
        bias = jnp.where(keep, 0.0, NEG_BIG)
        if nvis < nsb_all:
            bias = jnp.concatenate([bias, jnp.full((nsb_all - nvis, TQ), NEG_BIG, F32)], axis=0)
        return _dot(bias.T.astype(BF16), place).astype(BF16)

    selq_ref[0, 0] = mask_lanes(selected & (blk < own0), place_ref[...])
    jrow = lax.broadcasted_iota(jnp.int32, (nsb_all, HEAD_PAD), 0)
    lane = lax.broadcasted_iota(jnp.int32, (nsb_all, HEAD_PAD), 1)
    in_super = (jrow >> (MASK_LANES.bit_length() - 1)) == (own0 // MASK_LANES)
    place_own = jnp.where(in_super & (lane == HEAD_DIM + (jrow & (MASK_LANES - 1))), 1.0, 0.0)
    seld_ref[0, 0] = mask_lanes(selected, place_own.astype(BF16))


def _mask_lane_placement(nsb):
    j = np.arange(nsb)
    place = np.zeros((nsb, (nsb // MASK_LANES) * HEAD_PAD), np.float32)
    place[j, (j // MASK_LANES) * HEAD_PAD + HEAD_DIM + j % MASK_LANES] = 1.0
    return jnp.asarray(place, BF16)


def _cmp_end_minus_token(nsb):
    row = np.arange(CMP_PER_SLC * nsb)
    end = (row % nsb) * SLC_BLOCK + (row // nsb) * CMP_STRIDE + CMP_BLOCK - 1
    return jnp.asarray(end[:, None] - np.arange(TQ)[None, :], jnp.int32)


def _cmp_attention(q, kcmp, vcmp_t, gates):
    bsz, s, _ = q.shape
    nsb = s // SLC_BLOCK
    ncp = CMP_PER_SLC * nsb
    selq_w = (nsb // MASK_LANES) * HEAD_PAD
    q_spec, g_spec, o_spec = _attn_specs()
    const = lambda b, g, i: (0, 0)
    return pl.pallas_call(
        _cmp_kernel,
        grid=(bsz, N_KV_GROUPS, s // TQ),
        in_specs=[q_spec,
                  pl.BlockSpec((1, ncp, HEAD_PAD), lambda b, g, i: (b, 0, g)),
                  pl.BlockSpec((1, HEAD_PAD, ncp), lambda b, g, i: (b, g, 0)),
                  g_spec,
                  pl.BlockSpec((ncp, TQ), const),
                  pl.BlockSpec((nsb, selq_w), const)],
        out_specs=[o_spec, pl.BlockSpec((1, 1, TQ, selq_w), lambda b, g, i: (b, g, i, 0)),
                   pl.BlockSpec((1, 1, TQ, HEAD_PAD), lambda b, g, i: (b, g, i, 0))],
        out_shape=[jax.ShapeDtypeStruct((bsz, s, Q_PAD), BF16),
                   jax.ShapeDtypeStruct((bsz, N_KV_GROUPS, s, selq_w), BF16),
                   jax.ShapeDtypeStruct((bsz, N_KV_GROUPS, s, HEAD_PAD), BF16)],
        compiler_params=_params("parallel", "parallel", "arbitrary"),
        name="cmp_attn_topk",
    )(q, kcmp, vcmp_t, gates, _cmp_end_minus_token(nsb), _mask_lane_placement(nsb))


SLC_TK = 512
SUPER_KEYS = MASK_LANES * SLC_BLOCK
TILES_PER_SUPER = SUPER_KEYS // SLC_TK


def _slc_kernel(q_ref, k_ref, vt_ref, selq_ref, seld_ref, g_ref, o_ref, qm_scr, s_scr):
    n_super = selq_ref.shape[3] // HEAD_PAD
    t0 = pl.multiple_of(pl.program_id(2) * SLC_TQ, SLC_TQ)
    n_full = t0 // SUPER_KEYS
    tail_tiles = (t0 - n_full * SUPER_KEYS + SLC_TK - 1) // SLC_TK

    def masked_q(slab):
        return jnp.concatenate([q_ref[0, :, h * HEAD_PAD:(h + 1) * HEAD_PAD] + slab
                                for h in range(HEADS_PER_GROUP)], axis=0)

    for st in range(n_super):
        @pl.when(st * SUPER_KEYS < t0)
        def _():
            qm_scr[st] = masked_q(selq_ref[0, 0, :, st * HEAD_PAD:(st + 1) * HEAD_PAD])

    s = _dot_nt(k_ref[0, pl.ds(t0, SLC_TQ), :], masked_q(seld_ref[0, 0]))
    kk = lax.broadcasted_iota(jnp.int32, (SLC_TQ, SLC_TQ), 0)
    tt = lax.broadcasted_iota(jnp.int32, (SLC_TQ, SLC_TQ), 1)
    s = s + _per_head(jnp.where(kk <= tt, 0.0, NEG_BIG))
    m = jnp.max(s, axis=0, keepdims=True)
    acc = _dot(vt_ref[0, :, pl.ds(t0, SLC_TQ)], jnp.exp2(s - m).astype(BF16))

    def sweep(st, carry, n_keys):
        m, acc = carry
        k0 = pl.multiple_of(st * SUPER_KEYS, SUPER_KEYS)
        sc = _dot_nt(k_ref[0, pl.ds(k0, n_keys), :], qm_scr[st])
        s_scr[0:n_keys, :] = sc
        m_new = jnp.maximum(m, jnp.max(sc, axis=0, keepdims=True))
        acc = jnp.exp2(m - m_new) * acc
        for c in range(0, n_keys, SLC_TK):
            p = jnp.exp2(s_scr[c:c + SLC_TK, :] - m_new).astype(BF16)
            acc = acc + _dot(vt_ref[0, :, pl.ds(k0 + c, SLC_TK)], p)
        return m_new, acc

    carry = lax.fori_loop(0, n_full, functools.partial(sweep, n_keys=SUPER_KEYS), (m, acc))
    branches = [lambda c: c] + [functools.partial(sweep, n_full, n_keys=n * SLC_TK)
                                for n in range(1, TILES_PER_SUPER + 1)]
    _, acc = lax.switch(tail_tiles, branches, carry)
    _store_gated(o_ref, g_ref, acc / jnp.maximum(acc[ONES_ROW:ONES_ROW + 1, :], 1e-30), 1)


def _slc_attention(q, ks, vs_t, selq, seld, gates):
    bsz, s, _ = q.shape
    n_super = s // SUPER_KEYS
    q_spec, g_spec, o_spec = _attn_specs(SLC_TQ)
    return pl.pallas_call(
        _slc_kernel,
        grid=(bsz, N_KV_GROUPS, s // SLC_TQ),
        in_specs=[q_spec,
                  pl.BlockSpec((1, s, HEAD_PAD), lambda b, g, i: (b, 0, g)),
                  pl.BlockSpec((1, HEAD_PAD, s), lambda b, g, i: (b, g, 0)),
                  pl.BlockSpec((1, 1, SLC_TQ, n_super * HEAD_PAD), lambda b, g, i: (b, g, i, 0)),
                  pl.BlockSpec((1, 1, SLC_TQ, HEAD_PAD), lambda b, g, i: (b, g, i, 0)),
                  g_spec],
        out_specs=o_spec,
        out_shape=jax.ShapeDtypeStruct((bsz, s, Q_PAD), BF16),
        scratch_shapes=[pltpu.VMEM((n_super, HEADS_PER_GROUP * SLC_TQ, HEAD_PAD), BF16),
                        pltpu.VMEM((SUPER_KEYS, HEADS_PER_GROUP * SLC_TQ), F32)],
        compiler_params=_params("parallel", "parallel", "arbitrary"),
        name="slc_attn",
    )(q, ks, vs_t, selq, seld, gates)


WIN_TQ = 512
WIN_SUB = 256
WIN_KEYS = WINDOW + WIN_SUB


def _win_kernel(q_ref, k_ref, vt_ref, g_ref, o_ref):
    for sub in range(WIN_TQ // WIN_SUB):
        rows = slice(sub * WIN_SUB, (sub + 1) * WIN_SUB)
        t0 = pl.program_id(2) * WIN_TQ + sub * WIN_SUB
        k0 = pl.multiple_of(jnp.maximum(t0 - WINDOW, 0), WIN_SUB)
        q = jnp.concatenate([q_ref[0, rows, h * HEAD_PAD:(h + 1) * HEAD_PAD]
                             for h in range(HEADS_PER_GROUP)], axis=0)
        s = _dot_nt(k_ref[0, pl.ds(k0, WIN_KEYS), :], q)
        diff = ((t0 + lax.broadcasted_iota(jnp.int32, (WIN_KEYS, WIN_SUB), 1))
                - (k0 + lax.broadcasted_iota(jnp.int32, (WIN_KEYS, WIN_SUB), 0)))
        s = s + _per_head(jnp.where((diff >= 0) & (diff < WINDOW), 0.0, NEG_BIG))
        m = jnp.max(s, axis=0, keepdims=True)
        e = jnp.exp2(s - m).astype(BF16)
        o_t = _dot(vt_ref[0, :, pl.ds(k0, WIN_KEYS)], e)
        o_t = o_t / jnp.maximum(o_t[ONES_ROW:ONES_ROW + 1, :], 1e-30)
        for h in range(HEADS_PER_GROUP):
            c = h * N_BRANCHES + 2
            gate = g_ref[0, rows, c:c + 1]
            o_ref[0, rows, h * HEAD_PAD:(h + 1) * HEAD_PAD] = (
                o_t[:, h * WIN_SUB:(h + 1) * WIN_SUB].T * gate).astype(o_ref.dtype)


def _win_attention(q, kw, vw_t, gates):
    bsz, s, _ = q.shape
    q_spec, g_spec, o_spec = _attn_specs(WIN_TQ)
    return pl.pallas_call(
        _win_kernel,
        grid=(bsz, N_KV_GROUPS, s // WIN_TQ),
        in_specs=[q_spec,
                  pl.BlockSpec((1, s, HEAD_PAD), lambda b, g, i: (b, 0, g)),
                  pl.BlockSpec((1, HEAD_PAD, s), lambda b, g, i: (b, g, 0)),
                  g_spec],
        out_specs=o_spec,
        out_shape=jax.ShapeDtypeStruct((bsz, s, Q_PAD), BF16),
        compiler_params=_params("parallel", "parallel", "arbitrary"),
        name="win_attn",
    )(q, kw, vw_t, gates)


MIX_TM = 1024
POOL_HALO = 16


def _mix_out_kernel(x_ref, u_ref, halo_ref, oc_ref, os_ref, ow_ref, pw_ref, ps_ref,
                    wo_ref, g_ref, b_ref, o_ref, ext_scr):
    i = pl.program_id(1)
    u = u_ref[0]
    halo = jnp.where(i == 0, 0.0, halo_ref[0])
    ext_scr[0:POOL_HALO, :] = halo
    ext_scr[POOL_HALO:POOL_HALO + MIX_TM, :] = u

    lane = lax.broadcasted_iota(jnp.int32, (MIX_TM, POOL_WIDTH), 1)
    tpos = i * MIX_TM + lax.broadcasted_iota(jnp.int32, (MIX_TM, POOL_WIDTH), 0)
    grp = lane >> (POOL_GROUP_DIM.bit_length() - 1)
    run = u
    win_sum = jnp.zeros_like(u)
    cnt = jnp.zeros_like(u)
    done = 1
    for gidx, w in enumerate(POOL_WINDOWS):
        for kback in range(done, w):
            run = run + ext_scr[POOL_HALO - kback:POOL_HALO - kback + MIX_TM, :]
        done = w
        win_sum = jnp.where(grp == gidx, run, win_sum)
        cnt = jnp.where(grp == gidx, jnp.minimum(tpos + 1, w).astype(F32), cnt)
    pooled = win_sum / cnt - u
    mixed = _dot(pooled.astype(BF16), pw_ref[...]) * ps_ref[...]

    y_nsa = oc_ref[0].astype(F32) + os_ref[0].astype(F32) + ow_ref[0].astype(F32)
    y = (_dot(mixed.astype(BF16), wo_ref[0:POOL_WIDTH, :])
         + _dot(y_nsa.astype(BF16), wo_ref[POOL_WIDTH:, :]))
    z = ALPHA * x_ref[0] + y
    o_ref[0] = _layer_norm(z, g_ref[...], b_ref[...])


def _mix_out(x, u, o_cmp, o_slc, o_win, pool_w, pool_scale, w_out, g, b):
    bsz, s, _ = x.shape
    n_grp = len(POOL_WINDOWS)
    eye = jnp.eye(n_grp, dtype=F32)
    pw = (pool_w[:, :, None, :] * eye[:, None, :, None]).reshape(POOL_WIDTH, POOL_WIDTH).astype(BF16)
    wo_nsa = w_out[POOL_WIDTH:].reshape(N_Q_HEADS, HEAD_DIM, D_MODEL)
    wo_nsa = jnp.pad(wo_nsa, ((0, 0), (0, HEAD_PAD - HEAD_DIM), (0, 0))).reshape(Q_PAD, D_MODEL)
    wo = jnp.concatenate([w_out[:POOL_WIDTH], wo_nsa], axis=0).astype(BF16)

    def tile(width):
        return pl.BlockSpec((1, MIX_TM, width), lambda bb, i: (bb, i, 0))

    halo_blocks = MIX_TM // POOL_HALO
    const = lambda bb, i: (0, 0)
    return pl.pallas_call(
        _mix_out_kernel,
        grid=(bsz, s // MIX_TM),
        in_specs=[
            tile(D_MODEL), tile(POOL_WIDTH),
            pl.BlockSpec((1, POOL_HALO, POOL_WIDTH),
                         lambda bb, i: (bb, jnp.maximum(i * halo_blocks - 1, 0), 0)),
            tile(Q_PAD), tile(Q_PAD), tile(Q_PAD),
            pl.BlockSpec((POOL_WIDTH, POOL_WIDTH), const),
            pl.BlockSpec((1, POOL_WIDTH), const),
            pl.BlockSpec((POOL_WIDTH + Q_PAD, D_MODEL), const),
            pl.BlockSpec((1, D_MODEL), const),
            pl.BlockSpec((1, D_MODEL), const),
        ],
        out_specs=tile(D_MODEL),
        out_shape=jax.ShapeDtypeStruct((bsz, s, D_MODEL), F32),
        scratch_shapes=[pltpu.VMEM((POOL_HALO + MIX_TM, POOL_WIDTH), F32)],
        compiler_params=_params("parallel", "arbitrary"),
        name="mix_out_ln",
    )(x, u, u, o_cmp, o_slc, o_win, pw, pool_scale.reshape(1, -1), wo,
      g.reshape(1, -1), b.reshape(1, -1))


def kernel(x, ln1_g, ln1_b, ffn1_w_gate, ffn1_w_up, ffn1_w_down, w_in, b_gate, pool_w, pool_scale, cmp_pos_k, cmp_k_w1, cmp_k_w2, cmp_pos_v, cmp_v_w1, cmp_v_w2, w_out, ln2_g, ln2_b, ffn2_w_gate, ffn2_w_up, ffn2_w_down, ln3_g, ln3_b):
    bsz, s, d = x.shape
    assert d == D_MODEL and s % max(SUPER_KEYS, FFN_TM, MIX_TM, PROJ_TM) == 0
    for l in range(DEPTH):
        x = _ffn_ln(x.reshape(bsz * s, d), ffn1_w_gate[l], ffn1_w_up[l], ffn1_w_down[l],
                    ln1_g[l], ln1_b[l]).reshape(bsz, s, d)
        u, q, kc, vc, ks, kw, gates, vs_t, vw_t = _proj(x, w_in[l], b_gate[l])
        kcmp = _compress(kc, cmp_pos_k[l], cmp_k_w1[l], cmp_k_w2[l], channel_major=False)
        vcmp_t = _compress(vc, cmp_pos_v[l], cmp_v_w1[l], cmp_v_w2[l], channel_major=True)
        o_cmp, selq, seld = _cmp_attention(q, kcmp, vcmp_t, gates)
        o_slc = _slc_attention(q, ks, vs_t, selq, seld, gates)
        o_win = _win_attention(q, kw, vw_t, gates)
        x = _mix_out(x, u, o_cmp, o_slc, o_win, pool_w[l], pool_scale[l], w_out[l],
                     ln2_g[l], ln2_b[l])
        x = _ffn_ln(x.reshape(bsz * s, d), ffn2_w_gate[l], ffn2_w_up[l], ffn2_w_down[l],
                    ln3_g[l], ln3_b[l]).reshape(bsz, s, d)
    return x
```

```python
import functools

import numpy as np
import jax
import jax.numpy as jnp
from jax import lax
from jax.experimental import pallas as pl
from jax.experimental.pallas import tpu as pltpu

D_MODEL = 1024
DEPTH = 2
POOL_WIDTH = 256
POOL_WINDOWS = (2, 4, 8, 16)
POOL_GROUP_DIM = 64
N_Q_HEADS = 8
HEAD_DIM = 96
N_KV_GROUPS = 2
HEADS_PER_GROUP = 4
N_BRANCHES = 3
CMP_STRIDE = 16
CMP_BLOCK = 32
SLC_BLOCK = 64
SLC_BLOCK_LOG2 = 6
N_SELECT = 16
N_FORCED = 3
WINDOW = 512
D_FF = 2816
ALPHA = (2.0 * DEPTH) ** 0.25
LN_EPS = 1e-5
NEG_BIG = -1e30
SEL_BIG = 1e30
QK_SCALE = HEAD_DIM ** -0.5
LOG2_E = 1.4426950408889634
ONES_ROW = HEAD_DIM

LANES = 128
HEAD_PAD = LANES
Q_PAD = N_Q_HEADS * HEAD_PAD
KV_PAD = N_KV_GROUPS * HEAD_PAD
GROUP_Q = HEADS_PER_GROUP * HEAD_PAD
CMP_PER_SLC = SLC_BLOCK // CMP_STRIDE
CHUNK_FLAT = CMP_STRIDE * KV_PAD

VMEM_LIMIT = 56 * 1024 * 1024

F32 = jnp.float32
BF16 = jnp.bfloat16

_C_U = 0
_C_Q = _C_U + POOL_WIDTH
_C_KC = _C_Q + Q_PAD
_C_VC = _C_KC + KV_PAD
_C_KS = _C_VC + KV_PAD
_C_KW = _C_KS + KV_PAD
_C_END = _C_KW + KV_PAD


def _params(*sem):
    return pltpu.CompilerParams(dimension_semantics=sem, vmem_limit_bytes=VMEM_LIMIT)


def _layer_norm(z, g, b):
    mu = jnp.mean(z, axis=-1, keepdims=True)
    zc = z - mu
    var = jnp.mean(zc * zc, axis=-1, keepdims=True)
    return zc * lax.rsqrt(var + LN_EPS) * g + b


def _dot(a, b):
    return jnp.dot(a, b, preferred_element_type=F32)


def _dot_nt(a, b):
    return lax.dot_general(a, b, (((1,), (1,)), ((), ())), preferred_element_type=F32)


FFN_TM = 1024
MXU_TILE = 256
FFN_CHUNKS = (6 * MXU_TILE, 5 * MXU_TILE)
assert sum(FFN_CHUNKS) == D_FF


def _ffn_ln_kernel(x_ref, wg_ref, wu_ref, wd_ref, g_ref, b_ref, o_ref):
    x = x_ref[...]
    xb = x.astype(BF16)
    acc = None
    lo = 0
    for width in FFN_CHUNKS:
        hg = _dot(xb, wg_ref[:, lo:lo + width])
        hu = _dot(xb, wu_ref[:, lo:lo + width])
        h = (hg * jax.nn.sigmoid(hg)) * hu
        part = _dot(h.astype(BF16), wd_ref[lo:lo + width, :])
        acc = part if acc is None else acc + part
        lo += width
    z = ALPHA * x + 0.5 * acc
    o_ref[...] = _layer_norm(z, g_ref[...], b_ref[...])


def _ffn_ln(x2d, wg, wu, wd, g, b):
    t = x2d.shape[0]
    const = lambda i: (0, 0)
    return pl.pallas_call(
        _ffn_ln_kernel,
        grid=(t // FFN_TM,),
        in_specs=[
            pl.BlockSpec((FFN_TM, D_MODEL), lambda i: (i, 0)),
            pl.BlockSpec((D_MODEL, D_FF), const, pipeline_mode=pl.Buffered(1)),
            pl.BlockSpec((D_MODEL, D_FF), const, pipeline_mode=pl.Buffered(1)),
            pl.BlockSpec((D_FF, D_MODEL), const, pipeline_mode=pl.Buffered(1)),
            pl.BlockSpec((1, D_MODEL), const),
            pl.BlockSpec((1, D_MODEL), const),
        ],
        out_specs=pl.BlockSpec((FFN_TM, D_MODEL), lambda i: (i, 0)),
        out_shape=jax.ShapeDtypeStruct((t, D_MODEL), F32),
        compiler_params=_params("parallel"),
        name="ffn_ln",
    )(x2d, wg.astype(BF16), wu.astype(BF16), wd.astype(BF16), g.reshape(1, -1), b.reshape(1, -1))


PROJ_TM = 1024
MASK_LANES = HEAD_PAD - HEAD_DIM


def _proj_kernel(x_ref, w_ref, wvt_ref, bg_ref, u_ref, q_ref, kc_ref, vc_ref, ks_ref, kw_ref,
                 gt_ref, vst_ref, vwt_ref):
    xb = x_ref[0].astype(BF16)

    def mm(lo, n):
        return _dot(xb, w_ref[:, lo:lo + n])

    u_ref[0] = mm(_C_U, POOL_WIDTH)
    q_ref[0] = (mm(_C_Q, Q_PAD) * (QK_SCALE * LOG2_E)).astype(BF16)
    kc_ref[0] = mm(_C_KC, KV_PAD)
    vc_ref[0] = mm(_C_VC, KV_PAD)
    tpos = pl.program_id(1) * PROJ_TM + lax.broadcasted_iota(jnp.int32, (PROJ_TM, KV_PAD), 0)
    lane = lax.broadcasted_iota(jnp.int32, (PROJ_TM, KV_PAD), 1) & (HEAD_PAD - 1)
    hot = lane == HEAD_DIM + ((tpos >> SLC_BLOCK_LOG2) & (MASK_LANES - 1))
    ks_ref[0] = jnp.where(hot, 1.0, mm(_C_KS, KV_PAD)).astype(BF16)
    kw_ref[0] = mm(_C_KW, KV_PAD).astype(BF16)
    gt_ref[0] = jax.nn.sigmoid(_dot_nt(wvt_ref[2 * KV_PAD:3 * KV_PAD, :], xb) + bg_ref[...])
    chan = lax.broadcasted_iota(jnp.int32, (KV_PAD, PROJ_TM), 0) & (HEAD_PAD - 1)
    vst_ref[0] = jnp.where(chan == ONES_ROW, 1.0, _dot_nt(wvt_ref[0:KV_PAD, :], xb)).astype(BF16)
    vwt_ref[0] = jnp.where(chan == ONES_ROW, 1.0,
                           _dot_nt(wvt_ref[KV_PAD:2 * KV_PAD, :], xb)).astype(BF16)


def _pad_heads(w, n_heads):
    lead = w.shape[:-1]
    w = w.reshape(lead + (n_heads, HEAD_DIM))
    w = jnp.pad(w, [(0, 0)] * len(lead) + [(0, 0), (0, HEAD_PAD - HEAD_DIM)])
    return w.reshape(lead + (n_heads * HEAD_PAD,))


def _pad_gate_cols(w):
    lead = w.shape[:-1]
    per_group = HEADS_PER_GROUP * N_BRANCHES
    w = w.reshape(lead + (N_KV_GROUPS, per_group))
    w = jnp.pad(w, [(0, 0)] * len(lead) + [(0, 0), (0, HEAD_PAD - per_group)])
    return w.reshape(lead + (KV_PAD,))


def _proj(x, w_in, b_gate):
    bsz, s, _ = x.shape
    cuts = np.cumsum([POOL_WIDTH, N_Q_HEADS * HEAD_DIM] + [N_KV_GROUPS * HEAD_DIM] * 6)
    parts = jnp.split(w_in, [int(c) for c in cuts], axis=-1)
    cols = [parts[0], _pad_heads(parts[1], N_Q_HEADS)]
    cols += [_pad_heads(parts[i], N_KV_GROUPS) for i in (2, 3, 4, 6)]
    w = jnp.concatenate(cols, axis=-1).astype(BF16)
    wvt = jnp.concatenate([_pad_heads(parts[5], N_KV_GROUPS), _pad_heads(parts[7], N_KV_GROUPS),
                           _pad_gate_cols(parts[8])], axis=-1).T.astype(BF16)
    bg = _pad_gate_cols(b_gate).reshape(KV_PAD, 1)

    def tile(width):
        return pl.BlockSpec((1, PROJ_TM, width), lambda b, i: (b, i, 0))

    def out(width, dtype):
        return jax.ShapeDtypeStruct((bsz, s, width), dtype)

    vt_spec = pl.BlockSpec((1, KV_PAD, PROJ_TM), lambda b, i: (b, 0, i))
    vt_out = jax.ShapeDtypeStruct((bsz, KV_PAD, s), BF16)
    return pl.pallas_call(
        _proj_kernel,
        grid=(bsz, s // PROJ_TM),
        in_specs=[
            tile(D_MODEL),
            pl.BlockSpec((D_MODEL, _C_END), lambda b, i: (0, 0), pipeline_mode=pl.Buffered(1)),
            pl.BlockSpec((3 * KV_PAD, D_MODEL), lambda b, i: (0, 0), pipeline_mode=pl.Buffered(1)),
            pl.BlockSpec((KV_PAD, 1), lambda b, i: (0, 0)),
        ],
        out_specs=[tile(POOL_WIDTH), tile(Q_PAD), tile(KV_PAD), tile(KV_PAD), tile(KV_PAD),
                   tile(KV_PAD), vt_spec, vt_spec, vt_spec],
        out_shape=[out(POOL_WIDTH, F32), out(Q_PAD, BF16), out(KV_PAD, F32), out(KV_PAD, F32),
                   out(KV_PAD, BF16), out(KV_PAD, BF16),
                   jax.ShapeDtypeStruct((bsz, KV_PAD, s), F32), vt_out, vt_out],
        compiler_params=_params("parallel", "parallel"),
        name="in_proj",
    )(x, w, wvt, bg)


def _gelu_tanh(x):
    c = np.float32(np.sqrt(2.0 / np.pi))
    return x * (0.5 * (1.0 + jnp.tanh(c * (x + 0.044715 * (x * x * x)))))


def _compress_kernel(x_ref, plo_ref, phi_ref, wlo_ref, whi_ref, w2_ref, o_ref, a_scr, b_scr,
                     *, channel_major):
    r = pl.program_id(1)
    nsb = x_ref.shape[1]
    xr = x_ref[0]
    a_scr[r] = _dot((xr + plo_ref[...]).astype(BF16), wlo_ref[...])
    b_scr[r, 0:nsb, :] = _dot((xr + phi_ref[...]).astype(BF16), whi_ref[...])

    @pl.when(r == 0)
    def _():
        b_scr[0, nsb:nsb + 8, :] = jnp.zeros((8, KV_PAD), F32)

    @pl.when(r == CMP_PER_SLC - 1)
    def _():
        for rr in range(CMP_PER_SLC):
            if rr < CMP_PER_SLC - 1:
                h = a_scr[rr] + b_scr[rr + 1, 0:nsb, :]
            else:
                h = a_scr[rr] + b_scr[0, 1:nsb + 1, :]
            act = _gelu_tanh(h).astype(BF16)
            if channel_major:
                o_ref[0, :, rr * nsb:(rr + 1) * nsb] = _dot_nt(w2_ref[...], act).astype(BF16)
            else:
                o_ref[0, rr * nsb:(rr + 1) * nsb, :] = _dot(act, w2_ref[...]).astype(BF16)


def _compress_weights(pos, w1, w2):
    eye_g = jnp.eye(N_KV_GROUPS, dtype=F32)
    w1r = w1.reshape(CMP_BLOCK, HEAD_DIM, HEAD_DIM)
    w1r = jnp.pad(w1r, ((0, 0), (0, HEAD_PAD - HEAD_DIM), (0, HEAD_PAD - HEAD_DIM)))
    w1c = w1r[:, None, :, None, :] * eye_g[None, :, None, :, None]
    w1c = w1c.reshape(CMP_BLOCK, KV_PAD, KV_PAD)
    wlo = w1c[:CMP_STRIDE].reshape(CHUNK_FLAT, KV_PAD).astype(BF16)
    whi = w1c[CMP_STRIDE:].reshape(CHUNK_FLAT, KV_PAD).astype(BF16)
    posp = jnp.pad(pos, ((0, 0), (0, HEAD_PAD - HEAD_DIM)))
    posp = jnp.tile(posp[:, None, :], (1, N_KV_GROUPS, 1))
    plo = posp[:CMP_STRIDE].reshape(1, CHUNK_FLAT)
    phi = posp[CMP_STRIDE:].reshape(1, CHUNK_FLAT)
    w2p = jnp.pad(w2, ((0, HEAD_PAD - HEAD_DIM), (0, HEAD_PAD - HEAD_DIM)))
    w2c = (w2p[None, :, None, :] * eye_g[:, None, :, None]).reshape(KV_PAD, KV_PAD).astype(BF16)
    return plo, phi, wlo, whi, w2c


def _compress(kv, pos, w1, w2, channel_major):
    bsz, s, _ = kv.shape
    nsb = s // SLC_BLOCK
    ncp = CMP_PER_SLC * nsb
    plo, phi, wlo, whi, w2c = _compress_weights(pos, w1, w2)
    if channel_major:
        w2c = w2c.T
    out_dims = (KV_PAD, ncp) if channel_major else (ncp, KV_PAD)
    x = kv.reshape(bsz, nsb, CMP_PER_SLC * CHUNK_FLAT)
    const = lambda b, r: (0, 0)
    return pl.pallas_call(
        functools.partial(_compress_kernel, channel_major=channel_major),
        grid=(bsz, CMP_PER_SLC),
        in_specs=[
            pl.BlockSpec((1, nsb, CHUNK_FLAT), lambda b, r: (b, 0, r)),
            pl.BlockSpec((1, CHUNK_FLAT), const),
            pl.BlockSpec((1, CHUNK_FLAT), const),
            pl.BlockSpec((CHUNK_FLAT, KV_PAD), const),
            pl.BlockSpec((CHUNK_FLAT, KV_PAD), const),
            pl.BlockSpec((KV_PAD, KV_PAD), const),
        ],
        out_specs=pl.BlockSpec((1,) + out_dims, lambda b, r: (b, 0, 0)),
        out_shape=jax.ShapeDtypeStruct((bsz,) + out_dims, BF16),
        scratch_shapes=[pltpu.VMEM((CMP_PER_SLC, nsb, KV_PAD), F32),
                        pltpu.VMEM((CMP_PER_SLC, nsb + 8, KV_PAD), F32)],
        compiler_params=_params("parallel", "arbitrary"),
        name="compress",
    )(x, plo, phi, wlo, whi, w2c)


TQ = 512
SLC_TQ = 512


def _load_q(q_ref):
    return jnp.concatenate(
        [q_ref[0, :, h * HEAD_PAD:(h + 1) * HEAD_PAD] for h in range(HEADS_PER_GROUP)], axis=0)


def _per_head(row):
    return jnp.concatenate([row] * HEADS_PER_GROUP, axis=1)


def _store_gated(o_ref, g_ref, o_t, norm, branch, cols=None):
    cols = slice(None) if cols is None else cols
    gate = jnp.concatenate([g_ref[0, h * N_BRANCHES + branch:h * N_BRANCHES + branch + 1, cols]
                            for h in range(HEADS_PER_GROUP)], axis=1)
    if norm is not None:
        gate = gate / jnp.maximum(norm, 1e-30)
    o = (o_t * gate).astype(o_ref.dtype)
    tq = o.shape[1] // HEADS_PER_GROUP
    for h in range(HEADS_PER_GROUP):
        o_ref[0, h, :, cols] = o[:, h * tq:(h + 1) * tq]


def _attn_specs(tq=TQ):
    q_spec = pl.BlockSpec((1, tq, GROUP_Q), lambda b, g, i: (b, i, g))
    g_spec = pl.BlockSpec((1, HEAD_PAD, tq), lambda b, g, i: (b, g, i))
    o_spec = pl.BlockSpec((1, HEADS_PER_GROUP, HEAD_PAD, tq), lambda b, g, i: (b, g, 0, i))
    return q_spec, g_spec, o_spec


def _branch_out_shape(bsz, s):
    return jax.ShapeDtypeStruct((bsz, N_Q_HEADS, HEAD_PAD, s), BF16)


def _cmp_kernel(q_ref, kc_ref, vct_ref, g_ref, cend_ref, place_ref, o_ref, selq_ref, seld_ref):
    nsb = kc_ref.shape[1] // CMP_PER_SLC
    t0 = pl.program_id(2) * TQ
    half = nsb // 2
    if half % LANES == 0:
        @pl.when(t0 + TQ <= half * SLC_BLOCK)
        def _():
            _cmp_body(q_ref, kc_ref, vct_ref, g_ref, cend_ref, place_ref, o_ref, selq_ref, seld_ref, half)

        @pl.when(t0 + TQ > half * SLC_BLOCK)
        def _():
            _cmp_body(q_ref, kc_ref, vct_ref, g_ref, cend_ref, place_ref, o_ref, selq_ref, seld_ref, nsb)
    else:
        _cmp_body(q_ref, kc_ref, vct_ref, g_ref, cend_ref, place_ref, o_ref, selq_ref, seld_ref, nsb)


def _cmp_body(q_ref, kc_ref, vct_ref, g_ref, cend_ref, place_ref, o_ref, selq_ref, seld_ref, nvis):
    nsb_all = kc_ref.shape[1] // CMP_PER_SLC
    nsb = nvis
    t0 = pl.program_id(2) * TQ

    def slabs(ref_rows):
        if nvis == nsb_all:
            return ref_rows(0, CMP_PER_SLC * nsb_all)
        return jnp.concatenate([ref_rows(r * nsb_all, nvis) for r in range(CMP_PER_SLC)], axis=0)

    kc = slabs(lambda lo, n: kc_ref[0, lo:lo + n, :])
    cend = slabs(lambda lo, n: cend_ref[lo:lo + n, :])
    if nvis == nsb_all:
        vct = vct_ref[0]
    else:
        vct = jnp.concatenate([vct_ref[0, :, r * nsb_all:r * nsb_all + nvis]
                               for r in range(CMP_PER_SLC)], axis=1)

    s = _dot_nt(kc, _load_q(q_ref))
    s = s + _per_head(jnp.where(cend <= t0, 0.0, NEG_BIG))
    m = jnp.max(s, axis=0, keepdims=True)
    e = jnp.exp2(s - m)
    l = jnp.sum(e, axis=0, keepdims=True)
    tcol = t0 + lax.broadcasted_iota(jnp.int32, (1, TQ), 1)
    any_visible = _per_head(jnp.where(tcol >= CMP_BLOCK - 1, 1.0, 0.0))
    p = e * (any_visible / jnp.maximum(l, 1e-30))

    o_t = _dot(vct, p.astype(BF16))
    _store_gated(o_ref, g_ref, o_t, None, 0)

    imp = p[:, 0:TQ]
    for h in range(1, HEADS_PER_GROUP):
        imp = imp + p[:, h * TQ:(h + 1) * TQ]
    p0, p1, p2, p3 = (imp[r * nsb:(r + 1) * nsb, :] for r in range(CMP_PER_SLC))
    blk = lax.broadcasted_iota(jnp.int32, (nsb, TQ), 0)
    p3_prev = jnp.where(blk == 0, 0.0, pltpu.roll(p3, 1, axis=0))
    imp_slc = 0.5 * p3_prev + p0 + p1 + p2 + 0.5 * p3

    jt = (t0 + lax.broadcasted_iota(jnp.int32, (nsb, TQ), 1)) >> SLC_BLOCK_LOG2
    forced = (blk == 0) | (blk == jt) | (blk == jt - 1)
    free = (blk <= jt) & jnp.logical_not(forced)
    score = jnp.where(free, imp_slc, NEG_BIG)

    def pick(_, sc):
        mx = jnp.max(sc, axis=0, keepdims=True)
        first = jnp.min(jnp.where(sc == mx, blk, nsb), axis=0, keepdims=True)
        return jnp.where(blk == first, -jnp.inf, sc)

    picked = lax.fori_loop(0, min(N_SELECT - N_FORCED, nsb), pick, score, unroll=True) == -jnp.inf
    selected = (picked & free) | forced
    own0 = (t0 // SLC_TQ) * (SLC_TQ // SLC_BLOCK)

    def mask_lanes(keep, place):
        bias = jnp.where(keep, 0.0, NEG_BIG)
        if nvis < nsb_all:
            bias = jnp.concatenate([bias, jnp.full((nsb_all - nvis, TQ), NEG_BIG, F32)], axis=0)
        return _dot(bias.T.astype(BF16), place).astype(BF16)

    selq_ref[0, 0] = mask_lanes(selected & (blk < own0), place_ref[...])
    jrow = lax.broadcasted_iota(jnp.int32, (nsb_all, HEAD_PAD), 0)
    lane = lax.broadcasted_iota(jnp.int32, (nsb_all, HEAD_PAD), 1)
    in_super = (jrow >> (MASK_LANES.bit_length() - 1)) == (own0 // MASK_LANES)
    place_own = jnp.where(in_super & (lane == HEAD_DIM + (jrow & (MASK_LANES - 1))), 1.0, 0.0)
    seld_ref[0, 0] = mask_lanes(selected, place_own.astype(BF16))


def _mask_lane_placement(nsb):
    j = np.arange(nsb)
    place = np.zeros((nsb, (nsb // MASK_LANES) * HEAD_PAD), np.float32)
    place[j, (j // MASK_LANES) * HEAD_PAD + HEAD_DIM + j % MASK_LANES] = 1.0
    return jnp.asarray(place, BF16)


def _cmp_end_minus_token(nsb):
    row = np.arange(CMP_PER_SLC * nsb)
    end = (row % nsb) * SLC_BLOCK + (row // nsb) * CMP_STRIDE + CMP_BLOCK - 1
    return jnp.asarray(end[:, None] - np.arange(TQ)[None, :], jnp.int32)


def _cmp_attention(q, kcmp, vcmp_t, gates):
    bsz, s, _ = q.shape
    nsb = s // SLC_BLOCK
    ncp = CMP_PER_SLC * nsb
    selq_w = (nsb // MASK_LANES) * HEAD_PAD
    q_spec, g_spec, o_spec = _attn_specs()
    const = lambda b, g, i: (0, 0)
    return pl.pallas_call(
        _cmp_kernel,
        grid=(bsz, N_KV_GROUPS, s // TQ),
        in_specs=[q_spec,
                  pl.BlockSpec((1, ncp, HEAD_PAD), lambda b, g, i: (b, 0, g)),
                  pl.BlockSpec((1, HEAD_PAD, ncp), lambda b, g, i: (b, g, 0)),
                  g_spec,
                  pl.BlockSpec((ncp, TQ), const),
                  pl.BlockSpec((nsb, selq_w), const)],
        out_specs=[o_spec, pl.BlockSpec((1, 1, TQ, selq_w), lambda b, g, i: (b, g, i, 0)),
                   pl.BlockSpec((1, 1, TQ, HEAD_PAD), lambda b, g, i: (b, g, i, 0))],
        out_shape=[_branch_out_shape(bsz, s),
                   jax.ShapeDtypeStruct((bsz, N_KV_GROUPS, s, selq_w), BF16),
                   jax.ShapeDtypeStruct((bsz, N_KV_GROUPS, s, HEAD_PAD), BF16)],
        compiler_params=_params("parallel", "parallel", "arbitrary"),
        name="cmp_attn_topk",
    )(q, kcmp, vcmp_t, gates, _cmp_end_minus_token(nsb), _mask_lane_placement(nsb))


SLC_TK = 512
SUPER_KEYS = MASK_LANES * SLC_BLOCK
TILES_PER_SUPER = SUPER_KEYS // SLC_TK


def _slc_kernel(q_ref, k_ref, vt_ref, selq_ref, seld_ref, g_ref, o_ref, qm_scr, s_scr):
    n_super = selq_ref.shape[3] // HEAD_PAD
    t0 = pl.multiple_of(pl.program_id(2) * SLC_TQ, SLC_TQ)
    n_full = t0 // SUPER_KEYS
    tail_tiles = (t0 - n_full * SUPER_KEYS + SLC_TK - 1) // SLC_TK

    def masked_q(slab):
        return jnp.concatenate([q_ref[0, :, h * HEAD_PAD:(h + 1) * HEAD_PAD] + slab
                                for h in range(HEADS_PER_GROUP)], axis=0)

    for st in range(n_super):
        @pl.when(st * SUPER_KEYS < t0)
        def _():
            qm_scr[st] = masked_q(selq_ref[0, 0, :, st * HEAD_PAD:(st + 1) * HEAD_PAD])

    s = _dot_nt(k_ref[0, pl.ds(t0, SLC_TQ), :], masked_q(seld_ref[0, 0]))
    kk = lax.broadcasted_iota(jnp.int32, (SLC_TQ, SLC_TQ), 0)
    tt = lax.broadcasted_iota(jnp.int32, (SLC_TQ, SLC_TQ), 1)
    s = s + _per_head(jnp.where(kk <= tt, 0.0, NEG_BIG))
    m = jnp.max(s, axis=0, keepdims=True)
    acc = _dot(vt_ref[0, :, pl.ds(t0, SLC_TQ)], jnp.exp2(s - m).astype(BF16))

    def sweep(st, carry, n_keys):
        m, acc = carry
        k0 = pl.multiple_of(st * SUPER_KEYS, SUPER_KEYS)
        sc = _dot_nt(k_ref[0, pl.ds(k0, n_keys), :], qm_scr[st])
        s_scr[0:n_keys, :] = sc
        m_new = jnp.maximum(m, jnp.max(sc, axis=0, keepdims=True))
        acc = jnp.exp2(m - m_new) * acc
        for c in range(0, n_keys, SLC_TK):
            p = jnp.exp2(s_scr[c:c + SLC_TK, :] - m_new).astype(BF16)
            acc = acc + _dot(vt_ref[0, :, pl.ds(k0 + c, SLC_TK)], p)
        return m_new, acc

    carry = lax.fori_loop(0, n_full, functools.partial(sweep, n_keys=SUPER_KEYS), (m, acc))
    branches = [lambda c: c] + [functools.partial(sweep, n_full, n_keys=n * SLC_TK)
                                for n in range(1, TILES_PER_SUPER + 1)]
    _, acc = lax.switch(tail_tiles, branches, carry)
    _store_gated(o_ref, g_ref, acc, acc[ONES_ROW:ONES_ROW + 1, :], 1)


def _slc_attention(q, ks, vs_t, selq, seld, gates):
    bsz, s, _ = q.shape
    n_super = s // SUPER_KEYS
    q_spec, g_spec, o_spec = _attn_specs(SLC_TQ)
    return pl.pallas_call(
        _slc_kernel,
        grid=(bsz, N_KV_GROUPS, s // SLC_TQ),
        in_specs=[q_spec,
                  pl.BlockSpec((1, s, HEAD_PAD), lambda b, g, i: (b, 0, g)),
                  pl.BlockSpec((1, HEAD_PAD, s), lambda b, g, i: (b, g, 0)),
                  pl.BlockSpec((1, 1, SLC_TQ, n_super * HEAD_PAD), lambda b, g, i: (b, g, i, 0)),
                  pl.BlockSpec((1, 1, SLC_TQ, HEAD_PAD), lambda b, g, i: (b, g, i, 0)),
                  g_spec],
        out_specs=o_spec,
        out_shape=_branch_out_shape(bsz, s),
        scratch_shapes=[pltpu.VMEM((n_super, HEADS_PER_GROUP * SLC_TQ, HEAD_PAD), BF16),
                        pltpu.VMEM((SUPER_KEYS, HEADS_PER_GROUP * SLC_TQ), F32)],
        compiler_params=_params("parallel", "parallel", "arbitrary"),
        name="slc_attn",
    )(q, ks, vs_t, selq, seld, gates)


WIN_TQ = 512
WIN_SUB = 256
WIN_KEYS = WINDOW + WIN_SUB


def _win_kernel(q_ref, k_ref, vt_ref, g_ref, o_ref):
    for sub in range(WIN_TQ // WIN_SUB):
        rows = slice(sub * WIN_SUB, (sub + 1) * WIN_SUB)
        t0 = pl.program_id(2) * WIN_TQ + sub * WIN_SUB
        k0 = pl.multiple_of(jnp.maximum(t0 - WINDOW, 0), WIN_SUB)
        q = jnp.concatenate([q_ref[0, rows, h * HEAD_PAD:(h + 1) * HEAD_PAD]
                             for h in range(HEADS_PER_GROUP)], axis=0)
        s = _dot_nt(k_ref[0, pl.ds(k0, WIN_KEYS), :], q)
        diff = ((t0 + lax.broadcasted_iota(jnp.int32, (WIN_KEYS, WIN_SUB), 1))
                - (k0 + lax.broadcasted_iota(jnp.int32, (WIN_KEYS, WIN_SUB), 0)))
        s = s + _per_head(jnp.where((diff >= 0) & (diff < WINDOW), 0.0, NEG_BIG))
        m = jnp.max(s, axis=0, keepdims=True)
        e = jnp.exp2(s - m).astype(BF16)
        o_t = _dot(vt_ref[0, :, pl.ds(k0, WIN_KEYS)], e)
        _store_gated(o_ref, g_ref, o_t, o_t[ONES_ROW:ONES_ROW + 1, :], 2, cols=rows)


def _win_attention(q, kw, vw_t, gates):
    bsz, s, _ = q.shape
    q_spec, g_spec, o_spec = _attn_specs(WIN_TQ)
    return pl.pallas_call(
        _win_kernel,
        grid=(bsz, N_KV_GROUPS, s // WIN_TQ),
        in_specs=[q_spec,
                  pl.BlockSpec((1, s, HEAD_PAD), lambda b, g, i: (b, 0, g)),
                  pl.BlockSpec((1, HEAD_PAD, s), lambda b, g, i: (b, g, 0)),
                  g_spec],
        out_specs=o_spec,
        out_shape=_branch_out_shape(bsz, s),
        compiler_params=_params("parallel", "parallel", "arbitrary"),
        name="win_attn",
    )(q, kw, vw_t, gates)


MIX_TM = 1024
POOL_HALO = 16


def _mix_out_kernel(x_ref, u_ref, halo_ref, oc_ref, os_ref, ow_ref, pw_ref, ps_ref,
                    wo_ref, g_ref, b_ref, o_ref, ext_scr):
    i = pl.program_id(1)
    u = u_ref[0]
    halo = jnp.where(i == 0, 0.0, halo_ref[0])
    ext_scr[0:POOL_HALO, :] = halo
    ext_scr[POOL_HALO:POOL_HALO + MIX_TM, :] = u

    lane = lax.broadcasted_iota(jnp.int32, (MIX_TM, POOL_WIDTH), 1)
    tpos = i * MIX_TM + lax.broadcasted_iota(jnp.int32, (MIX_TM, POOL_WIDTH), 0)
    grp = lane >> (POOL_GROUP_DIM.bit_length() - 1)
    run = u
    win_sum = jnp.zeros_like(u)
    cnt = jnp.zeros_like(u)
    done = 1
    for gidx, w in enumerate(POOL_WINDOWS):
        for kback in range(done, w):
            run = run + ext_scr[POOL_HALO - kback:POOL_HALO - kback + MIX_TM, :]
        done = w
        win_sum = jnp.where(grp == gidx, run, win_sum)
        cnt = jnp.where(grp == gidx, jnp.minimum(tpos + 1, w).astype(F32), cnt)
    pooled = win_sum / cnt - u
    mixed = _dot(pooled.astype(BF16), pw_ref[...]) * ps_ref[...]

    y = _dot(mixed.astype(BF16), wo_ref[0:POOL_WIDTH, :])
    for h in range(0, N_Q_HEADS, 2):
        pair = [(oc_ref[0, hh].astype(F32) + os_ref[0, hh].astype(F32)
                 + ow_ref[0, hh].astype(F32)).T.astype(BF16) for hh in (h, h + 1)]
        lo = POOL_WIDTH + h * HEAD_PAD
        y = y + _dot(jnp.concatenate(pair, axis=1), wo_ref[lo:lo + 2 * HEAD_PAD, :])
    z = ALPHA * x_ref[0] + y
    o_ref[0] = _layer_norm(z, g_ref[...], b_ref[...])


def _mix_out(x, u, o_cmp, o_slc, o_win, pool_w, pool_scale, w_out, g, b):
    bsz, s, _ = x.shape
    n_grp = len(POOL_WINDOWS)
    eye = jnp.eye(n_grp, dtype=F32)
    pw = (pool_w[:, :, None, :] * eye[:, None, :, None]).reshape(POOL_WIDTH, POOL_WIDTH).astype(BF16)
    wo_nsa = w_out[POOL_WIDTH:].reshape(N_Q_HEADS, HEAD_DIM, D_MODEL)
    wo_nsa = jnp.pad(wo_nsa, ((0, 0), (0, HEAD_PAD - HEAD_DIM), (0, 0))).reshape(Q_PAD, D_MODEL)
    wo = jnp.concatenate([w_out[:POOL_WIDTH], wo_nsa], axis=0).astype(BF16)

    def tile(width):
        return pl.BlockSpec((1, MIX_TM, width), lambda bb, i: (bb, i, 0))

    branch = pl.BlockSpec((1, N_Q_HEADS, HEAD_PAD, MIX_TM), lambda bb, i: (bb, 0, 0, i))
    halo_blocks = MIX_TM // POOL_HALO
    const = lambda bb, i: (0, 0)
    return pl.pallas_call(
        _mix_out_kernel,
        grid=(bsz, s // MIX_TM),
        in_specs=[
            tile(D_MODEL), tile(POOL_WIDTH),
            pl.BlockSpec((1, POOL_HALO, POOL_WIDTH),
                         lambda bb, i: (bb, jnp.maximum(i * halo_blocks - 1, 0), 0)),
            branch, branch, branch,
            pl.BlockSpec((POOL_WIDTH, POOL_WIDTH), const),
            pl.BlockSpec((1, POOL_WIDTH), const),
            pl.BlockSpec((POOL_WIDTH + Q_PAD, D_MODEL), const),
            pl.BlockSpec((1, D_MODEL), const),
            pl.BlockSpec((1, D_MODEL), const),
        ],
        out_specs=tile(D_MODEL),
        out_shape=jax.ShapeDtypeStruct((bsz, s, D_MODEL), F32),
        scratch_shapes=[pltpu.VMEM((POOL_HALO + MIX_TM, POOL_WIDTH), F32)],
        compiler_params=_params("parallel", "arbitrary"),
        name="mix_out_ln",
    )(x, u, u, o_cmp, o_slc, o_win, pw, pool_scale.reshape(1, -1), wo,
      g.reshape(1, -1), b.reshape(1, -1))


def kernel(x, ln1_g, ln1_b, ffn1_w_gate, ffn1_w_up, ffn1_w_down, w_in, b_gate, pool_w, pool_scale, cmp_pos_k, cmp_k_w1, cmp_k_w2, cmp_pos_v, cmp_v_w1, cmp_v_w2, w_out, ln2_g, ln2_b, ffn2_w_gate, ffn2_w_up, ffn2_w_down, ln3_g, ln3_b):
    bsz, s, d = x.shape
    assert d == D_MODEL and s % max(SUPER_KEYS, FFN_TM, MIX_TM, PROJ_TM) == 0
    for l in range(DEPTH):
        x = _ffn_ln(x.reshape(bsz * s, d), ffn1_w_gate[l], ffn1_w_up[l], ffn1_w_down[l],
                    ln1_g[l], ln1_b[l]).reshape(bsz, s, d)
        u, q, kc, vc, ks, kw, gates, vs_t, vw_t = _proj(x, w_in[l], b_gate[l])
        kcmp = _compress(kc, cmp_pos_k[l], cmp_k_w1[l], cmp_k_w2[l], channel_major=False)
        vcmp_t = _compress(vc, cmp_pos_v[l], cmp_v_w1[l], cmp_v_w2[l], channel_major=True)
        o_cmp, selq, seld = _cmp_attention(q, kcmp, vcmp_t, gates)
        o_slc = _slc_attention(q, ks, vs_t, selq, seld, gates)
        o_win = _win_attention(q, kw, vw_t, gates)
        x = _mix_out(x, u, o_cmp, o_slc, o_win, pool_w[l], pool_scale[l], w_out[l],
                     ln2_g[l], ln2_b[l])
        x = _ffn_ln(x.reshape(bsz * s, d), ffn2_w_gate[l], ffn2_w_up[l], ffn2_w_down[l],
                    ln3_g[l], ln3_b[l]).reshape(bsz, s, d)
    return x
```

```python
import functools

import numpy as np
import jax
import jax.numpy as jnp
from jax import lax
from jax.experimental import pallas as pl
from jax.experimental.pallas import tpu as pltpu

D_MODEL = 1024
DEPTH = 2
POOL_WIDTH = 256
POOL_WINDOWS = (2, 4, 8, 16)
POOL_GROUP_DIM = 64
N_Q_HEADS = 8
HEAD_DIM = 96
N_KV_GROUPS = 2
HEADS_PER_GROUP = 4
N_BRANCHES = 3
CMP_STRIDE = 16
CMP_BLOCK = 32
SLC_BLOCK = 64
SLC_BLOCK_LOG2 = 6
N_SELECT = 16
N_FORCED = 3
WINDOW = 512
D_FF = 2816
ALPHA = (2.0 * DEPTH) ** 0.25
LN_EPS = 1e-5
NEG_BIG = -1e30
SEL_BIG = 1e30
QK_SCALE = HEAD_DIM ** -0.5
LOG2_E = 1.4426950408889634
ONES_ROW = HEAD_DIM

LANES = 128
HEAD_PAD = LANES
Q_PAD = N_Q_HEADS * HEAD_PAD
KV_PAD = N_KV_GROUPS * HEAD_PAD
GROUP_Q = HEADS_PER_GROUP * HEAD_PAD
CMP_PER_SLC = SLC_BLOCK // CMP_STRIDE
CHUNK_FLAT = CMP_STRIDE * KV_PAD

VMEM_LIMIT = 56 * 1024 * 1024

F32 = jnp.float32
BF16 = jnp.bfloat16

_C_U = 0
_C_Q = _C_U + POOL_WIDTH
_C_KC = _C_Q + Q_PAD
_C_VC = _C_KC + KV_PAD
_C_KS = _C_VC + KV_PAD
_C_KW = _C_KS + KV_PAD
_C_END = _C_KW + KV_PAD


def _params(*sem):
    return pltpu.CompilerParams(dimension_semantics=sem, vmem_limit_bytes=VMEM_LIMIT)


def _layer_norm(z, g, b):
    mu = jnp.mean(z, axis=-1, keepdims=True)
    zc = z - mu
    var = jnp.mean(zc * zc, axis=-1, keepdims=True)
    return zc * lax.rsqrt(var + LN_EPS) * g + b


def _dot(a, b):
    return jnp.dot(a, b, preferred_element_type=F32)


def _dot_nt(a, b):
    return lax.dot_general(a, b, (((1,), (1,)), ((), ())), preferred_element_type=F32)


FFN_TM = 1024
MXU_TILE = 256
FFN_CHUNKS = (6 * MXU_TILE, 5 * MXU_TILE)
assert sum(FFN_CHUNKS) == D_FF


def _ffn_ln_kernel(x_ref, wg_ref, wu_ref, wd_ref, g_ref, b_ref, o_ref):
    x = x_ref[...]
    xb = x.astype(BF16)
    acc = None
    lo = 0
    for width in FFN_CHUNKS:
        hg = _dot(xb, wg_ref[:, lo:lo + width])
        hu = _dot(xb, wu_ref[:, lo:lo + width])
        h = (hg * jax.nn.sigmoid(hg)) * hu
        part = _dot(h.astype(BF16), wd_ref[lo:lo + width, :])
        acc = part if acc is None else acc + part
        lo += width
    z = ALPHA * x + 0.5 * acc
    o_ref[...] = _layer_norm(z, g_ref[...], b_ref[...])


def _ffn_ln(x2d, wg, wu, wd, g, b):
    t = x2d.shape[0]
    const = lambda i: (0, 0)
    return pl.pallas_call(
        _ffn_ln_kernel,
        grid=(t // FFN_TM,),
        in_specs=[
            pl.BlockSpec((FFN_TM, D_MODEL), lambda i: (i, 0)),
            pl.BlockSpec((D_MODEL, D_FF), const, pipeline_mode=pl.Buffered(1)),
            pl.BlockSpec((D_MODEL, D_FF), const, pipeline_mode=pl.Buffered(1)),
            pl.BlockSpec((D_FF, D_MODEL), const, pipeline_mode=pl.Buffered(1)),
            pl.BlockSpec((1, D_MODEL), const),
            pl.BlockSpec((1, D_MODEL), const),
        ],
        out_specs=pl.BlockSpec((FFN_TM, D_MODEL), lambda i: (i, 0)),
        out_shape=jax.ShapeDtypeStruct((t, D_MODEL), F32),
        compiler_params=_params("parallel"),
        name="ffn_ln",
    )(x2d, wg.astype(BF16), wu.astype(BF16), wd.astype(BF16), g.reshape(1, -1), b.reshape(1, -1))


PROJ_TM = 1024
MASK_LANES = HEAD_PAD - HEAD_DIM


def _proj_kernel(x_ref, w_ref, wvt_ref, bg_ref, u_ref, q_ref, kc_ref, vc_ref, ks_ref, kw_ref,
                 gt_ref, vst_ref, vwt_ref):
    xb = x_ref[0].astype(BF16)

    def mm(lo, n):
        return _dot(xb, w_ref[:, lo:lo + n])

    u_ref[0] = mm(_C_U, POOL_WIDTH)
    q_ref[0] = (mm(_C_Q, Q_PAD) * (QK_SCALE * LOG2_E)).astype(BF16)
    kc_ref[0] = mm(_C_KC, KV_PAD)
    vc_ref[0] = mm(_C_VC, KV_PAD)
    tpos = pl.program_id(1) * PROJ_TM + lax.broadcasted_iota(jnp.int32, (PROJ_TM, KV_PAD), 0)
    lane = lax.broadcasted_iota(jnp.int32, (PROJ_TM, KV_PAD), 1) & (HEAD_PAD - 1)
    hot = lane == HEAD_DIM + ((tpos >> SLC_BLOCK_LOG2) & (MASK_LANES - 1))
    ks_ref[0] = jnp.where(hot, 1.0, mm(_C_KS, KV_PAD)).astype(BF16)
    kw_ref[0] = mm(_C_KW, KV_PAD).astype(BF16)
    gt_ref[0] = jax.nn.sigmoid(_dot_nt(wvt_ref[2 * KV_PAD:3 * KV_PAD, :], xb) + bg_ref[...])
    chan = lax.broadcasted_iota(jnp.int32, (KV_PAD, PROJ_TM), 0) & (HEAD_PAD - 1)
    vst_ref[0] = jnp.where(chan == ONES_ROW, 1.0, _dot_nt(wvt_ref[0:KV_PAD, :], xb)).astype(BF16)
    vwt_ref[0] = jnp.where(chan == ONES_ROW, 1.0,
                           _dot_nt(wvt_ref[KV_PAD:2 * KV_PAD, :], xb)).astype(BF16)


def _pad_heads(w, n_heads):
    lead = w.shape[:-1]
    w = w.reshape(lead + (n_heads, HEAD_DIM))
    w = jnp.pad(w, [(0, 0)] * len(lead) + [(0, 0), (0, HEAD_PAD - HEAD_DIM)])
    return w.reshape(lead + (n_heads * HEAD_PAD,))


def _pad_gate_cols(w):
    lead = w.shape[:-1]
    per_group = HEADS_PER_GROUP * N_BRANCHES
    w = w.reshape(lead + (N_KV_GROUPS, per_group))
    w = jnp.pad(w, [(0, 0)] * len(lead) + [(0, 0), (0, HEAD_PAD - per_group)])
    return w.reshape(lead + (KV_PAD,))


def _proj(x, w_in, b_gate):
    bsz, s, _ = x.shape
    cuts = np.cumsum([POOL_WIDTH, N_Q_HEADS * HEAD_DIM] + [N_KV_GROUPS * HEAD_DIM] * 6)
    parts = jnp.split(w_in, [int(c) for c in cuts], axis=-1)
    cols = [parts[0], _pad_heads(parts[1], N_Q_HEADS)]
    cols += [_pad_heads(parts[i], N_KV_GROUPS) for i in (2, 3, 4, 6)]
    w = jnp.concatenate(cols, axis=-1).astype(BF16)
    wvt = jnp.concatenate([_pad_heads(parts[5], N_KV_GROUPS), _pad_heads(parts[7], N_KV_GROUPS),
                           _pad_gate_cols(parts[8])], axis=-1).T.astype(BF16)
    bg = _pad_gate_cols(b_gate).reshape(KV_PAD, 1)

    def tile(width):
        return pl.BlockSpec((1, PROJ_TM, width), lambda b, i: (b, i, 0))

    def out(width, dtype):
        return jax.ShapeDtypeStruct((bsz, s, width), dtype)

    vt_spec = pl.BlockSpec((1, KV_PAD, PROJ_TM), lambda b, i: (b, 0, i))
    vt_out = jax.ShapeDtypeStruct((bsz, KV_PAD, s), BF16)
    return pl.pallas_call(
        _proj_kernel,
        grid=(bsz, s // PROJ_TM),
        in_specs=[
            tile(D_MODEL),
            pl.BlockSpec((D_MODEL, _C_END), lambda b, i: (0, 0), pipeline_mode=pl.Buffered(1)),
            pl.BlockSpec((3 * KV_PAD, D_MODEL), lambda b, i: (0, 0), pipeline_mode=pl.Buffered(1)),
            pl.BlockSpec((KV_PAD, 1), lambda b, i: (0, 0)),
        ],
        out_specs=[tile(POOL_WIDTH), tile(Q_PAD), tile(KV_PAD), tile(KV_PAD), tile(KV_PAD),
                   tile(KV_PAD), vt_spec, vt_spec, vt_spec],
        out_shape=[out(POOL_WIDTH, F32), out(Q_PAD, BF16), out(KV_PAD, F32), out(KV_PAD, F32),
                   out(KV_PAD, BF16), out(KV_PAD, BF16),
                   jax.ShapeDtypeStruct((bsz, KV_PAD, s), F32), vt_out, vt_out],
        compiler_params=_params("parallel", "parallel"),
        name="in_proj",
    )(x, w, wvt, bg)


def _gelu_tanh(x):
    c = np.float32(np.sqrt(2.0 / np.pi))
    return x * (0.5 * (1.0 + jnp.tanh(c * (x + 0.044715 * (x * x * x)))))


def _compress_kernel(x_ref, plo_ref, phi_ref, wlo_ref, whi_ref, w2_ref, o_ref, a_scr, b_scr,
                     *, channel_major):
    r = pl.program_id(1)
    nsb = x_ref.shape[1]
    xr = x_ref[0]
    a_scr[r] = _dot((xr + plo_ref[...]).astype(BF16), wlo_ref[...])
    b_scr[r, 0:nsb, :] = _dot((xr + phi_ref[...]).astype(BF16), whi_ref[...])

    @pl.when(r == 0)
    def _():
        b_scr[0, nsb:nsb + 8, :] = jnp.zeros((8, KV_PAD), F32)

    @pl.when(r == CMP_PER_SLC - 1)
    def _():
        for rr in range(CMP_PER_SLC):
            if rr < CMP_PER_SLC - 1:
                h = a_scr[rr] + b_scr[rr + 1, 0:nsb, :]
            else:
                h = a_scr[rr] + b_scr[0, 1:nsb + 1, :]
            act = _gelu_tanh(h).astype(BF16)
            if channel_major:
                o_ref[0, :, rr * nsb:(rr + 1) * nsb] = _dot_nt(w2_ref[...], act).astype(BF16)
            else:
                o_ref[0, rr * nsb:(rr + 1) * nsb, :] = _dot(act, w2_ref[...]).astype(BF16)


def _compress_weights(pos, w1, w2):
    eye_g = jnp.eye(N_KV_GROUPS, dtype=F32)
    w1r = w1.reshape(CMP_BLOCK, HEAD_DIM, HEAD_DIM)
    w1r = jnp.pad(w1r, ((0, 0), (0, HEAD_PAD - HEAD_DIM), (0, HEAD_PAD - HEAD_DIM)))
    w1c = w1r[:, None, :, None, :] * eye_g[None, :, None, :, None]
    w1c = w1c.reshape(CMP_BLOCK, KV_PAD, KV_PAD)
    wlo = w1c[:CMP_STRIDE].reshape(CHUNK_FLAT, KV_PAD).astype(BF16)
    whi = w1c[CMP_STRIDE:].reshape(CHUNK_FLAT, KV_PAD).astype(BF16)
    posp = jnp.pad(pos, ((0, 0), (0, HEAD_PAD - HEAD_DIM)))
    posp = jnp.tile(posp[:, None, :], (1, N_KV_GROUPS, 1))
    plo = posp[:CMP_STRIDE].reshape(1, CHUNK_FLAT)
    phi = posp[CMP_STRIDE:].reshape(1, CHUNK_FLAT)
    w2p = jnp.pad(w2, ((0, HEAD_PAD - HEAD_DIM), (0, HEAD_PAD - HEAD_DIM)))
    w2c = (w2p[None, :, None, :] * eye_g[:, None, :, None]).reshape(KV_PAD, KV_PAD).astype(BF16)
    return plo, phi, wlo, whi, w2c


def _compress(kv, pos, w1, w2, channel_major):
    bsz, s, _ = kv.shape
    nsb = s // SLC_BLOCK
    ncp = CMP_PER_SLC * nsb
    plo, phi, wlo, whi, w2c = _compress_weights(pos, w1, w2)
    if channel_major:
        w2c = w2c.T
    out_dims = (KV_PAD, ncp) if channel_major else (ncp, KV_PAD)
    x = kv.reshape(bsz, nsb, CMP_PER_SLC * CHUNK_FLAT)
    const = lambda b, r: (0, 0)
    return pl.pallas_call(
        functools.partial(_compress_kernel, channel_major=channel_major),
        grid=(bsz, CMP_PER_SLC),
        in_specs=[
            pl.BlockSpec((1, nsb, CHUNK_FLAT), lambda b, r: (b, 0, r)),
            pl.BlockSpec((1, CHUNK_FLAT), const),
            pl.BlockSpec((1, CHUNK_FLAT), const),
            pl.BlockSpec((CHUNK_FLAT, KV_PAD), const),
            pl.BlockSpec((CHUNK_FLAT, KV_PAD), const),
            pl.BlockSpec((KV_PAD, KV_PAD), const),
        ],
        out_specs=pl.BlockSpec((1,) + out_dims, lambda b, r: (b, 0, 0)),
        out_shape=jax.ShapeDtypeStruct((bsz,) + out_dims, BF16),
        scratch_shapes=[pltpu.VMEM((CMP_PER_SLC, nsb, KV_PAD), F32),
                        pltpu.VMEM((CMP_PER_SLC, nsb + 8, KV_PAD), F32)],
        compiler_params=_params("parallel", "arbitrary"),
        name="compress",
    )(x, plo, phi, wlo, whi, w2c)


TQ = 512
SLC_TQ = 512


def _load_q(q_ref):
    return jnp.concatenate(
        [q_ref[0, :, h * HEAD_PAD:(h + 1) * HEAD_PAD] for h in range(HEADS_PER_GROUP)], axis=0)


def _per_head(row):
    return jnp.concatenate([row] * HEADS_PER_GROUP, axis=1)


def _store_gated(o_ref, g_ref, o_t, norm, branch, cols=None):
    cols = slice(None) if cols is None else cols
    gate = jnp.concatenate([g_ref[0, h * N_BRANCHES + branch:h * N_BRANCHES + branch + 1, cols]
                            for h in range(HEADS_PER_GROUP)], axis=1)
    if norm is not None:
        gate = gate / jnp.maximum(norm, 1e-30)
    o = (o_t * gate).astype(o_ref.dtype)
    tq = o.shape[1] // HEADS_PER_GROUP
    for h in range(HEADS_PER_GROUP):
        o_ref[0, h, :, cols] = o[:, h * tq:(h + 1) * tq]


def _attn_specs(tq=TQ):
    q_spec = pl.BlockSpec((1, tq, GROUP_Q), lambda b, g, i: (b, i, g))
    g_spec = pl.BlockSpec((1, HEAD_PAD, tq), lambda b, g, i: (b, g, i))
    o_spec = pl.BlockSpec((1, HEADS_PER_GROUP, HEAD_PAD, tq), lambda b, g, i: (b, g, 0, i))
    return q_spec, g_spec, o_spec


def _branch_out_shape(bsz, s):
    return jax.ShapeDtypeStruct((bsz, N_Q_HEADS, HEAD_PAD, s), BF16)


def _cmp_kernel(q_ref, kc_ref, vct_ref, g_ref, cend_ref, place_ref, o_ref, selq_ref, seld_ref):
    nsb = kc_ref.shape[1] // CMP_PER_SLC
    t0 = pl.program_id(2) * TQ
    half = nsb // 2
    if half % LANES == 0:
        @pl.when(t0 + TQ <= half * SLC_BLOCK)
        def _():
            _cmp_body(q_ref, kc_ref, vct_ref, g_ref, cend_ref, place_ref, o_ref, selq_ref, seld_ref, half)

        @pl.when(t0 + TQ > half * SLC_BLOCK)
        def _():
            _cmp_body(q_ref, kc_ref, vct_ref, g_ref, cend_ref, place_ref, o_ref, selq_ref, seld_ref, nsb)
    else:
        _cmp_body(q_ref, kc_ref, vct_ref, g_ref, cend_ref, place_ref, o_ref, selq_ref, seld_ref, nsb)


def _cmp_body(q_ref, kc_ref, vct_ref, g_ref, cend_ref, place_ref, o_ref, selq_ref, seld_ref, nvis):
    nsb_all = kc_ref.shape[1] // CMP_PER_SLC
    nsb = nvis
    t0 = pl.program_id(2) * TQ

    def slabs(ref_rows):
        if nvis == nsb_all:
            return ref_rows(0, CMP_PER_SLC * nsb_all)
        return jnp.concatenate([ref_rows(r * nsb_all, nvis) for r in range(CMP_PER_SLC)], axis=0)

    kc = slabs(lambda lo, n: kc_ref[0, lo:lo + n, :])
    cend = slabs(lambda lo, n: cend_ref[lo:lo + n, :])
    if nvis == nsb_all:
        vct = vct_ref[0]
    else:
        vct = jnp.concatenate([vct_ref[0, :, r * nsb_all:r * nsb_all + nvis]
                               for r in range(CMP_PER_SLC)], axis=1)

    s = _dot_nt(kc, _load_q(q_ref))
    s = s + _per_head(jnp.where(cend <= t0, 0.0, NEG_BIG))
    m = jnp.max(s, axis=0, keepdims=True)
    e = jnp.exp2(s - m)
    l = jnp.sum(e, axis=0, keepdims=True)
    tcol = t0 + lax.broadcasted_iota(jnp.int32, (1, TQ), 1)
    any_visible = _per_head(jnp.where(tcol >= CMP_BLOCK - 1, 1.0, 0.0))
    p = e * (any_visible / jnp.maximum(l, 1e-30))

    o_t = _dot(vct, p.astype(BF16))
    _store_gated(o_ref, g_ref, o_t, None, 0)

    imp = p[:, 0:TQ]
    for h in range(1, HEADS_PER_GROUP):
        imp = imp + p[:, h * TQ:(h + 1) * TQ]
    p0, p1, p2, p3 = (imp[r * nsb:(r + 1) * nsb, :] for r in range(CMP_PER_SLC))
    blk = lax.broadcasted_iota(jnp.int32, (nsb, TQ), 0)
    p3_prev = jnp.where(blk == 0, 0.0, pltpu.roll(p3, 1, axis=0))
    imp_slc = 0.5 * p3_prev + p0 + p1 + p2 + 0.5 * p3

    jt = (t0 + lax.broadcasted_iota(jnp.int32, (nsb, TQ), 1)) >> SLC_BLOCK_LOG2
    forced = (blk == 0) | (blk == jt) | (blk == jt - 1)
    free = (blk <= jt) & jnp.logical_not(forced)
    score = jnp.where(free, imp_slc, NEG_BIG)

    def pick(_, sc):
        mx = jnp.max(sc, axis=0, keepdims=True)
        first = jnp.min(jnp.where(sc == mx, blk, nsb), axis=0, keepdims=True)
        return jnp.where(blk == first, -jnp.inf, sc)

    picked = lax.fori_loop(0, min(N_SELECT - N_FORCED, nsb), pick, score, unroll=True) == -jnp.inf
    selected = (picked & free) | forced
    own0 = (t0 // SLC_TQ) * (SLC_TQ // SLC_BLOCK)

    def mask_lanes(keep, place):
        bias = jnp.where(keep, 0.0, NEG_BIG)
        if nvis < nsb_all:
            bias = jnp.concatenate([bias, jnp.full((nsb_all - nvis, TQ), NEG_BIG, F32)], axis=0)
        return _dot(bias.T.astype(BF16), place).astype(BF16)

    selq_ref[0, 0] = mask_lanes(selected & (blk < own0), place_ref[...])
    jrow = lax.broadcasted_iota(jnp.int32, (nsb_all, HEAD_PAD), 0)
    lane = lax.broadcasted_iota(jnp.int32, (nsb_all, HEAD_PAD), 1)
    in_super = (jrow >> (MASK_LANES.bit_length() - 1)) == (own0 // MASK_LANES)
    place_own = jnp.where(in_super & (lane == HEAD_DIM + (jrow & (MASK_LANES - 1))), 1.0, 0.0)
    seld_ref[0, 0] = mask_lanes(selected, place_own.astype(BF16))


def _mask_lane_placement(nsb):
    j = np.arange(nsb)
    place = np.zeros((nsb, (nsb // MASK_LANES) * HEAD_PAD), np.float32)
    place[j, (j // MASK_LANES) * HEAD_PAD + HEAD_DIM + j % MASK_LANES] = 1.0
    return jnp.asarray(place, BF16)


def _cmp_end_minus_token(nsb):
    row = np.arange(CMP_PER_SLC * nsb)
    end = (row % nsb) * SLC_BLOCK + (row // nsb) * CMP_STRIDE + CMP_BLOCK - 1
    return jnp.asarray(end[:, None] - np.arange(TQ)[None, :], jnp.int32)


def _cmp_attention(q, kcmp, vcmp_t, gates):
    bsz, s, _ = q.shape
    nsb = s // SLC_BLOCK
    ncp = CMP_PER_SLC * nsb
    selq_w = (nsb // MASK_LANES) * HEAD_PAD
    q_spec, g_spec, o_spec = _attn_specs()
    const = lambda b, g, i: (0, 0)
    return pl.pallas_call(
        _cmp_kernel,
        grid=(bsz, N_KV_GROUPS, s // TQ),
        in_specs=[q_spec,
                  pl.BlockSpec((1, ncp, HEAD_PAD), lambda b, g, i: (b, 0, g)),
                  pl.BlockSpec((1, HEAD_PAD, ncp), lambda b, g, i: (b, g, 0)),
                  g_spec,
                  pl.BlockSpec((ncp, TQ), const),
                  pl.BlockSpec((nsb, selq_w), const)],
        out_specs=[o_spec, pl.BlockSpec((1, 1, TQ, selq_w), lambda b, g, i: (b, g, i, 0)),
                   pl.BlockSpec((1, 1, TQ, HEAD_PAD), lambda b, g, i: (b, g, i, 0))],
        out_shape=[_branch_out_shape(bsz, s),
                   jax.ShapeDtypeStruct((bsz, N_KV_GROUPS, s, selq_w), BF16),
                   jax.ShapeDtypeStruct((bsz, N_KV_GROUPS, s, HEAD_PAD), BF16)],
        compiler_params=_params("parallel", "parallel", "arbitrary"),
        name="cmp_attn_topk",
    )(q, kcmp, vcmp_t, gates, _cmp_end_minus_token(nsb), _mask_lane_placement(nsb))


SLC_TK = 512
SUPER_KEYS = MASK_LANES * SLC_BLOCK
TILES_PER_SUPER = SUPER_KEYS // SLC_TK
assert SLC_TQ == SLC_TK


def _slc_kernel(q_ref, k_ref, vt_ref, selq_ref, seld_ref, g_ref, o_ref, s_scr):
    cols = HEADS_PER_GROUP * SLC_TQ
    t0 = pl.multiple_of(pl.program_id(2) * SLC_TQ, SLC_TQ)
    n_full = t0 // SUPER_KEYS
    tail_tiles = (t0 - n_full * SUPER_KEYS) // SLC_TK

    def masked_q(slab):
        return jnp.concatenate([q_ref[0, :, h * HEAD_PAD:(h + 1) * HEAD_PAD] + slab
                                for h in range(HEADS_PER_GROUP)], axis=0)

    def accumulate(k0, n_keys, m, m_new, acc):
        acc = jnp.exp2(m - m_new) * acc
        for c in range(0, n_keys, SLC_TK):
            p = jnp.exp2(s_scr[c:c + SLC_TK, :] - m_new).astype(BF16)
            acc = acc + _dot(vt_ref[0, :, pl.ds(k0 + c, SLC_TK)], p)
        return m_new, acc

    def sweep(st, carry):
        m, acc = carry
        k0 = pl.multiple_of(st * SUPER_KEYS, SUPER_KEYS)
        slab = selq_ref[0, 0, :, pl.ds(pl.multiple_of(st * HEAD_PAD, HEAD_PAD), HEAD_PAD)]
        sc = _dot_nt(k_ref[0, pl.ds(k0, SUPER_KEYS), :], masked_q(slab))
        s_scr[...] = sc
        m_new = jnp.maximum(m, jnp.max(sc, axis=0, keepdims=True))
        return accumulate(k0, SUPER_KEYS, m, m_new, acc)

    def last_step(n_before, carry):
        m, acc = carry
        k0 = pl.multiple_of(n_full * SUPER_KEYS, SUPER_KEYS)
        n_keys = n_before + SLC_TQ
        sc = _dot_nt(k_ref[0, pl.ds(k0, n_keys), :], masked_q(seld_ref[0, 0]))
        kk = lax.broadcasted_iota(jnp.int32, (SLC_TQ, SLC_TQ), 0)
        tt = lax.broadcasted_iota(jnp.int32, (SLC_TQ, SLC_TQ), 1)
        own = sc[n_before:, :] + _per_head(jnp.where(kk <= tt, 0.0, NEG_BIG))
        s_scr[n_before:n_keys, :] = own
        m_new = jnp.maximum(m, jnp.max(own, axis=0, keepdims=True))
        if n_before:
            s_scr[0:n_before, :] = sc[:n_before, :]
            m_new = jnp.maximum(m_new, jnp.max(sc[:n_before, :], axis=0, keepdims=True))
        return accumulate(k0, n_keys, m, m_new, acc)

    carry = (jnp.full((1, cols), NEG_BIG, F32), jnp.zeros((HEAD_PAD, cols), F32))
    carry = lax.fori_loop(0, n_full, sweep, carry)
    _, acc = lax.switch(tail_tiles, [functools.partial(last_step, n * SLC_TK)
                                     for n in range(TILES_PER_SUPER)], carry)
    _store_gated(o_ref, g_ref, acc, acc[ONES_ROW:ONES_ROW + 1, :], 1)


def _slc_attention(q, ks, vs_t, selq, seld, gates):
    bsz, s, _ = q.shape
    n_super = s // SUPER_KEYS
    q_spec, g_spec, o_spec = _attn_specs(SLC_TQ)
    return pl.pallas_call(
        _slc_kernel,
        grid=(bsz, N_KV_GROUPS, s // SLC_TQ),
        in_specs=[q_spec,
                  pl.BlockSpec((1, s, HEAD_PAD), lambda b, g, i: (b, 0, g)),
                  pl.BlockSpec((1, HEAD_PAD, s), lambda b, g, i: (b, g, 0)),
                  pl.BlockSpec((1, 1, SLC_TQ, n_super * HEAD_PAD), lambda b, g, i: (b, g, i, 0)),
                  pl.BlockSpec((1, 1, SLC_TQ, HEAD_PAD), lambda b, g, i: (b, g, i, 0)),
                  g_spec],
        out_specs=o_spec,
        out_shape=_branch_out_shape(bsz, s),
        scratch_shapes=[pltpu.VMEM((SUPER_KEYS, HEADS_PER_GROUP * SLC_TQ), F32)],
        compiler_params=_params("parallel", "parallel", "arbitrary"),
        name="slc_attn",
    )(q, ks, vs_t, selq, seld, gates)


WIN_TQ = 512
WIN_SUB = 256
WIN_KEYS = WINDOW + WIN_SUB


def _win_kernel(q_ref, k_ref, vt_ref, g_ref, o_ref):
    for sub in range(WIN_TQ // WIN_SUB):
        rows = slice(sub * WIN_SUB, (sub + 1) * WIN_SUB)
        t0 = pl.program_id(2) * WIN_TQ + sub * WIN_SUB
        k0 = pl.multiple_of(jnp.maximum(t0 - WINDOW, 0), WIN_SUB)
        q = jnp.concatenate([q_ref[0, rows, h * HEAD_PAD:(h + 1) * HEAD_PAD]
                             for h in range(HEADS_PER_GROUP)], axis=0)
        s = _dot_nt(k_ref[0, pl.ds(k0, WIN_KEYS), :], q)
        diff = ((t0 + lax.broadcasted_iota(jnp.int32, (WIN_KEYS, WIN_SUB), 1))
                - (k0 + lax.broadcasted_iota(jnp.int32, (WIN_KEYS, WIN_SUB), 0)))
        s = s + _per_head(jnp.where((diff >= 0) & (diff < WINDOW), 0.0, NEG_BIG))
        m = jnp.max(s, axis=0, keepdims=True)
        e = jnp.exp2(s - m).astype(BF16)
        o_t = _dot(vt_ref[0, :, pl.ds(k0, WIN_KEYS)], e)
        _store_gated(o_ref, g_ref, o_t, o_t[ONES_ROW:ONES_ROW + 1, :], 2, cols=rows)


def _win_attention(q, kw, vw_t, gates):
    bsz, s, _ = q.shape
    q_spec, g_spec, o_spec = _attn_specs(WIN_TQ)
    return pl.pallas_call(
        _win_kernel,
        grid=(bsz, N_KV_GROUPS, s // WIN_TQ),
        in_specs=[q_spec,
                  pl.BlockSpec((1, s, HEAD_PAD), lambda b, g, i: (b, 0, g)),
                  pl.BlockSpec((1, HEAD_PAD, s), lambda b, g, i: (b, g, 0)),
                  g_spec],
        out_specs=o_spec,
        out_shape=_branch_out_shape(bsz, s),
        compiler_params=_params("parallel", "parallel", "arbitrary"),
        name="win_attn",
    )(q, kw, vw_t, gates)


MIX_TM = 1024
POOL_HALO = 16


def _mix_out_kernel(x_ref, u_ref, halo_ref, oc_ref, os_ref, ow_ref, pw_ref, ps_ref,
                    wo_ref, g_ref, b_ref, o_ref, ext_scr):
    i = pl.program_id(1)
    u = u_ref[0]
    halo = jnp.where(i == 0, 0.0, halo_ref[0])
    ext_scr[0:POOL_HALO, :] = halo
    ext_scr[POOL_HALO:POOL_HALO + MIX_TM, :] = u

    lane = lax.broadcasted_iota(jnp.int32, (MIX_TM, POOL_WIDTH), 1)
    tpos = i * MIX_TM + lax.broadcasted_iota(jnp.int32, (MIX_TM, POOL_WIDTH), 0)
    grp = lane >> (POOL_GROUP_DIM.bit_length() - 1)
    run = u
    win_sum = jnp.zeros_like(u)
    cnt = jnp.zeros_like(u)
    done = 1
    for gidx, w in enumerate(POOL_WINDOWS):
        for kback in range(done, w):
            run = run + ext_scr[POOL_HALO - kback:POOL_HALO - kback + MIX_TM, :]
        done = w
        win_sum = jnp.where(grp == gidx, run, win_sum)
        cnt = jnp.where(grp == gidx, jnp.minimum(tpos + 1, w).astype(F32), cnt)
    pooled = win_sum / cnt - u
    mixed = _dot(pooled.astype(BF16), pw_ref[...]) * ps_ref[...]

    y = _dot(mixed.astype(BF16), wo_ref[0:POOL_WIDTH, :])
    for h in range(0, N_Q_HEADS, 2):
        pair = [(oc_ref[0, hh].astype(F32) + os_ref[0, hh].astype(F32)
                 + ow_ref[0, hh].astype(F32)).T.astype(BF16) for hh in (h, h + 1)]
        lo = POOL_WIDTH + h * HEAD_PAD
        y = y + _dot(jnp.concatenate(pair, axis=1), wo_ref[lo:lo + 2 * HEAD_PAD, :])
    z = ALPHA * x_ref[0] + y
    o_ref[0] = _layer_norm(z, g_ref[...], b_ref[...])


def _mix_out(x, u, o_cmp, o_slc, o_win, pool_w, pool_scale, w_out, g, b):
    bsz, s, _ = x.shape
    n_grp = len(POOL_WINDOWS)
    eye = jnp.eye(n_grp, dtype=F32)
    pw = (pool_w[:, :, None, :] * eye[:, None, :, None]).reshape(POOL_WIDTH, POOL_WIDTH).astype(BF16)
    wo_nsa = w_out[POOL_WIDTH:].reshape(N_Q_HEADS, HEAD_DIM, D_MODEL)
    wo_nsa = jnp.pad(wo_nsa, ((0, 0), (0, HEAD_PAD - HEAD_DIM), (0, 0))).reshape(Q_PAD, D_MODEL)
    wo = jnp.concatenate([w_out[:POOL_WIDTH], wo_nsa], axis=0).astype(BF16)

    def tile(width):
        return pl.BlockSpec((1, MIX_TM, width), lambda bb, i: (bb, i, 0))

    branch = pl.BlockSpec((1, N_Q_HEADS, HEAD_PAD, MIX_TM), lambda bb, i: (bb, 0, 0, i))
    halo_blocks = MIX_TM // POOL_HALO
    const = lambda bb, i: (0, 0)
    return pl.pallas_call(
        _mix_out_kernel,
        grid=(bsz, s // MIX_TM),
        in_specs=[
            tile(D_MODEL), tile(POOL_WIDTH),
            pl.BlockSpec((1, POOL_HALO, POOL_WIDTH),
                         lambda bb, i: (bb, jnp.maximum(i * halo_blocks - 1, 0), 0)),
            branch, branch, branch,
            pl.BlockSpec((POOL_WIDTH, POOL_WIDTH), const),
            pl.BlockSpec((1, POOL_WIDTH), const),
            pl.BlockSpec((POOL_WIDTH + Q_PAD, D_MODEL), const),
            pl.BlockSpec((1, D_MODEL), const),
            pl.BlockSpec((1, D_MODEL), const),
        ],
        out_specs=tile(D_MODEL),
        out_shape=jax.ShapeDtypeStruct((bsz, s, D_MODEL), F32),
        scratch_shapes=[pltpu.VMEM((POOL_HALO + MIX_TM, POOL_WIDTH), F32)],
        compiler_params=_params("parallel", "arbitrary"),
        name="mix_out_ln",
    )(x, u, u, o_cmp, o_slc, o_win, pw, pool_scale.reshape(1, -1), wo,
      g.reshape(1, -1), b.reshape(1, -1))


def kernel(x, ln1_g, ln1_b, ffn1_w_gate, ffn1_w_up, ffn1_w_down, w_in, b_gate, pool_w, pool_scale, cmp_pos_k, cmp_k_w1, cmp_k_w2, cmp_pos_v, cmp_v_w1, cmp_v_w2, w_out, ln2_g, ln2_b, ffn2_w_gate, ffn2_w_up, ffn2_w_down, ln3_g, ln3_b):
    bsz, s, d = x.shape
    assert d == D_MODEL and s % max(SUPER_KEYS, FFN_TM, MIX_TM, PROJ_TM) == 0
    for l in range(DEPTH):
        x = _ffn_ln(x.reshape(bsz * s, d), ffn1_w_gate[l], ffn1_w_up[l], ffn1_w_down[l],
                    ln1_g[l], ln1_b[l]).reshape(bsz, s, d)
        u, q, kc, vc, ks, kw, gates, vs_t, vw_t = _proj(x, w_in[l], b_gate[l])
        kcmp = _compress(kc, cmp_pos_k[l], cmp_k_w1[l], cmp_k_w2[l], channel_major=False)
        vcmp_t = _compress(vc, cmp_pos_v[l], cmp_v_w1[l], cmp_v_w2[l], channel_major=True)
        o_cmp, selq, seld = _cmp_attention(q, kcmp, vcmp_t, gates)
        o_slc = _slc_attention(q, ks, vs_t, selq, seld, gates)
        o_win = _win_attention(q, kw, vw_t, gates)
        x = _mix_out(x, u, o_cmp, o_slc, o_win, pool_w[l], pool_scale[l], w_out[l],
                     ln2_g[l], ln2_b[l])
        x = _ffn_ln(x.reshape(bsz * s, d), ffn2_w_gate[l], ffn2_w_up[l], ffn2_w_down[l],
                    ln3_g[l], ln3_b[l]).reshape(bsz, s, d)
    return x
```

```python
import functools

import numpy as np
import jax
import jax.numpy as jnp
from jax import lax
from jax.experimental import pallas as pl
from jax.experimental.pallas import tpu as pltpu

D_MODEL = 1024
DEPTH = 2
POOL_WIDTH = 256
POOL_WINDOWS = (2, 4, 8, 16)
POOL_GROUP_DIM = 64
N_Q_HEADS = 8
HEAD_DIM = 96
N_KV_GROUPS = 2
HEADS_PER_GROUP = 4
N_BRANCHES = 3
CMP_STRIDE = 16
CMP_BLOCK = 32
SLC_BLOCK = 64
SLC_BLOCK_LOG2 = 6
N_SELECT = 16
N_FORCED = 3
WINDOW = 512
D_FF = 2816
ALPHA = (2.0 * DEPTH) ** 0.25
LN_EPS = 1e-5
NEG_BIG = -1e30
SEL_BIG = 1e30
QK_SCALE = HEAD_DIM ** -0.5
LOG2_E = 1.4426950408889634
ONES_ROW = HEAD_DIM

LANES = 128
HEAD_PAD = LANES
Q_PAD = N_Q_HEADS * HEAD_PAD
KV_PAD = N_KV_GROUPS * HEAD_PAD
GROUP_Q = HEADS_PER_GROUP * HEAD_PAD
CMP_PER_SLC = SLC_BLOCK // CMP_STRIDE
CHUNK_FLAT = CMP_STRIDE * KV_PAD

VMEM_LIMIT = 56 * 1024 * 1024

F32 = jnp.float32
BF16 = jnp.bfloat16

_C_U = 0
_C_Q = _C_U + POOL_WIDTH
_C_KC = _C_Q + Q_PAD
_C_VC = _C_KC + KV_PAD
_C_KS = _C_VC + KV_PAD
_C_KW = _C_KS + KV_PAD
_C_END = _C_KW + KV_PAD


def _params(*sem):
    return pltpu.CompilerParams(dimension_semantics=sem, vmem_limit_bytes=VMEM_LIMIT)


def _layer_norm(z, g, b):
    mu = jnp.mean(z, axis=-1, keepdims=True)
    zc = z - mu
    var = jnp.mean(zc * zc, axis=-1, keepdims=True)
    return zc * lax.rsqrt(var + LN_EPS) * g + b


def _dot(a, b):
    return jnp.dot(a, b, preferred_element_type=F32)


def _dot_nt(a, b):
    return lax.dot_general(a, b, (((1,), (1,)), ((), ())), preferred_element_type=F32)


FFN_TM = 1024
MXU_TILE = 256
FFN_CHUNKS = (6 * MXU_TILE, 5 * MXU_TILE)
assert sum(FFN_CHUNKS) == D_FF


def _ffn_ln_kernel(x_ref, wg_ref, wu_ref, wd_ref, g_ref, b_ref, o_ref):
    x = x_ref[...]
    xb = x.astype(BF16)
    acc = None
    lo = 0
    for width in FFN_CHUNKS:
        hg = _dot(xb, wg_ref[:, lo:lo + width])
        hu = _dot(xb, wu_ref[:, lo:lo + width])
        h = (hg * jax.nn.sigmoid(hg)) * hu
        part = _dot(h.astype(BF16), wd_ref[lo:lo + width, :])
        acc = part if acc is None else acc + part
        lo += width
    z = ALPHA * x + 0.5 * acc
    o_ref[...] = _layer_norm(z, g_ref[...], b_ref[...])


def _ffn_ln(x2d, wg, wu, wd, g, b):
    t = x2d.shape[0]
    const = lambda i: (0, 0)
    return pl.pallas_call(
        _ffn_ln_kernel,
        grid=(t // FFN_TM,),
        in_specs=[
            pl.BlockSpec((FFN_TM, D_MODEL), lambda i: (i, 0)),
            pl.BlockSpec((D_MODEL, D_FF), const, pipeline_mode=pl.Buffered(1)),
            pl.BlockSpec((D_MODEL, D_FF), const, pipeline_mode=pl.Buffered(1)),
            pl.BlockSpec((D_FF, D_MODEL), const, pipeline_mode=pl.Buffered(1)),
            pl.BlockSpec((1, D_MODEL), const),
            pl.BlockSpec((1, D_MODEL), const),
        ],
        out_specs=pl.BlockSpec((FFN_TM, D_MODEL), lambda i: (i, 0)),
        out_shape=jax.ShapeDtypeStruct((t, D_MODEL), F32),
        compiler_params=_params("parallel"),
        name="ffn_ln",
    )(x2d, wg.astype(BF16), wu.astype(BF16), wd.astype(BF16), g.reshape(1, -1), b.reshape(1, -1))


PROJ_TM = 1024
MASK_LANES = HEAD_PAD - HEAD_DIM


def _proj_kernel(x_ref, w_ref, wvt_ref, bg_ref, u_ref, q_ref, kc_ref, vc_ref, ks_ref, kw_ref,
                 gt_ref, vst_ref, vwt_ref):
    xb = x_ref[0].astype(BF16)

    def mm(lo, n):
        return _dot(xb, w_ref[:, lo:lo + n])

    u_ref[0] = mm(_C_U, POOL_WIDTH)
    q_ref[0] = (mm(_C_Q, Q_PAD) * (QK_SCALE * LOG2_E)).astype(BF16)
    kc_ref[0] = mm(_C_KC, KV_PAD)
    vc_ref[0] = mm(_C_VC, KV_PAD)
    tpos = pl.program_id(1) * PROJ_TM + lax.broadcasted_iota(jnp.int32, (PROJ_TM, KV_PAD), 0)
    lane = lax.broadcasted_iota(jnp.int32, (PROJ_TM, KV_PAD), 1) & (HEAD_PAD - 1)
    hot = lane == HEAD_DIM + ((tpos >> SLC_BLOCK_LOG2) & (MASK_LANES - 1))
    ks_ref[0] = jnp.where(hot, 1.0, mm(_C_KS, KV_PAD)).astype(BF16)
    kw_ref[0] = mm(_C_KW, KV_PAD).astype(BF16)
    gt_ref[0] = jax.nn.sigmoid(_dot_nt(wvt_ref[2 * KV_PAD:3 * KV_PAD, :], xb) + bg_ref[...])
    chan = lax.broadcasted_iota(jnp.int32, (KV_PAD, PROJ_TM), 0) & (HEAD_PAD - 1)
    vst_ref[0] = jnp.where(chan == ONES_ROW, 1.0, _dot_nt(wvt_ref[0:KV_PAD, :], xb)).astype(BF16)
    vwt_ref[0] = jnp.where(chan == ONES_ROW, 1.0,
                           _dot_nt(wvt_ref[KV_PAD:2 * KV_PAD, :], xb)).astype(BF16)


def _pad_heads(w, n_heads):
    lead = w.shape[:-1]
    w = w.reshape(lead + (n_heads, HEAD_DIM))
    w = jnp.pad(w, [(0, 0)] * len(lead) + [(0, 0), (0, HEAD_PAD - HEAD_DIM)])
    return w.reshape(lead + (n_heads * HEAD_PAD,))


def _pad_gate_cols(w):
    lead = w.shape[:-1]
    per_group = HEADS_PER_GROUP * N_BRANCHES
    w = w.reshape(lead + (N_KV_GROUPS, per_group))
    w = jnp.pad(w, [(0, 0)] * len(lead) + [(0, 0), (0, HEAD_PAD - per_group)])
    return w.reshape(lead + (KV_PAD,))


def _proj(x, w_in, b_gate):
    bsz, s, _ = x.shape
    cuts = np.cumsum([POOL_WIDTH, N_Q_HEADS * HEAD_DIM] + [N_KV_GROUPS * HEAD_DIM] * 6)
    parts = jnp.split(w_in, [int(c) for c in cuts], axis=-1)
    cols = [parts[0], _pad_heads(parts[1], N_Q_HEADS)]
    cols += [_pad_heads(parts[i], N_KV_GROUPS) for i in (2, 3, 4, 6)]
    w = jnp.concatenate(cols, axis=-1).astype(BF16)
    wvt = jnp.concatenate([_pad_heads(parts[5], N_KV_GROUPS), _pad_heads(parts[7], N_KV_GROUPS),
                           _pad_gate_cols(parts[8])], axis=-1).T.astype(BF16)
    bg = _pad_gate_cols(b_gate).reshape(KV_PAD, 1)

    def tile(width):
        return pl.BlockSpec((1, PROJ_TM, width), lambda b, i: (b, i, 0))

    def out(width, dtype):
        return jax.ShapeDtypeStruct((bsz, s, width), dtype)

    vt_spec = pl.BlockSpec((1, KV_PAD, PROJ_TM), lambda b, i: (b, 0, i))
    vt_out = jax.ShapeDtypeStruct((bsz, KV_PAD, s), BF16)
    return pl.pallas_call(
        _proj_kernel,
        grid=(bsz, s // PROJ_TM),
        in_specs=[
            tile(D_MODEL),
            pl.BlockSpec((D_MODEL, _C_END), lambda b, i: (0, 0), pipeline_mode=pl.Buffered(1)),
            pl.BlockSpec((3 * KV_PAD, D_MODEL), lambda b, i: (0, 0), pipeline_mode=pl.Buffered(1)),
            pl.BlockSpec((KV_PAD, 1), lambda b, i: (0, 0)),
        ],
        out_specs=[tile(POOL_WIDTH), tile(Q_PAD), tile(KV_PAD), tile(KV_PAD), tile(KV_PAD),
                   tile(KV_PAD), vt_spec, vt_spec, vt_spec],
        out_shape=[out(POOL_WIDTH, F32), out(Q_PAD, BF16), out(KV_PAD, F32), out(KV_PAD, F32),
                   out(KV_PAD, BF16), out(KV_PAD, BF16),
                   jax.ShapeDtypeStruct((bsz, KV_PAD, s), F32), vt_out, vt_out],
        compiler_params=_params("parallel", "parallel"),
        name="in_proj",
    )(x, w, wvt, bg)


def _gelu_tanh(x):
    c = np.float32(np.sqrt(2.0 / np.pi))
    return x * (0.5 * (1.0 + jnp.tanh(c * (x + 0.044715 * (x * x * x)))))


def _compress_kernel(x_ref, plo_ref, phi_ref, wlo_ref, whi_ref, w2_ref, o_ref, a_scr, b_scr,
                     *, channel_major):
    r = pl.program_id(1)
    nsb = x_ref.shape[1]
    xr = x_ref[0]
    a_scr[r] = _dot((xr + plo_ref[...]).astype(BF16), wlo_ref[...])
    b_scr[r, 0:nsb, :] = _dot((xr + phi_ref[...]).astype(BF16), whi_ref[...])

    @pl.when(r == 0)
    def _():
        b_scr[0, nsb:nsb + 8, :] = jnp.zeros((8, KV_PAD), F32)

    @pl.when(r == CMP_PER_SLC - 1)
    def _():
        for rr in range(CMP_PER_SLC):
            if rr < CMP_PER_SLC - 1:
                h = a_scr[rr] + b_scr[rr + 1, 0:nsb, :]
            else:
                h = a_scr[rr] + b_scr[0, 1:nsb + 1, :]
            act = _gelu_tanh(h).astype(BF16)
            if channel_major:
                o_ref[0, :, rr * nsb:(rr + 1) * nsb] = _dot_nt(w2_ref[...], act).astype(BF16)
            else:
                o_ref[0, rr * nsb:(rr + 1) * nsb, :] = _dot(act, w2_ref[...]).astype(BF16)


def _compress_weights(pos, w1, w2):
    eye_g = jnp.eye(N_KV_GROUPS, dtype=F32)
    w1r = w1.reshape(CMP_BLOCK, HEAD_DIM, HEAD_DIM)
    w1r = jnp.pad(w1r, ((0, 0), (0, HEAD_PAD - HEAD_DIM), (0, HEAD_PAD - HEAD_DIM)))
    w1c = w1r[:, None, :, None, :] * eye_g[None, :, None, :, None]
    w1c = w1c.reshape(CMP_BLOCK, KV_PAD, KV_PAD)
    wlo = w1c[:CMP_STRIDE].reshape(CHUNK_FLAT, KV_PAD).astype(BF16)
    whi = w1c[CMP_STRIDE:].reshape(CHUNK_FLAT, KV_PAD).astype(BF16)
    posp = jnp.pad(pos, ((0, 0), (0, HEAD_PAD - HEAD_DIM)))
    posp = jnp.tile(posp[:, None, :], (1, N_KV_GROUPS, 1))
    plo = posp[:CMP_STRIDE].reshape(1, CHUNK_FLAT)
    phi = posp[CMP_STRIDE:].reshape(1, CHUNK_FLAT)
    w2p = jnp.pad(w2, ((0, HEAD_PAD - HEAD_DIM), (0, HEAD_PAD - HEAD_DIM)))
    w2c = (w2p[None, :, None, :] * eye_g[:, None, :, None]).reshape(KV_PAD, KV_PAD).astype(BF16)
    return plo, phi, wlo, whi, w2c


def _compress(kv, pos, w1, w2, channel_major):
    bsz, s, _ = kv.shape
    nsb = s // SLC_BLOCK
    ncp = CMP_PER_SLC * nsb
    plo, phi, wlo, whi, w2c = _compress_weights(pos, w1, w2)
    if channel_major:
        w2c = w2c.T
    out_dims = (KV_PAD, ncp) if channel_major else (ncp, KV_PAD)
    x = kv.reshape(bsz, nsb, CMP_PER_SLC * CHUNK_FLAT)
    const = lambda b, r: (0, 0)
    return pl.pallas_call(
        functools.partial(_compress_kernel, channel_major=channel_major),
        grid=(bsz, CMP_PER_SLC),
        in_specs=[
            pl.BlockSpec((1, nsb, CHUNK_FLAT), lambda b, r: (b, 0, r)),
            pl.BlockSpec((1, CHUNK_FLAT), const),
            pl.BlockSpec((1, CHUNK_FLAT), const),
            pl.BlockSpec((CHUNK_FLAT, KV_PAD), const),
            pl.BlockSpec((CHUNK_FLAT, KV_PAD), const),
            pl.BlockSpec((KV_PAD, KV_PAD), const),
        ],
        out_specs=pl.BlockSpec((1,) + out_dims, lambda b, r: (b, 0, 0)),
        out_shape=jax.ShapeDtypeStruct((bsz,) + out_dims, BF16),
        scratch_shapes=[pltpu.VMEM((CMP_PER_SLC, nsb, KV_PAD), F32),
                        pltpu.VMEM((CMP_PER_SLC, nsb + 8, KV_PAD), F32)],
        compiler_params=_params("parallel", "arbitrary"),
        name="compress",
    )(x, plo, phi, wlo, whi, w2c)


TQ = 512
SLC_TQ = 512


def _load_q(q_ref):
    return jnp.concatenate(
        [q_ref[0, :, h * HEAD_PAD:(h + 1) * HEAD_PAD] for h in range(HEADS_PER_GROUP)], axis=0)


def _per_head(row):
    return jnp.concatenate([row] * HEADS_PER_GROUP, axis=1)


def _store_gated(o_ref, g_ref, o_t, norm, branch, cols=None):
    cols = slice(None) if cols is None else cols
    gate = jnp.concatenate([g_ref[0, h * N_BRANCHES + branch:h * N_BRANCHES + branch + 1, cols]
                            for h in range(HEADS_PER_GROUP)], axis=1)
    if norm is not None:
        gate = gate / jnp.maximum(norm, 1e-30)
    o = (o_t * gate).astype(o_ref.dtype)
    tq = o.shape[1] // HEADS_PER_GROUP
    for h in range(HEADS_PER_GROUP):
        o_ref[0, h, :, cols] = o[:, h * tq:(h + 1) * tq]


def _attn_specs(tq=TQ):
    q_spec = pl.BlockSpec((1, tq, GROUP_Q), lambda b, g, i: (b, i, g))
    g_spec = pl.BlockSpec((1, HEAD_PAD, tq), lambda b, g, i: (b, g, i))
    o_spec = pl.BlockSpec((1, HEADS_PER_GROUP, HEAD_PAD, tq), lambda b, g, i: (b, g, 0, i))
    return q_spec, g_spec, o_spec


def _branch_out_shape(bsz, s):
    return jax.ShapeDtypeStruct((bsz, N_Q_HEADS, HEAD_PAD, s), BF16)


def _cmp_kernel(q_ref, kc_ref, vct_ref, g_ref, cend_ref, place_ref, o_ref, selq_ref):
    nsb = kc_ref.shape[1] // CMP_PER_SLC
    t0 = pl.program_id(2) * TQ
    half = nsb // 2
    if half % LANES == 0:
        @pl.when(t0 + TQ <= half * SLC_BLOCK)
        def _():
            _cmp_body(q_ref, kc_ref, vct_ref, g_ref, cend_ref, place_ref, o_ref, selq_ref, half)

        @pl.when(t0 + TQ > half * SLC_BLOCK)
        def _():
            _cmp_body(q_ref, kc_ref, vct_ref, g_ref, cend_ref, place_ref, o_ref, selq_ref, nsb)
    else:
        _cmp_body(q_ref, kc_ref, vct_ref, g_ref, cend_ref, place_ref, o_ref, selq_ref, nsb)


def _cmp_body(q_ref, kc_ref, vct_ref, g_ref, cend_ref, place_ref, o_ref, selq_ref, nvis):
    nsb_all = kc_ref.shape[1] // CMP_PER_SLC
    nsb = nvis
    t0 = pl.program_id(2) * TQ

    def slabs(ref_rows):
        if nvis == nsb_all:
            return ref_rows(0, CMP_PER_SLC * nsb_all)
        return jnp.concatenate([ref_rows(r * nsb_all, nvis) for r in range(CMP_PER_SLC)], axis=0)

    kc = slabs(lambda lo, n: kc_ref[0, lo:lo + n, :])
    cend = slabs(lambda lo, n: cend_ref[lo:lo + n, :])
    if nvis == nsb_all:
        vct = vct_ref[0]
    else:
        vct = jnp.concatenate([vct_ref[0, :, r * nsb_all:r * nsb_all + nvis]
                               for r in range(CMP_PER_SLC)], axis=1)

    s = _dot_nt(kc, _load_q(q_ref))
    s = s + _per_head(jnp.where(cend <= t0, 0.0, NEG_BIG))
    m = jnp.max(s, axis=0, keepdims=True)
    e = jnp.exp2(s - m)
    l = jnp.sum(e, axis=0, keepdims=True)
    tcol = t0 + lax.broadcasted_iota(jnp.int32, (1, TQ), 1)
    any_visible = _per_head(jnp.where(tcol >= CMP_BLOCK - 1, 1.0, 0.0))
    p = e * (any_visible / jnp.maximum(l, 1e-30))

    o_t = _dot(vct, p.astype(BF16))
    _store_gated(o_ref, g_ref, o_t, None, 0)

    imp = p[:, 0:TQ]
    for h in range(1, HEADS_PER_GROUP):
        imp = imp + p[:, h * TQ:(h + 1) * TQ]
    p0, p1, p2, p3 = (imp[r * nsb:(r + 1) * nsb, :] for r in range(CMP_PER_SLC))
    blk = lax.broadcasted_iota(jnp.int32, (nsb, TQ), 0)
    p3_prev = jnp.where(blk == 0, 0.0, pltpu.roll(p3, 1, axis=0))
    imp_slc = 0.5 * p3_prev + p0 + p1 + p2 + 0.5 * p3

    jt = (t0 + lax.broadcasted_iota(jnp.int32, (nsb, TQ), 1)) >> SLC_BLOCK_LOG2
    forced = (blk == 0) | (blk == jt) | (blk == jt - 1)
    free = (blk <= jt) & jnp.logical_not(forced)
    score = jnp.where(free, imp_slc, NEG_BIG)

    def pick(_, sc):
        mx = jnp.max(sc, axis=0, keepdims=True)
        first = jnp.min(jnp.where(sc == mx, blk, nsb), axis=0, keepdims=True)
        return jnp.where(blk == first, -jnp.inf, sc)

    picked = lax.fori_loop(0, min(N_SELECT - N_FORCED, nsb), pick, score, unroll=True) == -jnp.inf
    selected = (picked & free) | forced
    bias = jnp.where(selected, 0.0, NEG_BIG)
    if nvis < nsb_all:
        bias = jnp.concatenate([bias, jnp.full((nsb_all - nvis, TQ), NEG_BIG, F32)], axis=0)
    selq_ref[0, 0] = _dot(bias.T.astype(BF16), place_ref[...]).astype(BF16)


def _mask_lane_placement(nsb):
    j = np.arange(nsb)
    place = np.zeros((nsb, (nsb // MASK_LANES) * HEAD_PAD), np.float32)
    place[j, (j // MASK_LANES) * HEAD_PAD + HEAD_DIM + j % MASK_LANES] = 1.0
    return jnp.asarray(place, BF16)


def _cmp_end_minus_token(nsb):
    row = np.arange(CMP_PER_SLC * nsb)
    end = (row % nsb) * SLC_BLOCK + (row // nsb) * CMP_STRIDE + CMP_BLOCK - 1
    return jnp.asarray(end[:, None] - np.arange(TQ)[None, :], jnp.int32)


def _cmp_attention(q, kcmp, vcmp_t, gates):
    bsz, s, _ = q.shape
    nsb = s // SLC_BLOCK
    ncp = CMP_PER_SLC * nsb
    selq_w = (nsb // MASK_LANES) * HEAD_PAD
    q_spec, g_spec, o_spec = _attn_specs()
    const = lambda b, g, i: (0, 0)
    return pl.pallas_call(
        _cmp_kernel,
        grid=(bsz, N_KV_GROUPS, s // TQ),
        in_specs=[q_spec,
                  pl.BlockSpec((1, ncp, HEAD_PAD), lambda b, g, i: (b, 0, g)),
                  pl.BlockSpec((1, HEAD_PAD, ncp), lambda b, g, i: (b, g, 0)),
                  g_spec,
                  pl.BlockSpec((ncp, TQ), const),
                  pl.BlockSpec((nsb, selq_w), const)],
        out_specs=[o_spec, pl.BlockSpec((1, 1, TQ, selq_w), lambda b, g, i: (b, g, i, 0))],
        out_shape=[_branch_out_shape(bsz, s),
                   jax.ShapeDtypeStruct((bsz, N_KV_GROUPS, s, selq_w), BF16)],
        compiler_params=_params("parallel", "parallel", "arbitrary"),
        name="cmp_attn_topk",
    )(q, kcmp, vcmp_t, gates, _cmp_end_minus_token(nsb), _mask_lane_placement(nsb))


SLC_TK = 512
SUPER_KEYS = MASK_LANES * SLC_BLOCK
SLC_STEP = SUPER_KEYS
assert SLC_TQ == SLC_TK
assert SUPER_KEYS % SLC_STEP == 0 and SLC_STEP % SLC_TK == 0


def _slc_kernel(q_ref, k_ref, vt_ref, selq_ref, g_ref, o_ref, s_scr):
    cols = HEADS_PER_GROUP * SLC_TQ
    t0 = pl.multiple_of(pl.program_id(2) * SLC_TQ, SLC_TQ)
    n_full = t0 // SLC_STEP
    tail_tiles = (t0 - n_full * SLC_STEP) // SLC_TK

    def masked_q(slab):
        return jnp.concatenate([q_ref[0, :, h * HEAD_PAD:(h + 1) * HEAD_PAD] + slab
                                for h in range(HEADS_PER_GROUP)], axis=0)

    def accumulate(k0, n_keys, m, m_new, acc):
        acc = jnp.exp2(m - m_new) * acc
        for c in range(0, n_keys, SLC_TK):
            p = jnp.exp2(s_scr[c:c + SLC_TK, :] - m_new).astype(BF16)
            acc = acc + _dot(vt_ref[0, :, pl.ds(k0 + c, SLC_TK)], p)
        return m_new, acc

    def step_q(step):
        lane0 = pl.multiple_of((step * SLC_STEP // SUPER_KEYS) * HEAD_PAD, HEAD_PAD)
        return masked_q(selq_ref[0, 0, :, pl.ds(lane0, HEAD_PAD)])

    def sweep(st, carry):
        m, acc = carry
        k0 = pl.multiple_of(st * SLC_STEP, SLC_STEP)
        sc = _dot_nt(k_ref[0, pl.ds(k0, SLC_STEP), :], step_q(st))
        s_scr[...] = sc
        m_new = jnp.maximum(m, jnp.max(sc, axis=0, keepdims=True))
        return accumulate(k0, SLC_STEP, m, m_new, acc)

    def last_step(n_before, carry):
        m, acc = carry
        k0 = pl.multiple_of(n_full * SLC_STEP, SLC_STEP)
        n_keys = n_before + SLC_TQ
        sc = _dot_nt(k_ref[0, pl.ds(k0, n_keys), :], step_q(n_full))
        kk = lax.broadcasted_iota(jnp.int32, (SLC_TQ, SLC_TQ), 0)
        tt = lax.broadcasted_iota(jnp.int32, (SLC_TQ, SLC_TQ), 1)
        own = sc[n_before:, :] + _per_head(jnp.where(kk <= tt, 0.0, NEG_BIG))
        s_scr[n_before:n_keys, :] = own
        m_new = jnp.maximum(m, jnp.max(own, axis=0, keepdims=True))
        if n_before:
            s_scr[0:n_before, :] = sc[:n_before, :]
            m_new = jnp.maximum(m_new, jnp.max(sc[:n_before, :], axis=0, keepdims=True))
        return accumulate(k0, n_keys, m, m_new, acc)

    carry = (jnp.full((1, cols), NEG_BIG, F32), jnp.zeros((HEAD_PAD, cols), F32))
    carry = lax.fori_loop(0, n_full, sweep, carry)
    _, acc = lax.switch(tail_tiles, [functools.partial(last_step, n * SLC_TK)
                                     for n in range(SLC_STEP // SLC_TK)], carry)
    _store_gated(o_ref, g_ref, acc, acc[ONES_ROW:ONES_ROW + 1, :], 1)


def _slc_attention(q, ks, vs_t, selq, gates):
    bsz, s, _ = q.shape
    n_super = s // SUPER_KEYS
    q_spec, g_spec, o_spec = _attn_specs(SLC_TQ)
    return pl.pallas_call(
        _slc_kernel,
        grid=(bsz, N_KV_GROUPS, s // SLC_TQ),
        in_specs=[q_spec,
                  pl.BlockSpec((1, s, HEAD_PAD), lambda b, g, i: (b, 0, g)),
                  pl.BlockSpec((1, HEAD_PAD, s), lambda b, g, i: (b, g, 0)),
                  pl.BlockSpec((1, 1, SLC_TQ, n_super * HEAD_PAD), lambda b, g, i: (b, g, i, 0)),
                  g_spec],
        out_specs=o_spec,
        out_shape=_branch_out_shape(bsz, s),
        scratch_shapes=[pltpu.VMEM((SLC_STEP, HEADS_PER_GROUP * SLC_TQ), F32)],
        compiler_params=_params("parallel", "parallel", "arbitrary"),
        name="slc_attn",
    )(q, ks, vs_t, selq, gates)


WIN_TQ = 512
WIN_SUB = 256
WIN_KEYS = WINDOW + WIN_SUB


def _win_kernel(q_ref, k_ref, vt_ref, g_ref, o_ref):
    for sub in range(WIN_TQ // WIN_SUB):
        rows = slice(sub * WIN_SUB, (sub + 1) * WIN_SUB)
        t0 = pl.program_id(2) * WIN_TQ + sub * WIN_SUB
        k0 = pl.multiple_of(jnp.maximum(t0 - WINDOW, 0), WIN_SUB)
        q = jnp.concatenate([q_ref[0, rows, h * HEAD_PAD:(h + 1) * HEAD_PAD]
                             for h in range(HEADS_PER_GROUP)], axis=0)
        s = _dot_nt(k_ref[0, pl.ds(k0, WIN_KEYS), :], q)
        diff = ((t0 + lax.broadcasted_iota(jnp.int32, (WIN_KEYS, WIN_SUB), 1))
                - (k0 + lax.broadcasted_iota(jnp.int32, (WIN_KEYS, WIN_SUB), 0)))
        s = s + _per_head(jnp.where((diff >= 0) & (diff < WINDOW), 0.0, NEG_BIG))
        m = jnp.max(s, axis=0, keepdims=True)
        e = jnp.exp2(s - m).astype(BF16)
        o_t = _dot(vt_ref[0, :, pl.ds(k0, WIN_KEYS)], e)
        _store_gated(o_ref, g_ref, o_t, o_t[ONES_ROW:ONES_ROW + 1, :], 2, cols=rows)


def _win_attention(q, kw, vw_t, gates):
    bsz, s, _ = q.shape
    q_spec, g_spec, o_spec = _attn_specs(WIN_TQ)
    return pl.pallas_call(
        _win_kernel,
        grid=(bsz, N_KV_GROUPS, s // WIN_TQ),
        in_specs=[q_spec,
                  pl.BlockSpec((1, s, HEAD_PAD), lambda b, g, i: (b, 0, g)),
                  pl.BlockSpec((1, HEAD_PAD, s), lambda b, g, i: (b, g, 0)),
                  g_spec],
        out_specs=o_spec,
        out_shape=_branch_out_shape(bsz, s),
        compiler_params=_params("parallel", "parallel", "arbitrary"),
        name="win_attn",
    )(q, kw, vw_t, gates)


MIX_TM = 1024
POOL_HALO = 16


def _mix_out_kernel(x_ref, u_ref, halo_ref, oc_ref, os_ref, ow_ref, pw_ref, ps_ref,
                    wo_ref, g_ref, b_ref, o_ref, ext_scr):
    i = pl.program_id(1)
    u = u_ref[0]
    halo = jnp.where(i == 0, 0.0, halo_ref[0])
    ext_scr[0:POOL_HALO, :] = halo
    ext_scr[POOL_HALO:POOL_HALO + MIX_TM, :] = u

    lane = lax.broadcasted_iota(jnp.int32, (MIX_TM, POOL_WIDTH), 1)
    tpos = i * MIX_TM + lax.broadcasted_iota(jnp.int32, (MIX_TM, POOL_WIDTH), 0)
    grp = lane >> (POOL_GROUP_DIM.bit_length() - 1)
    run = u
    win_sum = jnp.zeros_like(u)
    cnt = jnp.zeros_like(u)
    done = 1
    for gidx, w in enumerate(POOL_WINDOWS):
        for kback in range(done, w):
            run = run + ext_scr[POOL_HALO - kback:POOL_HALO - kback + MIX_TM, :]
        done = w
        win_sum = jnp.where(grp == gidx, run, win_sum)
        cnt = jnp.where(grp == gidx, jnp.minimum(tpos + 1, w).astype(F32), cnt)
    pooled = win_sum / cnt - u
    mixed = _dot(pooled.astype(BF16), pw_ref[...]) * ps_ref[...]

    y = _dot(mixed.astype(BF16), wo_ref[0:POOL_WIDTH, :])
    for h in range(0, N_Q_HEADS, 2):
        pair = [(oc_ref[0, hh].astype(F32) + os_ref[0, hh].astype(F32)
                 + ow_ref[0, hh].astype(F32)).T.astype(BF16) for hh in (h, h + 1)]
        lo = POOL_WIDTH + h * HEAD_PAD
        y = y + _dot(jnp.concatenate(pair, axis=1), wo_ref[lo:lo + 2 * HEAD_PAD, :])
    z = ALPHA * x_ref[0] + y
    o_ref[0] = _layer_norm(z, g_ref[...], b_ref[...])


def _mix_out(x, u, o_cmp, o_slc, o_win, pool_w, pool_scale, w_out, g, b):
    bsz, s, _ = x.shape
    n_grp = len(POOL_WINDOWS)
    eye = jnp.eye(n_grp, dtype=F32)
    pw = (pool_w[:, :, None, :] * eye[:, None, :, None]).reshape(POOL_WIDTH, POOL_WIDTH).astype(BF16)
    wo_nsa = w_out[POOL_WIDTH:].reshape(N_Q_HEADS, HEAD_DIM, D_MODEL)
    wo_nsa = jnp.pad(wo_nsa, ((0, 0), (0, HEAD_PAD - HEAD_DIM), (0, 0))).reshape(Q_PAD, D_MODEL)
    wo = jnp.concatenate([w_out[:POOL_WIDTH], wo_nsa], axis=0).astype(BF16)

    def tile(width):
        return pl.BlockSpec((1, MIX_TM, width), lambda bb, i: (bb, i, 0))

    branch = pl.BlockSpec((1, N_Q_HEADS, HEAD_PAD, MIX_TM), lambda bb, i: (bb, 0, 0, i))
    halo_blocks = MIX_TM // POOL_HALO
    const = lambda bb, i: (0, 0)
    return pl.pallas_call(
        _mix_out_kernel,
        grid=(bsz, s // MIX_TM),
        in_specs=[
            tile(D_MODEL), tile(POOL_WIDTH),
            pl.BlockSpec((1, POOL_HALO, POOL_WIDTH),
                         lambda bb, i: (bb, jnp.maximum(i * halo_blocks - 1, 0), 0)),
            branch, branch, branch,
            pl.BlockSpec((POOL_WIDTH, POOL_WIDTH), const),
            pl.BlockSpec((1, POOL_WIDTH), const),
            pl.BlockSpec((POOL_WIDTH + Q_PAD, D_MODEL), const),
            pl.BlockSpec((1, D_MODEL), const),
            pl.BlockSpec((1, D_MODEL), const),
        ],
        out_specs=tile(D_MODEL),
        out_shape=jax.ShapeDtypeStruct((bsz, s, D_MODEL), F32),
        scratch_shapes=[pltpu.VMEM((POOL_HALO + MIX_TM, POOL_WIDTH), F32)],
        compiler_params=_params("parallel", "arbitrary"),
        name="mix_out_ln",
    )(x, u, u, o_cmp, o_slc, o_win, pw, pool_scale.reshape(1, -1), wo,
      g.reshape(1, -1), b.reshape(1, -1))


def kernel(x, ln1_g, ln1_b, ffn1_w_gate, ffn1_w_up, ffn1_w_down, w_in, b_gate, pool_w, pool_scale, cmp_pos_k, cmp_k_w1, cmp_k_w2, cmp_pos_v, cmp_v_w1, cmp_v_w2, w_out, ln2_g, ln2_b, ffn2_w_gate, ffn2_w_up, ffn2_w_down, ln3_g, ln3_b):
    bsz, s, d = x.shape
    assert d == D_MODEL and s % max(SUPER_KEYS, FFN_TM, MIX_TM, PROJ_TM) == 0
    for l in range(DEPTH):
        x = _ffn_ln(x.reshape(bsz * s, d), ffn1_w_gate[l], ffn1_w_up[l], ffn1_w_down[l],
                    ln1_g[l], ln1_b[l]).reshape(bsz, s, d)
        u, q, kc, vc, ks, kw, gates, vs_t, vw_t = _proj(x, w_in[l], b_gate[l])
        kcmp = _compress(kc, cmp_pos_k[l], cmp_k_w1[l], cmp_k_w2[l], channel_major=False)
        vcmp_t = _compress(vc, cmp_pos_v[l], cmp_v_w1[l], cmp_v_w2[l], channel_major=True)
        o_cmp, selq = _cmp_attention(q, kcmp, vcmp_t, gates)
        o_slc = _slc_attention(q, ks, vs_t, selq, gates)
        o_win = _win_attention(q, kw, vw_t, gates)
        x = _mix_out(x, u, o_cmp, o_slc, o_win, pool_w[l], pool_scale[l], w_out[l],
                     ln2_g[l], ln2_b[l])
        x = _ffn_ln(x.reshape(bsz * s, d), ffn2_w_gate[l], ffn2_w_up[l], ffn2_w_down[l],
                    ln3_g[l], ln3_b[l]).reshape(bsz, s, d)
    return x
```

```python
import functools

import numpy as np
import jax
import jax.numpy as jnp
from jax import lax
from jax.experimental import pallas as pl
from jax.experimental.pallas import tpu as pltpu

D_MODEL = 1024
DEPTH = 2
POOL_WIDTH = 256
POOL_WINDOWS = (2, 4, 8, 16)
POOL_GROUP_DIM = 64
N_Q_HEADS = 8
HEAD_DIM = 96
N_KV_GROUPS = 2
HEADS_PER_GROUP = 4
N_BRANCHES = 3
CMP_STRIDE = 16
CMP_BLOCK = 32
SLC_BLOCK = 64
SLC_BLOCK_LOG2 = 6
N_SELECT = 16
N_FORCED = 3
WINDOW = 512
D_FF = 2816
ALPHA = (2.0 * DEPTH) ** 0.25
LN_EPS = 1e-5
NEG_BIG = -1e30
SEL_BIG = 1e30
QK_SCALE = HEAD_DIM ** -0.5
LOG2_E = 1.4426950408889634
ONES_ROW = HEAD_DIM

LANES = 128
HEAD_PAD = LANES
Q_PAD = N_Q_HEADS * HEAD_PAD
KV_PAD = N_KV_GROUPS * HEAD_PAD
GROUP_Q = HEADS_PER_GROUP * HEAD_PAD
CMP_PER_SLC = SLC_BLOCK // CMP_STRIDE
CHUNK_FLAT = CMP_STRIDE * KV_PAD

VMEM_LIMIT = 56 * 1024 * 1024

F32 = jnp.float32
BF16 = jnp.bfloat16

_C_U = 0
_C_Q = _C_U + POOL_WIDTH
_C_KC = _C_Q + Q_PAD
_C_VC = _C_KC + KV_PAD
_C_KS = _C_VC + KV_PAD
_C_KW = _C_KS + KV_PAD
_C_END = _C_KW + KV_PAD


def _params(*sem):
    return pltpu.CompilerParams(dimension_semantics=sem, vmem_limit_bytes=VMEM_LIMIT)


def _layer_norm(z, g, b):
    mu = jnp.mean(z, axis=-1, keepdims=True)
    zc = z - mu
    var = jnp.mean(zc * zc, axis=-1, keepdims=True)
    return zc * lax.rsqrt(var + LN_EPS) * g + b


def _dot(a, b):
    return jnp.dot(a, b, preferred_element_type=F32)


def _dot_nt(a, b):
    return lax.dot_general(a, b, (((1,), (1,)), ((), ())), preferred_element_type=F32)


FFN_TM = 1024
MXU_TILE = 256
FFN_CHUNKS = (6 * MXU_TILE, 5 * MXU_TILE)
assert sum(FFN_CHUNKS) == D_FF


def _ffn_ln_kernel(x_ref, wg_ref, wu_ref, wd_ref, g_ref, b_ref, o_ref):
    x = x_ref[...]
    xb = x.astype(BF16)
    acc = None
    lo = 0
    for width in FFN_CHUNKS:
        hg = _dot(xb, wg_ref[:, lo:lo + width])
        hu = _dot(xb, wu_ref[:, lo:lo + width])
        h = (hg * jax.nn.sigmoid(hg)) * hu
        part = _dot(h.astype(BF16), wd_ref[lo:lo + width, :])
        acc = part if acc is None else acc + part
        lo += width
    z = ALPHA * x + 0.5 * acc
    o_ref[...] = _layer_norm(z, g_ref[...], b_ref[...])


def _ffn_ln(x2d, wg, wu, wd, g, b):
    t = x2d.shape[0]
    const = lambda i: (0, 0)
    return pl.pallas_call(
        _ffn_ln_kernel,
        grid=(t // FFN_TM,),
        in_specs=[
            pl.BlockSpec((FFN_TM, D_MODEL), lambda i: (i, 0)),
            pl.BlockSpec((D_MODEL, D_FF), const, pipeline_mode=pl.Buffered(1)),
            pl.BlockSpec((D_MODEL, D_FF), const, pipeline_mode=pl.Buffered(1)),
            pl.BlockSpec((D_FF, D_MODEL), const, pipeline_mode=pl.Buffered(1)),
            pl.BlockSpec((1, D_MODEL), const),
            pl.BlockSpec((1, D_MODEL), const),
        ],
        out_specs=pl.BlockSpec((FFN_TM, D_MODEL), lambda i: (i, 0)),
        out_shape=jax.ShapeDtypeStruct((t, D_MODEL), F32),
        compiler_params=_params("parallel"),
        name="ffn_ln",
    )(x2d, wg.astype(BF16), wu.astype(BF16), wd.astype(BF16), g.reshape(1, -1), b.reshape(1, -1))


PROJ_TM = 1024
MASK_LANES = HEAD_PAD - HEAD_DIM


def _proj_kernel(x_ref, w_ref, wvt_ref, bg_ref, u_ref, q_ref, kc_ref, vc_ref, ks_ref, kw_ref,
                 gt_ref, vst_ref, vwt_ref):
    xb = x_ref[0].astype(BF16)

    def mm(lo, n):
        return _dot(xb, w_ref[:, lo:lo + n])

    u_ref[0] = mm(_C_U, POOL_WIDTH)
    q_ref[0] = (mm(_C_Q, Q_PAD) * (QK_SCALE * LOG2_E)).astype(BF16)
    kc_ref[0] = mm(_C_KC, KV_PAD)
    vc_ref[0] = mm(_C_VC, KV_PAD)
    tpos = pl.program_id(1) * PROJ_TM + lax.broadcasted_iota(jnp.int32, (PROJ_TM, KV_PAD), 0)
    lane = lax.broadcasted_iota(jnp.int32, (PROJ_TM, KV_PAD), 1) & (HEAD_PAD - 1)
    hot = lane == HEAD_DIM + ((tpos >> SLC_BLOCK_LOG2) & (MASK_LANES - 1))
    ks_ref[0] = jnp.where(hot, 1.0, mm(_C_KS, KV_PAD)).astype(BF16)
    kw_ref[0] = mm(_C_KW, KV_PAD).astype(BF16)
    gt_ref[0] = jax.nn.sigmoid(_dot_nt(wvt_ref[2 * KV_PAD:3 * KV_PAD, :], xb) + bg_ref[...])
    chan = lax.broadcasted_iota(jnp.int32, (KV_PAD, PROJ_TM), 0) & (HEAD_PAD - 1)
    vst_ref[0] = jnp.where(chan == ONES_ROW, 1.0, _dot_nt(wvt_ref[0:KV_PAD, :], xb)).astype(BF16)
    vwt_ref[0] = jnp.where(chan == ONES_ROW, 1.0,
                           _dot_nt(wvt_ref[KV_PAD:2 * KV_PAD, :], xb)).astype(BF16)


def _pad_heads(w, n_heads):
    lead = w.shape[:-1]
    w = w.reshape(lead + (n_heads, HEAD_DIM))
    w = jnp.pad(w, [(0, 0)] * len(lead) + [(0, 0), (0, HEAD_PAD - HEAD_DIM)])
    return w.reshape(lead + (n_heads * HEAD_PAD,))


def _pad_gate_cols(w):
    lead = w.shape[:-1]
    per_group = HEADS_PER_GROUP * N_BRANCHES
    w = w.reshape(lead + (N_KV_GROUPS, per_group))
    w = jnp.pad(w, [(0, 0)] * len(lead) + [(0, 0), (0, HEAD_PAD - per_group)])
    return w.reshape(lead + (KV_PAD,))


def _proj(x, w_in, b_gate):
    bsz, s, _ = x.shape
    cuts = np.cumsum([POOL_WIDTH, N_Q_HEADS * HEAD_DIM] + [N_KV_GROUPS * HEAD_DIM] * 6)
    parts = jnp.split(w_in, [int(c) for c in cuts], axis=-1)
    cols = [parts[0], _pad_heads(parts[1], N_Q_HEADS)]
    cols += [_pad_heads(parts[i], N_KV_GROUPS) for i in (2, 3, 4, 6)]
    w = jnp.concatenate(cols, axis=-1).astype(BF16)
    wvt = jnp.concatenate([_pad_heads(parts[5], N_KV_GROUPS), _pad_heads(parts[7], N_KV_GROUPS),
                           _pad_gate_cols(parts[8])], axis=-1).T.astype(BF16)
    bg = _pad_gate_cols(b_gate).reshape(KV_PAD, 1)

    def tile(width):
        return pl.BlockSpec((1, PROJ_TM, width), lambda b, i: (b, i, 0))

    def out(width, dtype):
        return jax.ShapeDtypeStruct((bsz, s, width), dtype)

    vt_spec = pl.BlockSpec((1, KV_PAD, PROJ_TM), lambda b, i: (b, 0, i))
    vt_out = jax.ShapeDtypeStruct((bsz, KV_PAD, s), BF16)
    return pl.pallas_call(
        _proj_kernel,
        grid=(bsz, s // PROJ_TM),
        in_specs=[
            tile(D_MODEL),
            pl.BlockSpec((D_MODEL, _C_END), lambda b, i: (0, 0), pipeline_mode=pl.Buffered(1)),
            pl.BlockSpec((3 * KV_PAD, D_MODEL), lambda b, i: (0, 0), pipeline_mode=pl.Buffered(1)),
            pl.BlockSpec((KV_PAD, 1), lambda b, i: (0, 0)),
        ],
        out_specs=[tile(POOL_WIDTH), tile(Q_PAD), tile(KV_PAD), tile(KV_PAD), tile(KV_PAD),
                   tile(KV_PAD), vt_spec, vt_spec, vt_spec],
        out_shape=[out(POOL_WIDTH, F32), out(Q_PAD, BF16), out(KV_PAD, F32), out(KV_PAD, F32),
                   out(KV_PAD, BF16), out(KV_PAD, BF16),
                   jax.ShapeDtypeStruct((bsz, KV_PAD, s), F32), vt_out, vt_out],
        compiler_params=_params("parallel", "parallel"),
        name="in_proj",
    )(x, w, wvt, bg)


def _gelu_tanh(x):
    c = np.float32(np.sqrt(2.0 / np.pi))
    return x * (0.5 * (1.0 + jnp.tanh(c * (x + 0.044715 * (x * x * x)))))


def _compress_kernel(x_ref, plo_ref, phi_ref, wlo_ref, whi_ref, w2_ref, o_ref, a_scr, b_scr,
                     *, channel_major):
    r = pl.program_id(1)
    nsb = x_ref.shape[1]
    xr = x_ref[0]
    a_scr[r] = _dot((xr + plo_ref[...]).astype(BF16), wlo_ref[...])
    b_scr[r, 0:nsb, :] = _dot((xr + phi_ref[...]).astype(BF16), whi_ref[...])

    @pl.when(r == 0)
    def _():
        b_scr[0, nsb:nsb + 8, :] = jnp.zeros((8, KV_PAD), F32)

    @pl.when(r == CMP_PER_SLC - 1)
    def _():
        for rr in range(CMP_PER_SLC):
            if rr < CMP_PER_SLC - 1:
                h = a_scr[rr] + b_scr[rr + 1, 0:nsb, :]
            else:
                h = a_scr[rr] + b_scr[0, 1:nsb + 1, :]
            act = _gelu_tanh(h).astype(BF16)
            if channel_major:
                o_ref[0, :, rr * nsb:(rr + 1) * nsb] = _dot_nt(w2_ref[...], act).astype(BF16)
            else:
                o_ref[0, rr * nsb:(rr + 1) * nsb, :] = _dot(act, w2_ref[...]).astype(BF16)


def _compress_weights(pos, w1, w2):
    eye_g = jnp.eye(N_KV_GROUPS, dtype=F32)
    w1r = w1.reshape(CMP_BLOCK, HEAD_DIM, HEAD_DIM)
    w1r = jnp.pad(w1r, ((0, 0), (0, HEAD_PAD - HEAD_DIM), (0, HEAD_PAD - HEAD_DIM)))
    w1c = w1r[:, None, :, None, :] * eye_g[None, :, None, :, None]
    w1c = w1c.reshape(CMP_BLOCK, KV_PAD, KV_PAD)
    wlo = w1c[:CMP_STRIDE].reshape(CHUNK_FLAT, KV_PAD).astype(BF16)
    whi = w1c[CMP_STRIDE:].reshape(CHUNK_FLAT, KV_PAD).astype(BF16)
    posp = jnp.pad(pos, ((0, 0), (0, HEAD_PAD - HEAD_DIM)))
    posp = jnp.tile(posp[:, None, :], (1, N_KV_GROUPS, 1))
    plo = posp[:CMP_STRIDE].reshape(1, CHUNK_FLAT)
    phi = posp[CMP_STRIDE:].reshape(1, CHUNK_FLAT)
    w2p = jnp.pad(w2, ((0, HEAD_PAD - HEAD_DIM), (0, HEAD_PAD - HEAD_DIM)))
    w2c = (w2p[None, :, None, :] * eye_g[:, None, :, None]).reshape(KV_PAD, KV_PAD).astype(BF16)
    return plo, phi, wlo, whi, w2c


def _compress(kv, pos, w1, w2, channel_major):
    bsz, s, _ = kv.shape
    nsb = s // SLC_BLOCK
    ncp = CMP_PER_SLC * nsb
    plo, phi, wlo, whi, w2c = _compress_weights(pos, w1, w2)
    if channel_major:
        w2c = w2c.T
    out_dims = (KV_PAD, ncp) if channel_major else (ncp, KV_PAD)
    x = kv.reshape(bsz, nsb, CMP_PER_SLC * CHUNK_FLAT)
    const = lambda b, r: (0, 0)
    return pl.pallas_call(
        functools.partial(_compress_kernel, channel_major=channel_major),
        grid=(bsz, CMP_PER_SLC),
        in_specs=[
            pl.BlockSpec((1, nsb, CHUNK_FLAT), lambda b, r: (b, 0, r)),
            pl.BlockSpec((1, CHUNK_FLAT), const),
            pl.BlockSpec((1, CHUNK_FLAT), const),
            pl.BlockSpec((CHUNK_FLAT, KV_PAD), const),
            pl.BlockSpec((CHUNK_FLAT, KV_PAD), const),
            pl.BlockSpec((KV_PAD, KV_PAD), const),
        ],
        out_specs=pl.BlockSpec((1,) + out_dims, lambda b, r: (b, 0, 0)),
        out_shape=jax.ShapeDtypeStruct((bsz,) + out_dims, BF16),
        scratch_shapes=[pltpu.VMEM((CMP_PER_SLC, nsb, KV_PAD), F32),
                        pltpu.VMEM((CMP_PER_SLC, nsb + 8, KV_PAD), F32)],
        compiler_params=_params("parallel", "arbitrary"),
        name="compress",
    )(x, plo, phi, wlo, whi, w2c)


TQ = 1024
SLC_TQ = 512


def _load_q(q_ref):
    return jnp.concatenate(
        [q_ref[0, :, h * HEAD_PAD:(h + 1) * HEAD_PAD] for h in range(HEADS_PER_GROUP)], axis=0)


def _per_head(row):
    return jnp.concatenate([row] * HEADS_PER_GROUP, axis=1)


def _store_gated(o_ref, g_ref, o_t, norm, branch, cols=None):
    cols = slice(None) if cols is None else cols
    gate = jnp.concatenate([g_ref[0, h * N_BRANCHES + branch:h * N_BRANCHES + branch + 1, cols]
                            for h in range(HEADS_PER_GROUP)], axis=1)
    if norm is not None:
        gate = gate / jnp.maximum(norm, 1e-30)
    o = (o_t * gate).astype(o_ref.dtype)
    tq = o.shape[1] // HEADS_PER_GROUP
    for h in range(HEADS_PER_GROUP):
        o_ref[0, h, :, cols] = o[:, h * tq:(h + 1) * tq]


def _attn_specs(tq=TQ):
    q_spec = pl.BlockSpec((1, tq, GROUP_Q), lambda b, g, i: (b, i, g))
    g_spec = pl.BlockSpec((1, HEAD_PAD, tq), lambda b, g, i: (b, g, i))
    o_spec = pl.BlockSpec((1, HEADS_PER_GROUP, HEAD_PAD, tq), lambda b, g, i: (b, g, 0, i))
    return q_spec, g_spec, o_spec


def _branch_out_shape(bsz, s):
    return jax.ShapeDtypeStruct((bsz, N_Q_HEADS, HEAD_PAD, s), BF16)


def _cmp_kernel(q_ref, kc_ref, vct_ref, g_ref, cend_ref, place_ref, o_ref, selq_ref):
    nsb = kc_ref.shape[1] // CMP_PER_SLC
    t0 = pl.program_id(2) * TQ
    half = nsb // 2
    if half % LANES == 0:
        @pl.when(t0 + TQ <= half * SLC_BLOCK)
        def _():
            _cmp_body(q_ref, kc_ref, vct_ref, g_ref, cend_ref, place_ref, o_ref, selq_ref, half)

        @pl.when(t0 + TQ > half * SLC_BLOCK)
        def _():
            _cmp_body(q_ref, kc_ref, vct_ref, g_ref, cend_ref, place_ref, o_ref, selq_ref, nsb)
    else:
        _cmp_body(q_ref, kc_ref, vct_ref, g_ref, cend_ref, place_ref, o_ref, selq_ref, nsb)


def _cmp_body(q_ref, kc_ref, vct_ref, g_ref, cend_ref, place_ref, o_ref, selq_ref, nvis):
    nsb_all = kc_ref.shape[1] // CMP_PER_SLC
    nsb = nvis
    t0 = pl.program_id(2) * TQ

    def slabs(ref_rows):
        if nvis == nsb_all:
            return ref_rows(0, CMP_PER_SLC * nsb_all)
        return jnp.concatenate([ref_rows(r * nsb_all, nvis) for r in range(CMP_PER_SLC)], axis=0)

    kc = slabs(lambda lo, n: kc_ref[0, lo:lo + n, :])
    cend = slabs(lambda lo, n: cend_ref[lo:lo + n, :])
    if nvis == nsb_all:
        vct = vct_ref[0]
    else:
        vct = jnp.concatenate([vct_ref[0, :, r * nsb_all:r * nsb_all + nvis]
                               for r in range(CMP_PER_SLC)], axis=1)

    s = _dot_nt(kc, _load_q(q_ref))
    s = s + _per_head(jnp.where(cend <= t0, 0.0, NEG_BIG))
    m = jnp.max(s, axis=0, keepdims=True)
    e = jnp.exp2(s - m)
    l = jnp.sum(e, axis=0, keepdims=True)
    tcol = t0 + lax.broadcasted_iota(jnp.int32, (1, TQ), 1)
    any_visible = _per_head(jnp.where(tcol >= CMP_BLOCK - 1, 1.0, 0.0))
    p = e * (any_visible / jnp.maximum(l, 1e-30))

    o_t = _dot(vct, p.astype(BF16))
    _store_gated(o_ref, g_ref, o_t, None, 0)

    imp = p[:, 0:TQ]
    for h in range(1, HEADS_PER_GROUP):
        imp = imp + p[:, h * TQ:(h + 1) * TQ]
    p0, p1, p2, p3 = (imp[r * nsb:(r + 1) * nsb, :] for r in range(CMP_PER_SLC))
    blk = lax.broadcasted_iota(jnp.int32, (nsb, TQ), 0)
    p3_prev = jnp.where(blk == 0, 0.0, pltpu.roll(p3, 1, axis=0))
    imp_slc = 0.5 * p3_prev + p0 + p1 + p2 + 0.5 * p3

    jt = (t0 + lax.broadcasted_iota(jnp.int32, (nsb, TQ), 1)) >> SLC_BLOCK_LOG2
    forced = (blk == 0) | (blk == jt) | (blk == jt - 1)
    free = (blk <= jt) & jnp.logical_not(forced)
    score = jnp.where(free, imp_slc, NEG_BIG)

    def pick(_, sc):
        mx = jnp.max(sc, axis=0, keepdims=True)
        first = jnp.min(jnp.where(sc == mx, blk, nsb), axis=0, keepdims=True)
        return jnp.where(blk == first, -jnp.inf, sc)

    picked = lax.fori_loop(0, min(N_SELECT - N_FORCED, nsb), pick, score, unroll=True) == -jnp.inf
    selected = (picked & free) | forced
    bias = jnp.where(selected, 0.0, NEG_BIG)
    if nvis < nsb_all:
        bias = jnp.concatenate([bias, jnp.full((nsb_all - nvis, TQ), NEG_BIG, F32)], axis=0)
    selq_ref[0, 0] = _dot(bias.T.astype(BF16), place_ref[...]).astype(BF16)


def _mask_lane_placement(nsb):
    j = np.arange(nsb)
    place = np.zeros((nsb, (nsb // MASK_LANES) * HEAD_PAD), np.float32)
    place[j, (j // MASK_LANES) * HEAD_PAD + HEAD_DIM + j % MASK_LANES] = 1.0
    return jnp.asarray(place, BF16)


def _cmp_end_minus_token(nsb):
    row = np.arange(CMP_PER_SLC * nsb)
    end = (row % nsb) * SLC_BLOCK + (row // nsb) * CMP_STRIDE + CMP_BLOCK - 1
    return jnp.asarray(end[:, None] - np.arange(TQ)[None, :], jnp.int32)


def _cmp_attention(q, kcmp, vcmp_t, gates):
    bsz, s, _ = q.shape
    nsb = s // SLC_BLOCK
    ncp = CMP_PER_SLC * nsb
    selq_w = (nsb // MASK_LANES) * HEAD_PAD
    q_spec, g_spec, o_spec = _attn_specs()
    const = lambda b, g, i: (0, 0)
    return pl.pallas_call(
        _cmp_kernel,
        grid=(bsz, N_KV_GROUPS, s // TQ),
        in_specs=[q_spec,
                  pl.BlockSpec((1, ncp, HEAD_PAD), lambda b, g, i: (b, 0, g)),
                  pl.BlockSpec((1, HEAD_PAD, ncp), lambda b, g, i: (b, g, 0)),
                  g_spec,
                  pl.BlockSpec((ncp, TQ), const),
                  pl.BlockSpec((nsb, selq_w), const)],
        out_specs=[o_spec, pl.BlockSpec((1, 1, TQ, selq_w), lambda b, g, i: (b, g, i, 0))],
        out_shape=[_branch_out_shape(bsz, s),
                   jax.ShapeDtypeStruct((bsz, N_KV_GROUPS, s, selq_w), BF16)],
        compiler_params=_params("parallel", "parallel", "arbitrary"),
        name="cmp_attn_topk",
    )(q, kcmp, vcmp_t, gates, _cmp_end_minus_token(nsb), _mask_lane_placement(nsb))


SLC_TK = 512
SUPER_KEYS = MASK_LANES * SLC_BLOCK
SLC_STEP = SUPER_KEYS
assert SLC_TQ == SLC_TK
assert SUPER_KEYS % SLC_STEP == 0 and SLC_STEP % SLC_TK == 0


def _slc_kernel(q_ref, k_ref, vt_ref, selq_ref, g_ref, o_ref, s_scr):
    cols = HEADS_PER_GROUP * SLC_TQ
    t0 = pl.multiple_of(pl.program_id(2) * SLC_TQ, SLC_TQ)
    n_full = t0 // SLC_STEP
    tail_tiles = (t0 - n_full * SLC_STEP) // SLC_TK

    def masked_q(slab):
        return jnp.concatenate([q_ref[0, :, h * HEAD_PAD:(h + 1) * HEAD_PAD] + slab
                                for h in range(HEADS_PER_GROUP)], axis=0)

    def accumulate(k0, n_keys, m, m_new, acc):
        acc = jnp.exp2(m - m_new) * acc
        for c in range(0, n_keys, SLC_TK):
            p = jnp.exp2(s_scr[c:c + SLC_TK, :] - m_new).astype(BF16)
            acc = acc + _dot(vt_ref[0, :, pl.ds(k0 + c, SLC_TK)], p)
        return m_new, acc

    def step_q(step):
        lane0 = pl.multiple_of((step * SLC_STEP // SUPER_KEYS) * HEAD_PAD, HEAD_PAD)
        return masked_q(selq_ref[0, 0, :, pl.ds(lane0, HEAD_PAD)])

    def sweep(st, carry):
        m, acc = carry
        k0 = pl.multiple_of(st * SLC_STEP, SLC_STEP)
        sc = _dot_nt(k_ref[0, pl.ds(k0, SLC_STEP), :], step_q(st))
        s_scr[...] = sc
        m_new = jnp.maximum(m, jnp.max(sc, axis=0, keepdims=True))
        return accumulate(k0, SLC_STEP, m, m_new, acc)

    def last_step(n_before, carry):
        m, acc = carry
        k0 = pl.multiple_of(n_full * SLC_STEP, SLC_STEP)
        n_keys = n_before + SLC_TQ
        sc = _dot_nt(k_ref[0, pl.ds(k0, n_keys), :], step_q(n_full))
        kk = lax.broadcasted_iota(jnp.int32, (SLC_TQ, SLC_TQ), 0)
        tt = lax.broadcasted_iota(jnp.int32, (SLC_TQ, SLC_TQ), 1)
        own = sc[n_before:, :] + _per_head(jnp.where(kk <= tt, 0.0, NEG_BIG))
        s_scr[n_before:n_keys, :] = own
        m_new = jnp.maximum(m, jnp.max(own, axis=0, keepdims=True))
        if n_before:
            s_scr[0:n_before, :] = sc[:n_before, :]
            m_new = jnp.maximum(m_new, jnp.max(sc[:n_before, :], axis=0, keepdims=True))
        return accumulate(k0, n_keys, m, m_new, acc)

    carry = (jnp.full((1, cols), NEG_BIG, F32), jnp.zeros((HEAD_PAD, cols), F32))
    carry = lax.fori_loop(0, n_full, sweep, carry)
    _, acc = lax.switch(tail_tiles, [functools.partial(last_step, n * SLC_TK)
                                     for n in range(SLC_STEP // SLC_TK)], carry)
    _store_gated(o_ref, g_ref, acc, acc[ONES_ROW:ONES_ROW + 1, :], 1)


def _slc_attention(q, ks, vs_t, selq, gates):
    bsz, s, _ = q.shape
    n_super = s // SUPER_KEYS
    q_spec, g_spec, o_spec = _attn_specs(SLC_TQ)
    return pl.pallas_call(
        _slc_kernel,
        grid=(bsz, N_KV_GROUPS, s // SLC_TQ),
        in_specs=[q_spec,
                  pl.BlockSpec((1, s, HEAD_PAD), lambda b, g, i: (b, 0, g)),
                  pl.BlockSpec((1, HEAD_PAD, s), lambda b, g, i: (b, g, 0)),
                  pl.BlockSpec((1, 1, SLC_TQ, n_super * HEAD_PAD), lambda b, g, i: (b, g, i, 0)),
                  g_spec],
        out_specs=o_spec,
        out_shape=_branch_out_shape(bsz, s),
        scratch_shapes=[pltpu.VMEM((SLC_STEP, HEADS_PER_GROUP * SLC_TQ), F32)],
        compiler_params=_params("parallel", "parallel", "arbitrary"),
        name="slc_attn",
    )(q, ks, vs_t, selq, gates)


WIN_TQ = 512
WIN_SUB = 256
WIN_KEYS = WINDOW + WIN_SUB


def _win_kernel(q_ref, k_ref, vt_ref, g_ref, o_ref):
    for sub in range(WIN_TQ // WIN_SUB):
        rows = slice(sub * WIN_SUB, (sub + 1) * WIN_SUB)
        t0 = pl.program_id(2) * WIN_TQ + sub * WIN_SUB
        k0 = pl.multiple_of(jnp.maximum(t0 - WINDOW, 0), WIN_SUB)
        q = jnp.concatenate([q_ref[0, rows, h * HEAD_PAD:(h + 1) * HEAD_PAD]
                             for h in range(HEADS_PER_GROUP)], axis=0)
        s = _dot_nt(k_ref[0, pl.ds(k0, WIN_KEYS), :], q)
        diff = ((t0 + lax.broadcasted_iota(jnp.int32, (WIN_KEYS, WIN_SUB), 1))
                - (k0 + lax.broadcasted_iota(jnp.int32, (WIN_KEYS, WIN_SUB), 0)))
        s = s + _per_head(jnp.where((diff >= 0) & (diff < WINDOW), 0.0, NEG_BIG))
        m = jnp.max(s, axis=0, keepdims=True)
        e = jnp.exp2(s - m).astype(BF16)
        o_t = _dot(vt_ref[0, :, pl.ds(k0, WIN_KEYS)], e)
        _store_gated(o_ref, g_ref, o_t, o_t[ONES_ROW:ONES_ROW + 1, :], 2, cols=rows)


def _win_attention(q, kw, vw_t, gates):
    bsz, s, _ = q.shape
    q_spec, g_spec, o_spec = _attn_specs(WIN_TQ)
    return pl.pallas_call(
        _win_kernel,
        grid=(bsz, N_KV_GROUPS, s // WIN_TQ),
        in_specs=[q_spec,
                  pl.BlockSpec((1, s, HEAD_PAD), lambda b, g, i: (b, 0, g)),
                  pl.BlockSpec((1, HEAD_PAD, s), lambda b, g, i: (b, g, 0)),
                  g_spec],
        out_specs=o_spec,
        out_shape=_branch_out_shape(bsz, s),
        compiler_params=_params("parallel", "parallel", "arbitrary"),
        name="win_attn",
    )(q, kw, vw_t, gates)


MIX_TM = 1024
POOL_HALO = 16


def _mix_out_kernel(x_ref, u_ref, halo_ref, oc_ref, os_ref, ow_ref, pw_ref, ps_ref,
                    wo_ref, g_ref, b_ref, o_ref, ext_scr):
    i = pl.program_id(1)
    u = u_ref[0]
    halo = jnp.where(i == 0, 0.0, halo_ref[0])
    ext_scr[0:POOL_HALO, :] = halo
    ext_scr[POOL_HALO:POOL_HALO + MIX_TM, :] = u

    lane = lax.broadcasted_iota(jnp.int32, (MIX_TM, POOL_WIDTH), 1)
    tpos = i * MIX_TM + lax.broadcasted_iota(jnp.int32, (MIX_TM, POOL_WIDTH), 0)
    grp = lane >> (POOL_GROUP_DIM.bit_length() - 1)
    run = u
    win_sum = jnp.zeros_like(u)
    cnt = jnp.zeros_like(u)
    done = 1
    for gidx, w in enumerate(POOL_WINDOWS):
        for kback in range(done, w):
            run = run + ext_scr[POOL_HALO - kback:POOL_HALO - kback + MIX_TM, :]
        done = w
        win_sum = jnp.where(grp == gidx, run, win_sum)
        cnt = jnp.where(grp == gidx, jnp.minimum(tpos + 1, w).astype(F32), cnt)
    pooled = win_sum / cnt - u
    mixed = _dot(pooled.astype(BF16), pw_ref[...]) * ps_ref[...]

    y = _dot(mixed.astype(BF16), wo_ref[0:POOL_WIDTH, :])
    for h in range(0, N_Q_HEADS, 2):
        pair = [(oc_ref[0, hh].astype(F32) + os_ref[0, hh].astype(F32)
                 + ow_ref[0, hh].astype(F32)).T.astype(BF16) for hh in (h, h + 1)]
        lo = POOL_WIDTH + h * HEAD_PAD
        y = y + _dot(jnp.concatenate(pair, axis=1), wo_ref[lo:lo + 2 * HEAD_PAD, :])
    z = ALPHA * x_ref[0] + y
    o_ref[0] = _layer_norm(z, g_ref[...], b_ref[...])


def _mix_out(x, u, o_cmp, o_slc, o_win, pool_w, pool_scale, w_out, g, b):
    bsz, s, _ = x.shape
    n_grp = len(POOL_WINDOWS)
    eye = jnp.eye(n_grp, dtype=F32)
    pw = (pool_w[:, :, None, :] * eye[:, None, :, None]).reshape(POOL_WIDTH, POOL_WIDTH).astype(BF16)
    wo_nsa = w_out[POOL_WIDTH:].reshape(N_Q_HEADS, HEAD_DIM, D_MODEL)
    wo_nsa = jnp.pad(wo_nsa, ((0, 0), (0, HEAD_PAD - HEAD_DIM), (0, 0))).reshape(Q_PAD, D_MODEL)
    wo = jnp.concatenate([w_out[:POOL_WIDTH], wo_nsa], axis=0).astype(BF16)

    def tile(width):
        return pl.BlockSpec((1, MIX_TM, width), lambda bb, i: (bb, i, 0))

    branch = pl.BlockSpec((1, N_Q_HEADS, HEAD_PAD, MIX_TM), lambda bb, i: (bb, 0, 0, i))
    halo_blocks = MIX_TM // POOL_HALO
    const = lambda bb, i: (0, 0)
    return pl.pallas_call(
        _mix_out_kernel,
        grid=(bsz, s // MIX_TM),
        in_specs=[
            tile(D_MODEL), tile(POOL_WIDTH),
            pl.BlockSpec((1, POOL_HALO, POOL_WIDTH),
                         lambda bb, i: (bb, jnp.maximum(i * halo_blocks - 1, 0), 0)),
            branch, branch, branch,
            pl.BlockSpec((POOL_WIDTH, POOL_WIDTH), const),
            pl.BlockSpec((1, POOL_WIDTH), const),
            pl.BlockSpec((POOL_WIDTH + Q_PAD, D_MODEL), const),
            pl.BlockSpec((1, D_MODEL), const),
            pl.BlockSpec((1, D_MODEL), const),
        ],
        out_specs=tile(D_MODEL),
        out_shape=jax.ShapeDtypeStruct((bsz, s, D_MODEL), F32),
        scratch_shapes=[pltpu.VMEM((POOL_HALO + MIX_TM, POOL_WIDTH), F32)],
        compiler_params=_params("parallel", "arbitrary"),
        name="mix_out_ln",
    )(x, u, u, o_cmp, o_slc, o_win, pw, pool_scale.reshape(1, -1), wo,
      g.reshape(1, -1), b.reshape(1, -1))


def kernel(x, ln1_g, ln1_b, ffn1_w_gate, ffn1_w_up, ffn1_w_down, w_in, b_gate, pool_w, pool_scale, cmp_pos_k, cmp_k_w1, cmp_k_w2, cmp_pos_v, cmp_v_w1, cmp_v_w2, w_out, ln2_g, ln2_b, ffn2_w_gate, ffn2_w_up, ffn2_w_down, ln3_g, ln3_b):
    bsz, s, d = x.shape
    assert d == D_MODEL and s % max(SUPER_KEYS, FFN_TM, MIX_TM, PROJ_TM) == 0
    for l in range(DEPTH):
        x = _ffn_ln(x.reshape(bsz * s, d), ffn1_w_gate[l], ffn1_w_up[l], ffn1_w_down[l],
                    ln1_g[l], ln1_b[l]).reshape(bsz, s, d)
        u, q, kc, vc, ks, kw, gates, vs_t, vw_t = _proj(x, w_in[l], b_gate[l])
        kcmp = _compress(kc, cmp_pos_k[l], cmp_k_w1[l], cmp_k_w2[l], channel_major=False)
        vcmp_t = _compress(vc, cmp_pos_v[l], cmp_v_w1[l], cmp_v_w2[l], channel_major=True)
        o_cmp, selq = _cmp_attention(q, kcmp, vcmp_t, gates)
        o_slc = _slc_attention(q, ks, vs_t, selq, gates)
        o_win = _win_attention(q, kw, vw_t, gates)
        x = _mix_out(x, u, o_cmp, o_slc, o_win, pool_w[l], pool_scale[l], w_out[l],
                     ln2_g[l], ln2_b[l])
        x = _ffn_ln(x.reshape(bsz * s, d), ffn2_w_gate[l], ffn2_w_up[l], ffn2_w_down[l],
                    ln3_g[l], ln3_b[l]).reshape(bsz, s, d)
    return x
```

```python
import functools

import numpy as np
import jax
import jax.numpy as jnp
from jax import lax
from jax.experimental import pallas as pl
from jax.experimental.pallas import tpu as pltpu

D_MODEL = 1024
DEPTH = 2
POOL_WIDTH = 256
POOL_WINDOWS = (2, 4, 8, 16)
POOL_GROUP_DIM = 64
N_Q_HEADS = 8
HEAD_DIM = 96
N_KV_GROUPS = 2
HEADS_PER_GROUP = 4
N_BRANCHES = 3
CMP_STRIDE = 16
CMP_BLOCK = 32
SLC_BLOCK = 64
SLC_BLOCK_LOG2 = 6
N_SELECT = 16
N_FORCED = 3
WINDOW = 512
D_FF = 2816
ALPHA = (2.0 * DEPTH) ** 0.25
LN_EPS = 1e-5
NEG_BIG = -1e30
QK_SCALE = HEAD_DIM ** -0.5
LOG2_E = 1.4426950408889634
ONES_ROW = HEAD_DIM

LANES = 128
HEAD_PAD = LANES
Q_PAD = N_Q_HEADS * HEAD_PAD
KV_PAD = N_KV_GROUPS * HEAD_PAD
GROUP_Q = HEADS_PER_GROUP * HEAD_PAD
CMP_PER_SLC = SLC_BLOCK // CMP_STRIDE
CHUNK_FLAT = CMP_STRIDE * KV_PAD

VMEM_LIMIT = 56 * 1024 * 1024

F32 = jnp.float32
BF16 = jnp.bfloat16

_C_U = 0
_C_Q = _C_U + POOL_WIDTH
_C_KC = _C_Q + Q_PAD
_C_VC = _C_KC + KV_PAD
_C_KS = _C_VC + KV_PAD
_C_KW = _C_KS + KV_PAD
_C_END = _C_KW + KV_PAD


def _params(*sem):
    return pltpu.CompilerParams(dimension_semantics=sem, vmem_limit_bytes=VMEM_LIMIT)


def _layer_norm(z, g, b):
    mu = jnp.mean(z, axis=-1, keepdims=True)
    zc = z - mu
    var = jnp.mean(zc * zc, axis=-1, keepdims=True)
    return zc * lax.rsqrt(var + LN_EPS) * g + b


def _dot(a, b):
    return jnp.dot(a, b, preferred_element_type=F32)


def _dot_nt(a, b):
    return lax.dot_general(a, b, (((1,), (1,)), ((), ())), preferred_element_type=F32)


FFN_TM = 1024
MXU_TILE = 256
FFN_CHUNKS = (6 * MXU_TILE, 5 * MXU_TILE)
assert sum(FFN_CHUNKS) == D_FF


def _ffn_ln_kernel(x_ref, wg_ref, wu_ref, wd_ref, g_ref, b_ref, o_ref):
    x = x_ref[...]
    xb = x.astype(BF16)
    acc = None
    lo = 0
    for width in FFN_CHUNKS:
        hg = _dot(xb, wg_ref[:, lo:lo + width])
        hu = _dot(xb, wu_ref[:, lo:lo + width])
        h = (hg * jax.nn.sigmoid(hg)) * hu
        part = _dot(h.astype(BF16), wd_ref[lo:lo + width, :])
        acc = part if acc is None else acc + part
        lo += width
    z = ALPHA * x + 0.5 * acc
    o_ref[...] = _layer_norm(z, g_ref[...], b_ref[...])


def _ffn_ln(x2d, wg, wu, wd, g, b):
    t = x2d.shape[0]
    const = lambda i: (0, 0)
    return pl.pallas_call(
        _ffn_ln_kernel,
        grid=(t // FFN_TM,),
        in_specs=[
            pl.BlockSpec((FFN_TM, D_MODEL), lambda i: (i, 0)),
            pl.BlockSpec((D_MODEL, D_FF), const, pipeline_mode=pl.Buffered(1)),
            pl.BlockSpec((D_MODEL, D_FF), const, pipeline_mode=pl.Buffered(1)),
            pl.BlockSpec((D_FF, D_MODEL), const, pipeline_mode=pl.Buffered(1)),
            pl.BlockSpec((1, D_MODEL), const),
            pl.BlockSpec((1, D_MODEL), const),
        ],
        out_specs=pl.BlockSpec((FFN_TM, D_MODEL), lambda i: (i, 0)),
        out_shape=jax.ShapeDtypeStruct((t, D_MODEL), F32),
        compiler_params=_params("parallel"),
        name="ffn_ln",
    )(x2d, wg.astype(BF16), wu.astype(BF16), wd.astype(BF16), g.reshape(1, -1), b.reshape(1, -1))


PROJ_TM = 1024
MASK_LANES = HEAD_PAD - HEAD_DIM


def _proj_kernel(x_ref, w_ref, wvt_ref, bg_ref, u_ref, q_ref, kc_ref, vc_ref, ks_ref, kw_ref,
                 gt_ref, vst_ref, vwt_ref):
    xb = x_ref[0].astype(BF16)

    def mm(lo, n):
        return _dot(xb, w_ref[:, lo:lo + n])

    u_ref[0] = mm(_C_U, POOL_WIDTH)
    q_ref[0] = (mm(_C_Q, Q_PAD) * (QK_SCALE * LOG2_E)).astype(BF16)
    kc_ref[0] = mm(_C_KC, KV_PAD)
    vc_ref[0] = mm(_C_VC, KV_PAD)
    tpos = pl.program_id(1) * PROJ_TM + lax.broadcasted_iota(jnp.int32, (PROJ_TM, KV_PAD), 0)
    lane = lax.broadcasted_iota(jnp.int32, (PROJ_TM, KV_PAD), 1) & (HEAD_PAD - 1)
    hot = lane == HEAD_DIM + ((tpos >> SLC_BLOCK_LOG2) & (MASK_LANES - 1))
    ks_ref[0] = jnp.where(hot, 1.0, mm(_C_KS, KV_PAD)).astype(BF16)
    kw_ref[0] = mm(_C_KW, KV_PAD).astype(BF16)
    gt_ref[0] = jax.nn.sigmoid(_dot_nt(wvt_ref[2 * KV_PAD:3 * KV_PAD, :], xb) + bg_ref[...])
    chan = lax.broadcasted_iota(jnp.int32, (KV_PAD, PROJ_TM), 0) & (HEAD_PAD - 1)
    vst_ref[0] = jnp.where(chan == ONES_ROW, 1.0, _dot_nt(wvt_ref[0:KV_PAD, :], xb)).astype(BF16)
    vwt_ref[0] = jnp.where(chan == ONES_ROW, 1.0,
                           _dot_nt(wvt_ref[KV_PAD:2 * KV_PAD, :], xb)).astype(BF16)


def _pad_heads(w, n_heads):
    lead = w.shape[:-1]
    w = w.reshape(lead + (n_heads, HEAD_DIM))
    w = jnp.pad(w, [(0, 0)] * len(lead) + [(0, 0), (0, HEAD_PAD - HEAD_DIM)])
    return w.reshape(lead + (n_heads * HEAD_PAD,))


def _pad_gate_cols(w):
    lead = w.shape[:-1]
    per_group = HEADS_PER_GROUP * N_BRANCHES
    w = w.reshape(lead + (N_KV_GROUPS, per_group))
    w = jnp.pad(w, [(0, 0)] * len(lead) + [(0, 0), (0, HEAD_PAD - per_group)])
    return w.reshape(lead + (KV_PAD,))


def _proj(x, w_in, b_gate):
    bsz, s, _ = x.shape
    cuts = np.cumsum([POOL_WIDTH, N_Q_HEADS * HEAD_DIM] + [N_KV_GROUPS * HEAD_DIM] * 6)
    parts = jnp.split(w_in, [int(c) for c in cuts], axis=-1)
    cols = [parts[0], _pad_heads(parts[1], N_Q_HEADS)]
    cols += [_pad_heads(parts[i], N_KV_GROUPS) for i in (2, 3, 4, 6)]
    w = jnp.concatenate(cols, axis=-1).astype(BF16)
    wvt = jnp.concatenate([_pad_heads(parts[5], N_KV_GROUPS), _pad_heads(parts[7], N_KV_GROUPS),
                           _pad_gate_cols(parts[8])], axis=-1).T.astype(BF16)
    bg = _pad_gate_cols(b_gate).reshape(KV_PAD, 1)

    def tile(width):
        return pl.BlockSpec((1, PROJ_TM, width), lambda b, i: (b, i, 0))

    def out(width, dtype):
        return jax.ShapeDtypeStruct((bsz, s, width), dtype)

    vt_spec = pl.BlockSpec((1, KV_PAD, PROJ_TM), lambda b, i: (b, 0, i))
    vt_out = jax.ShapeDtypeStruct((bsz, KV_PAD, s), BF16)
    return pl.pallas_call(
        _proj_kernel,
        grid=(bsz, s // PROJ_TM),
        in_specs=[
            tile(D_MODEL),
            pl.BlockSpec((D_MODEL, _C_END), lambda b, i: (0, 0), pipeline_mode=pl.Buffered(1)),
            pl.BlockSpec((3 * KV_PAD, D_MODEL), lambda b, i: (0, 0), pipeline_mode=pl.Buffered(1)),
            pl.BlockSpec((KV_PAD, 1), lambda b, i: (0, 0)),
        ],
        out_specs=[tile(POOL_WIDTH), tile(Q_PAD), tile(KV_PAD), tile(KV_PAD), tile(KV_PAD),
                   tile(KV_PAD), vt_spec, vt_spec, vt_spec],
        out_shape=[out(POOL_WIDTH, F32), out(Q_PAD, BF16), out(KV_PAD, F32), out(KV_PAD, F32),
                   out(KV_PAD, BF16), out(KV_PAD, BF16),
                   jax.ShapeDtypeStruct((bsz, KV_PAD, s), F32), vt_out, vt_out],
        compiler_params=_params("parallel", "parallel"),
        name="in_proj",
    )(x, w, wvt, bg)


def _gelu_tanh(x):
    c = np.float32(np.sqrt(2.0 / np.pi))
    return x * (0.5 * (1.0 + jnp.tanh(c * (x + 0.044715 * (x * x * x)))))


def _compress_kernel(x_ref, plo_ref, phi_ref, wlo_ref, whi_ref, w2_ref, o_ref, a_scr, b_scr,
                     *, channel_major):
    r = pl.program_id(1)
    nsb = x_ref.shape[1]
    xr = x_ref[0]
    a_scr[r] = _dot((xr + plo_ref[...]).astype(BF16), wlo_ref[...])
    b_scr[r, 0:nsb, :] = _dot((xr + phi_ref[...]).astype(BF16), whi_ref[...])

    @pl.when(r == 0)
    def _():
        b_scr[0, nsb:nsb + 8, :] = jnp.zeros((8, KV_PAD), F32)

    @pl.when(r == CMP_PER_SLC - 1)
    def _():
        for rr in range(CMP_PER_SLC):
            if rr < CMP_PER_SLC - 1:
                h = a_scr[rr] + b_scr[rr + 1, 0:nsb, :]
            else:
                h = a_scr[rr] + b_scr[0, 1:nsb + 1, :]
            act = _gelu_tanh(h).astype(BF16)
            if channel_major:
                o_ref[0, :, rr * nsb:(rr + 1) * nsb] = _dot_nt(w2_ref[...], act).astype(BF16)
            else:
                o_ref[0, rr * nsb:(rr + 1) * nsb, :] = _dot(act, w2_ref[...]).astype(BF16)


def _compress_weights(pos, w1, w2):
    eye_g = jnp.eye(N_KV_GROUPS, dtype=F32)
    w1r = w1.reshape(CMP_BLOCK, HEAD_DIM, HEAD_DIM)
    w1r = jnp.pad(w1r, ((0, 0), (0, HEAD_PAD - HEAD_DIM), (0, HEAD_PAD - HEAD_DIM)))
    w1c = w1r[:, None, :, None, :] * eye_g[None, :, None, :, None]
    w1c = w1c.reshape(CMP_BLOCK, KV_PAD, KV_PAD)
    wlo = w1c[:CMP_STRIDE].reshape(CHUNK_FLAT, KV_PAD).astype(BF16)
    whi = w1c[CMP_STRIDE:].reshape(CHUNK_FLAT, KV_PAD).astype(BF16)
    posp = jnp.pad(pos, ((0, 0), (0, HEAD_PAD - HEAD_DIM)))
    posp = jnp.tile(posp[:, None, :], (1, N_KV_GROUPS, 1))
    plo = posp[:CMP_STRIDE].reshape(1, CHUNK_FLAT)
    phi = posp[CMP_STRIDE:].reshape(1, CHUNK_FLAT)
    w2p = jnp.pad(w2, ((0, HEAD_PAD - HEAD_DIM), (0, HEAD_PAD - HEAD_DIM)))
    w2c = (w2p[None, :, None, :] * eye_g[:, None, :, None]).reshape(KV_PAD, KV_PAD).astype(BF16)
    return plo, phi, wlo, whi, w2c


def _compress(kv, pos, w1, w2, channel_major):
    bsz, s, _ = kv.shape
    nsb = s // SLC_BLOCK
    ncp = CMP_PER_SLC * nsb
    plo, phi, wlo, whi, w2c = _compress_weights(pos, w1, w2)
    if channel_major:
        w2c = w2c.T
    out_dims = (KV_PAD, ncp) if channel_major else (ncp, KV_PAD)
    x = kv.reshape(bsz, nsb, CMP_PER_SLC * CHUNK_FLAT)
    const = lambda b, r: (0, 0)
    return pl.pallas_call(
        functools.partial(_compress_kernel, channel_major=channel_major),
        grid=(bsz, CMP_PER_SLC),
        in_specs=[
            pl.BlockSpec((1, nsb, CHUNK_FLAT), lambda b, r: (b, 0, r)),
            pl.BlockSpec((1, CHUNK_FLAT), const),
            pl.BlockSpec((1, CHUNK_FLAT), const),
            pl.BlockSpec((CHUNK_FLAT, KV_PAD), const),
            pl.BlockSpec((CHUNK_FLAT, KV_PAD), const),
            pl.BlockSpec((KV_PAD, KV_PAD), const),
        ],
        out_specs=pl.BlockSpec((1,) + out_dims, lambda b, r: (b, 0, 0)),
        out_shape=jax.ShapeDtypeStruct((bsz,) + out_dims, BF16),
        scratch_shapes=[pltpu.VMEM((CMP_PER_SLC, nsb, KV_PAD), F32),
                        pltpu.VMEM((CMP_PER_SLC, nsb + 8, KV_PAD), F32)],
        compiler_params=_params("parallel", "arbitrary"),
        name="compress",
    )(x, plo, phi, wlo, whi, w2c)


TQ = 1024
SLC_TQ = 512


def _load_q(q_ref):
    return jnp.concatenate(
        [q_ref[0, :, h * HEAD_PAD:(h + 1) * HEAD_PAD] for h in range(HEADS_PER_GROUP)], axis=0)


def _per_head(row):
    return jnp.concatenate([row] * HEADS_PER_GROUP, axis=1)


def _store_gated(o_ref, g_ref, o_t, norm, branch, cols=None):
    cols = slice(None) if cols is None else cols
    gate = jnp.concatenate([g_ref[0, h * N_BRANCHES + branch:h * N_BRANCHES + branch + 1, cols]
                            for h in range(HEADS_PER_GROUP)], axis=1)
    if norm is not None:
        gate = gate / jnp.maximum(norm, 1e-30)
    o = (o_t * gate).astype(o_ref.dtype)
    tq = o.shape[1] // HEADS_PER_GROUP
    for h in range(HEADS_PER_GROUP):
        o_ref[0, h, :, cols] = o[:, h * tq:(h + 1) * tq]


def _attn_specs(tq=TQ):
    q_spec = pl.BlockSpec((1, tq, GROUP_Q), lambda b, g, i: (b, i, g))
    g_spec = pl.BlockSpec((1, HEAD_PAD, tq), lambda b, g, i: (b, g, i))
    o_spec = pl.BlockSpec((1, HEADS_PER_GROUP, HEAD_PAD, tq), lambda b, g, i: (b, g, 0, i))
    return q_spec, g_spec, o_spec


def _branch_out_shape(bsz, s):
    return jax.ShapeDtypeStruct((bsz, N_Q_HEADS, HEAD_PAD, s), BF16)


def _cmp_kernel(q_ref, kc_ref, vct_ref, g_ref, cend_ref, place_ref, o_ref, selq_ref):
    nsb = kc_ref.shape[1] // CMP_PER_SLC
    t0 = pl.program_id(2) * TQ
    half = nsb // 2
    if half % LANES == 0:
        @pl.when(t0 + TQ <= half * SLC_BLOCK)
        def _():
            _cmp_body(q_ref, kc_ref, vct_ref, g_ref, cend_ref, place_ref, o_ref, selq_ref, half)

        @pl.when(t0 + TQ > half * SLC_BLOCK)
        def _():
            _cmp_body(q_ref, kc_ref, vct_ref, g_ref, cend_ref, place_ref, o_ref, selq_ref, nsb)
    else:
        _cmp_body(q_ref, kc_ref, vct_ref, g_ref, cend_ref, place_ref, o_ref, selq_ref, nsb)


def _cmp_body(q_ref, kc_ref, vct_ref, g_ref, cend_ref, place_ref, o_ref, selq_ref, nvis):
    nsb_all = kc_ref.shape[1] // CMP_PER_SLC
    nsb = nvis
    t0 = pl.program_id(2) * TQ

    def slabs(ref_rows):
        if nvis == nsb_all:
            return ref_rows(0, CMP_PER_SLC * nsb_all)
        return jnp.concatenate([ref_rows(r * nsb_all, nvis) for r in range(CMP_PER_SLC)], axis=0)

    kc = slabs(lambda lo, n: kc_ref[0, lo:lo + n, :])
    cend = slabs(lambda lo, n: cend_ref[lo:lo + n, :])
    if nvis == nsb_all:
        vct = vct_ref[0]
    else:
        vct = jnp.concatenate([vct_ref[0, :, r * nsb_all:r * nsb_all + nvis]
                               for r in range(CMP_PER_SLC)], axis=1)

    s = _dot_nt(kc, _load_q(q_ref))
    s = s + _per_head(jnp.where(cend <= t0, 0.0, NEG_BIG))
    m = jnp.max(s, axis=0, keepdims=True)
    e = jnp.exp2(s - m)
    l = jnp.sum(e, axis=0, keepdims=True)
    tcol = t0 + lax.broadcasted_iota(jnp.int32, (1, TQ), 1)
    any_visible = _per_head(jnp.where(tcol >= CMP_BLOCK - 1, 1.0, 0.0))
    p = e * (any_visible / jnp.maximum(l, 1e-30))

    o_t = _dot(vct, p.astype(BF16))
    _store_gated(o_ref, g_ref, o_t, None, 0)

    imp = p[:, 0:TQ]
    for h in range(1, HEADS_PER_GROUP):
        imp = imp + p[:, h * TQ:(h + 1) * TQ]
    p0, p1, p2, p3 = (imp[r * nsb:(r + 1) * nsb, :] for r in range(CMP_PER_SLC))
    blk = lax.broadcasted_iota(jnp.int32, (nsb, TQ), 0)
    p3_prev = jnp.where(blk == 0, 0.0, pltpu.roll(p3, 1, axis=0))
    imp_slc = 0.5 * p3_prev + p0 + p1 + p2 + 0.5 * p3

    jt = (t0 + lax.broadcasted_iota(jnp.int32, (nsb, TQ), 1)) >> SLC_BLOCK_LOG2
    forced = (blk == 0) | (blk == jt) | (blk == jt - 1)
    free = (blk <= jt) & jnp.logical_not(forced)
    score = jnp.where(free, imp_slc, NEG_BIG)

    def pick(_, sc):
        mx = jnp.max(sc, axis=0, keepdims=True)
        first = jnp.min(jnp.where(sc == mx, blk, nsb), axis=0, keepdims=True)
        return jnp.where(blk == first, -jnp.inf, sc)

    picked = lax.fori_loop(0, min(N_SELECT - N_FORCED, nsb), pick, score, unroll=True) == -jnp.inf
    selected = (picked & free) | forced
    bias = jnp.where(selected, 0.0, NEG_BIG)
    if nvis < nsb_all:
        bias = jnp.concatenate([bias, jnp.full((nsb_all - nvis, TQ), NEG_BIG, F32)], axis=0)
    selq_ref[0, 0] = _dot(bias.T.astype(BF16), place_ref[...]).astype(BF16)


def _mask_lane_placement(nsb):
    j = np.arange(nsb)
    place = np.zeros((nsb, (nsb // MASK_LANES) * HEAD_PAD), np.float32)
    place[j, (j // MASK_LANES) * HEAD_PAD + HEAD_DIM + j % MASK_LANES] = 1.0
    return jnp.asarray(place, BF16)


def _cmp_end_minus_token(nsb):
    row = np.arange(CMP_PER_SLC * nsb)
    end = (row % nsb) * SLC_BLOCK + (row // nsb) * CMP_STRIDE + CMP_BLOCK - 1
    return jnp.asarray(end[:, None] - np.arange(TQ)[None, :], jnp.int32)


def _cmp_attention(q, kcmp, vcmp_t, gates):
    bsz, s, _ = q.shape
    nsb = s // SLC_BLOCK
    ncp = CMP_PER_SLC * nsb
    selq_w = (nsb // MASK_LANES) * HEAD_PAD
    q_spec, g_spec, o_spec = _attn_specs()
    const = lambda b, g, i: (0, 0)
    return pl.pallas_call(
        _cmp_kernel,
        grid=(bsz, N_KV_GROUPS, s // TQ),
        in_specs=[q_spec,
                  pl.BlockSpec((1, ncp, HEAD_PAD), lambda b, g, i: (b, 0, g)),
                  pl.BlockSpec((1, HEAD_PAD, ncp), lambda b, g, i: (b, g, 0)),
                  g_spec,
                  pl.BlockSpec((ncp, TQ), const),
                  pl.BlockSpec((nsb, selq_w), const)],
        out_specs=[o_spec, pl.BlockSpec((1, 1, TQ, selq_w), lambda b, g, i: (b, g, i, 0))],
        out_shape=[_branch_out_shape(bsz, s),
                   jax.ShapeDtypeStruct((bsz, N_KV_GROUPS, s, selq_w), BF16)],
        compiler_params=_params("parallel", "parallel", "arbitrary"),
        name="cmp_attn_topk",
    )(q, kcmp, vcmp_t, gates, _cmp_end_minus_token(nsb), _mask_lane_placement(nsb))


SLC_TK = 512
SUPER_KEYS = MASK_LANES * SLC_BLOCK
SLC_STEP = SUPER_KEYS
assert SLC_TQ == SLC_TK
assert SUPER_KEYS % SLC_STEP == 0 and SLC_STEP % SLC_TK == 0


def _slc_kernel(q_ref, k_ref, vt_ref, selq_ref, g_ref, o_ref, s_scr):
    cols = HEADS_PER_GROUP * SLC_TQ
    t0 = pl.multiple_of(pl.program_id(2) * SLC_TQ, SLC_TQ)
    n_full = t0 // SLC_STEP
    tail_tiles = (t0 - n_full * SLC_STEP) // SLC_TK

    def masked_q(slab):
        return jnp.concatenate([q_ref[0, :, h * HEAD_PAD:(h + 1) * HEAD_PAD] + slab
                                for h in range(HEADS_PER_GROUP)], axis=0)

    def accumulate(k0, n_keys, m, m_new, acc):
        acc = jnp.exp2(m - m_new) * acc
        for c in range(0, n_keys, SLC_TK):
            p = jnp.exp2(s_scr[c:c + SLC_TK, :] - m_new).astype(BF16)
            acc = acc + _dot(vt_ref[0, :, pl.ds(k0 + c, SLC_TK)], p)
        return m_new, acc

    def step_q(step):
        lane0 = pl.multiple_of((step * SLC_STEP // SUPER_KEYS) * HEAD_PAD, HEAD_PAD)
        return masked_q(selq_ref[0, 0, :, pl.ds(lane0, HEAD_PAD)])

    def sweep(st, carry):
        m, acc = carry
        k0 = pl.multiple_of(st * SLC_STEP, SLC_STEP)
        sc = _dot_nt(k_ref[0, pl.ds(k0, SLC_STEP), :], step_q(st))
        s_scr[...] = sc
        m_new = jnp.maximum(m, jnp.max(sc, axis=0, keepdims=True))
        return accumulate(k0, SLC_STEP, m, m_new, acc)

    def last_step(n_before, carry):
        m, acc = carry
        k0 = pl.multiple_of(n_full * SLC_STEP, SLC_STEP)
        n_keys = n_before + SLC_TQ
        sc = _dot_nt(k_ref[0, pl.ds(k0, n_keys), :], step_q(n_full))
        kk = lax.broadcasted_iota(jnp.int32, (SLC_TQ, SLC_TQ), 0)
        tt = lax.broadcasted_iota(jnp.int32, (SLC_TQ, SLC_TQ), 1)
        own = sc[n_before:, :] + _per_head(jnp.where(kk <= tt, 0.0, NEG_BIG))
        s_scr[n_before:n_keys, :] = own
        m_new = jnp.maximum(m, jnp.max(own, axis=0, keepdims=True))
        if n_before:
            s_scr[0:n_before, :] = sc[:n_before, :]
            m_new = jnp.maximum(m_new, jnp.max(sc[:n_before, :], axis=0, keepdims=True))
        return accumulate(k0, n_keys, m, m_new, acc)

    carry = (jnp.full((1, cols), NEG_BIG, F32), jnp.zeros((HEAD_PAD, cols), F32))
    carry = lax.fori_loop(0, n_full, sweep, carry)
    _, acc = lax.switch(tail_tiles, [functools.partial(last_step, n * SLC_TK)
                                     for n in range(SLC_STEP // SLC_TK)], carry)
    _store_gated(o_ref, g_ref, acc, acc[ONES_ROW:ONES_ROW + 1, :], 1)


def _slc_attention(q, ks, vs_t, selq, gates):
    bsz, s, _ = q.shape
    n_super = s // SUPER_KEYS
    q_spec, g_spec, o_spec = _attn_specs(SLC_TQ)
    return pl.pallas_call(
        _slc_kernel,
        grid=(bsz, N_KV_GROUPS, s // SLC_TQ),
        in_specs=[q_spec,
                  pl.BlockSpec((1, s, HEAD_PAD), lambda b, g, i: (b, 0, g)),
                  pl.BlockSpec((1, HEAD_PAD, s), lambda b, g, i: (b, g, 0)),
                  pl.BlockSpec((1, 1, SLC_TQ, n_super * HEAD_PAD), lambda b, g, i: (b, g, i, 0)),
                  g_spec],
        out_specs=o_spec,
        out_shape=_branch_out_shape(bsz, s),
        scratch_shapes=[pltpu.VMEM((SLC_STEP, HEADS_PER_GROUP * SLC_TQ), F32)],
        compiler_params=_params("parallel", "parallel", "arbitrary"),
        name="slc_attn",
    )(q, ks, vs_t, selq, gates)


WIN_TQ = 1024
WIN_SUB = 256
WIN_KEYS = WINDOW + WIN_SUB


def _win_kernel(q_ref, k_ref, vt_ref, g_ref, o_ref):
    for sub in range(WIN_TQ // WIN_SUB):
        rows = slice(sub * WIN_SUB, (sub + 1) * WIN_SUB)
        t0 = pl.program_id(2) * WIN_TQ + sub * WIN_SUB
        k0 = pl.multiple_of(jnp.maximum(t0 - WINDOW, 0), WIN_SUB)
        q = jnp.concatenate([q_ref[0, rows, h * HEAD_PAD:(h + 1) * HEAD_PAD]
                             for h in range(HEADS_PER_GROUP)], axis=0)
        s = _dot_nt(k_ref[0, pl.ds(k0, WIN_KEYS), :], q)
        diff = ((t0 + lax.broadcasted_iota(jnp.int32, (WIN_KEYS, WIN_SUB), 1))
                - (k0 + lax.broadcasted_iota(jnp.int32, (WIN_KEYS, WIN_SUB), 0)))
        s = s + _per_head(jnp.where((diff >= 0) & (diff < WINDOW), 0.0, NEG_BIG))
        m = jnp.max(s, axis=0, keepdims=True)
        e = jnp.exp2(s - m).astype(BF16)
        o_t = _dot(vt_ref[0, :, pl.ds(k0, WIN_KEYS)], e)
        _store_gated(o_ref, g_ref, o_t, o_t[ONES_ROW:ONES_ROW + 1, :], 2, cols=rows)


def _win_attention(q, kw, vw_t, gates):
    bsz, s, _ = q.shape
    q_spec, g_spec, o_spec = _attn_specs(WIN_TQ)
    return pl.pallas_call(
        _win_kernel,
        grid=(bsz, N_KV_GROUPS, s // WIN_TQ),
        in_specs=[q_spec,
                  pl.BlockSpec((1, s, HEAD_PAD), lambda b, g, i: (b, 0, g)),
                  pl.BlockSpec((1, HEAD_PAD, s), lambda b, g, i: (b, g, 0)),
                  g_spec],
        out_specs=o_spec,
        out_shape=_branch_out_shape(bsz, s),
        compiler_params=_params("parallel", "parallel", "arbitrary"),
        name="win_attn",
    )(q, kw, vw_t, gates)


MIX_TM = 1024
POOL_HALO = 16


def _mix_out_kernel(x_ref, u_ref, halo_ref, oc_ref, os_ref, ow_ref, pw_ref, ps_ref,
                    wo_ref, g_ref, b_ref, o_ref, ext_scr):
    i = pl.program_id(1)
    u = u_ref[0]
    halo = jnp.where(i == 0, 0.0, halo_ref[0])
    ext_scr[0:POOL_HALO, :] = halo
    ext_scr[POOL_HALO:POOL_HALO + MIX_TM, :] = u

    lane = lax.broadcasted_iota(jnp.int32, (MIX_TM, POOL_WIDTH), 1)
    tpos = i * MIX_TM + lax.broadcasted_iota(jnp.int32, (MIX_TM, POOL_WIDTH), 0)
    grp = lane >> (POOL_GROUP_DIM.bit_length() - 1)
    run = u
    win_sum = jnp.zeros_like(u)
    cnt = jnp.zeros_like(u)
    done = 1
    for gidx, w in enumerate(POOL_WINDOWS):
        for kback in range(done, w):
            run = run + ext_scr[POOL_HALO - kback:POOL_HALO - kback + MIX_TM, :]
        done = w
        win_sum = jnp.where(grp == gidx, run, win_sum)
        cnt = jnp.where(grp == gidx, jnp.minimum(tpos + 1, w).astype(F32), cnt)
    pooled = win_sum / cnt - u
    mixed = _dot(pooled.astype(BF16), pw_ref[...]) * ps_ref[...]

    y = _dot(mixed.astype(BF16), wo_ref[0:POOL_WIDTH, :])
    for h in range(0, N_Q_HEADS, 2):
        pair = [(oc_ref[0, hh].astype(F32) + os_ref[0, hh].astype(F32)
                 + ow_ref[0, hh].astype(F32)).T.astype(BF16) for hh in (h, h + 1)]
        lo = POOL_WIDTH + h * HEAD_PAD
        y = y + _dot(jnp.concatenate(pair, axis=1), wo_ref[lo:lo + 2 * HEAD_PAD, :])
    z = ALPHA * x_ref[0] + y
    o_ref[0] = _layer_norm(z, g_ref[...], b_ref[...])


def _mix_out(x, u, o_cmp, o_slc, o_win, pool_w, pool_scale, w_out, g, b):
    bsz, s, _ = x.shape
    n_grp = len(POOL_WINDOWS)
    eye = jnp.eye(n_grp, dtype=F32)
    pw = (pool_w[:, :, None, :] * eye[:, None, :, None]).reshape(POOL_WIDTH, POOL_WIDTH).astype(BF16)
    wo_nsa = w_out[POOL_WIDTH:].reshape(N_Q_HEADS, HEAD_DIM, D_MODEL)
    wo_nsa = jnp.pad(wo_nsa, ((0, 0), (0, HEAD_PAD - HEAD_DIM), (0, 0))).reshape(Q_PAD, D_MODEL)
    wo = jnp.concatenate([w_out[:POOL_WIDTH], wo_nsa], axis=0).astype(BF16)

    def tile(width):
        return pl.BlockSpec((1, MIX_TM, width), lambda bb, i: (bb, i, 0))

    branch = pl.BlockSpec((1, N_Q_HEADS, HEAD_PAD, MIX_TM), lambda bb, i: (bb, 0, 0, i))
    halo_blocks = MIX_TM // POOL_HALO
    const = lambda bb, i: (0, 0)
    return pl.pallas_call(
        _mix_out_kernel,
        grid=(bsz, s // MIX_TM),
        in_specs=[
            tile(D_MODEL), tile(POOL_WIDTH),
            pl.BlockSpec((1, POOL_HALO, POOL_WIDTH),
                         lambda bb, i: (bb, jnp.maximum(i * halo_blocks - 1, 0), 0)),
            branch, branch, branch,
            pl.BlockSpec((POOL_WIDTH, POOL_WIDTH), const),
            pl.BlockSpec((1, POOL_WIDTH), const),
            pl.BlockSpec((POOL_WIDTH + Q_PAD, D_MODEL), const),
            pl.BlockSpec((1, D_MODEL), const),
            pl.BlockSpec((1, D_MODEL), const),
        ],
        out_specs=tile(D_MODEL),
        out_shape=jax.ShapeDtypeStruct((bsz, s, D_MODEL), F32),
        scratch_shapes=[pltpu.VMEM((POOL_HALO + MIX_TM, POOL_WIDTH), F32)],
        compiler_params=_params("parallel", "arbitrary"),
        name="mix_out_ln",
    )(x, u, u, o_cmp, o_slc, o_win, pw, pool_scale.reshape(1, -1), wo,
      g.reshape(1, -1), b.reshape(1, -1))


def kernel(x, ln1_g, ln1_b, ffn1_w_gate, ffn1_w_up, ffn1_w_down, w_in, b_gate, pool_w, pool_scale, cmp_pos_k, cmp_k_w1, cmp_k_w2, cmp_pos_v, cmp_v_w1, cmp_v_w2, w_out, ln2_g, ln2_b, ffn2_w_gate, ffn2_w_up, ffn2_w_down, ln3_g, ln3_b):
    bsz, s, d = x.shape
    assert d == D_MODEL and s % max(SUPER_KEYS, FFN_TM, MIX_TM, PROJ_TM) == 0
    for l in range(DEPTH):
        x = _ffn_ln(x.reshape(bsz * s, d), ffn1_w_gate[l], ffn1_w_up[l], ffn1_w_down[l],
                    ln1_g[l], ln1_b[l]).reshape(bsz, s, d)
        u, q, kc, vc, ks, kw, gates, vs_t, vw_t = _proj(x, w_in[l], b_gate[l])
        kcmp = _compress(kc, cmp_pos_k[l], cmp_k_w1[l], cmp_k_w2[l], channel_major=False)
        vcmp_t = _compress(vc, cmp_pos_v[l], cmp_v_w1[l], cmp_v_w2[l], channel_major=True)
        o_cmp, selq = _cmp_attention(q, kcmp, vcmp_t, gates)
        o_slc = _slc_attention(q, ks, vs_t, selq, gates)
        o_win = _win_attention(q, kw, vw_t, gates)
        x = _mix_out(x, u, o_cmp, o_slc, o_win, pool_w[l], pool_scale[l], w_out[l],
                     ln2_g[l], ln2_b[l])
        x = _ffn_ln(x.reshape(bsz * s, d), ffn2_w_gate[l], ffn2_w_up[l], ffn2_w_down[l],
                    ln3_g[l], ln3_b[l]).reshape(bsz, s, d)
    return x
```

```python
import functools

import numpy as np
import jax
import jax.numpy as jnp
from jax import lax
from jax.experimental import pallas as pl
from jax.experimental.pallas import tpu as pltpu

D_MODEL = 1024
DEPTH = 2
POOL_WIDTH = 256
POOL_WINDOWS = (2, 4, 8, 16)
POOL_GROUP_DIM = 64
N_Q_HEADS = 8
HEAD_DIM = 96
N_KV_GROUPS = 2
HEADS_PER_GROUP = 4
N_BRANCHES = 3
CMP_STRIDE = 16
CMP_BLOCK = 32
SLC_BLOCK = 64
SLC_BLOCK_LOG2 = 6
N_SELECT = 16
N_FORCED = 3
WINDOW = 512
D_FF = 2816
ALPHA = (2.0 * DEPTH) ** 0.25
LN_EPS = 1e-5
NEG_BIG = -1e30
QK_SCALE = HEAD_DIM ** -0.5
LOG2_E = 1.4426950408889634
ONES_ROW = HEAD_DIM

LANES = 128
HEAD_PAD = LANES
Q_PAD = N_Q_HEADS * HEAD_PAD
KV_PAD = N_KV_GROUPS * HEAD_PAD
GROUP_Q = HEADS_PER_GROUP * HEAD_PAD
CMP_PER_SLC = SLC_BLOCK // CMP_STRIDE
CHUNK_FLAT = CMP_STRIDE * KV_PAD

VMEM_LIMIT = 56 * 1024 * 1024

F32 = jnp.float32
BF16 = jnp.bfloat16

_C_U = 0
_C_Q = _C_U + POOL_WIDTH
_C_KC = _C_Q + Q_PAD
_C_VC = _C_KC + KV_PAD
_C_KS = _C_VC + KV_PAD
_C_KW = _C_KS + KV_PAD
_C_END = _C_KW + KV_PAD


def _params(*sem):
    return pltpu.CompilerParams(dimension_semantics=sem, vmem_limit_bytes=VMEM_LIMIT)


def _layer_norm(z, g, b):
    mu = jnp.mean(z, axis=-1, keepdims=True)
    zc = z - mu
    var = jnp.mean(zc * zc, axis=-1, keepdims=True)
    return zc * lax.rsqrt(var + LN_EPS) * g + b


def _dot(a, b):
    return jnp.dot(a, b, preferred_element_type=F32)


def _dot_nt(a, b):
    return lax.dot_general(a, b, (((1,), (1,)), ((), ())), preferred_element_type=F32)


FFN_TM = 1024
MXU_TILE = 256
FFN_CHUNKS = (6 * MXU_TILE, 5 * MXU_TILE)
assert sum(FFN_CHUNKS) == D_FF


def _ffn_ln_kernel(x_ref, wg_ref, wu_ref, wd_ref, g_ref, b_ref, o_ref):
    x = x_ref[...]
    xb = x.astype(BF16)
    acc = None
    lo = 0
    for width in FFN_CHUNKS:
        hg = _dot(xb, wg_ref[:, lo:lo + width])
        hu = _dot(xb, wu_ref[:, lo:lo + width])
        h = (hg * jax.nn.sigmoid(hg)) * hu
        part = _dot(h.astype(BF16), wd_ref[lo:lo + width, :])
        acc = part if acc is None else acc + part
        lo += width
    z = ALPHA * x + 0.5 * acc
    o_ref[...] = _layer_norm(z, g_ref[...], b_ref[...])


def _ffn_ln(x2d, wg, wu, wd, g, b):
    t = x2d.shape[0]
    const = lambda i: (0, 0)
    return pl.pallas_call(
        _ffn_ln_kernel,
        grid=(t // FFN_TM,),
        in_specs=[
            pl.BlockSpec((FFN_TM, D_MODEL), lambda i: (i, 0)),
            pl.BlockSpec((D_MODEL, D_FF), const, pipeline_mode=pl.Buffered(1)),
            pl.BlockSpec((D_MODEL, D_FF), const, pipeline_mode=pl.Buffered(1)),
            pl.BlockSpec((D_FF, D_MODEL), const, pipeline_mode=pl.Buffered(1)),
            pl.BlockSpec((1, D_MODEL), const),
            pl.BlockSpec((1, D_MODEL), const),
        ],
        out_specs=pl.BlockSpec((FFN_TM, D_MODEL), lambda i: (i, 0)),
        out_shape=jax.ShapeDtypeStruct((t, D_MODEL), F32),
        compiler_params=_params("parallel"),
        name="ffn_ln",
    )(x2d, wg.astype(BF16), wu.astype(BF16), wd.astype(BF16), g.reshape(1, -1), b.reshape(1, -1))


PROJ_TM = 1024
MASK_LANES = HEAD_PAD - HEAD_DIM


def _proj_kernel(x_ref, w_ref, wvt_ref, bg_ref, u_ref, q_ref, kc_ref, vc_ref, ks_ref, kw_ref,
                 gt_ref, vst_ref, vwt_ref):
    xb = x_ref[0].astype(BF16)

    def mm(lo, n):
        return _dot(xb, w_ref[:, lo:lo + n])

    u_ref[0] = mm(_C_U, POOL_WIDTH)
    q_ref[0] = (mm(_C_Q, Q_PAD) * (QK_SCALE * LOG2_E)).astype(BF16)
    kc_ref[0] = mm(_C_KC, KV_PAD)
    vc_ref[0] = mm(_C_VC, KV_PAD)
    tpos = pl.program_id(1) * PROJ_TM + lax.broadcasted_iota(jnp.int32, (PROJ_TM, KV_PAD), 0)
    lane = lax.broadcasted_iota(jnp.int32, (PROJ_TM, KV_PAD), 1) & (HEAD_PAD - 1)
    hot = lane == HEAD_DIM + ((tpos >> SLC_BLOCK_LOG2) & (MASK_LANES - 1))
    ks_ref[0] = jnp.where(hot, 1.0, mm(_C_KS, KV_PAD)).astype(BF16)
    kw_ref[0] = mm(_C_KW, KV_PAD).astype(BF16)
    gt_ref[0] = jax.nn.sigmoid(_dot_nt(wvt_ref[2 * KV_PAD:3 * KV_PAD, :], xb) + bg_ref[...])
    chan = lax.broadcasted_iota(jnp.int32, (KV_PAD, PROJ_TM), 0) & (HEAD_PAD - 1)
    vst_ref[0] = jnp.where(chan == ONES_ROW, 1.0, _dot_nt(wvt_ref[0:KV_PAD, :], xb)).astype(BF16)
    vwt_ref[0] = jnp.where(chan == ONES_ROW, 1.0,
                           _dot_nt(wvt_ref[KV_PAD:2 * KV_PAD, :], xb)).astype(BF16)


def _pad_heads(w, n_heads):
    lead = w.shape[:-1]
    w = w.reshape(lead + (n_heads, HEAD_DIM))
    w = jnp.pad(w, [(0, 0)] * len(lead) + [(0, 0), (0, HEAD_PAD - HEAD_DIM)])
    return w.reshape(lead + (n_heads * HEAD_PAD,))


def _pad_gate_cols(w):
    lead = w.shape[:-1]
    per_group = HEADS_PER_GROUP * N_BRANCHES
    w = w.reshape(lead + (N_KV_GROUPS, per_group))
    w = jnp.pad(w, [(0, 0)] * len(lead) + [(0, 0), (0, HEAD_PAD - per_group)])
    return w.reshape(lead + (KV_PAD,))


def _proj(x, w_in, b_gate):
    bsz, s, _ = x.shape
    cuts = np.cumsum([POOL_WIDTH, N_Q_HEADS * HEAD_DIM] + [N_KV_GROUPS * HEAD_DIM] * 6)
    parts = jnp.split(w_in, [int(c) for c in cuts], axis=-1)
    cols = [parts[0], _pad_heads(parts[1], N_Q_HEADS)]
    cols += [_pad_heads(parts[i], N_KV_GROUPS) for i in (2, 3, 4, 6)]
    w = jnp.concatenate(cols, axis=-1).astype(BF16)
    wvt = jnp.concatenate([_pad_heads(parts[5], N_KV_GROUPS), _pad_heads(parts[7], N_KV_GROUPS),
                           _pad_gate_cols(parts[8])], axis=-1).T.astype(BF16)
    bg = _pad_gate_cols(b_gate).reshape(KV_PAD, 1)

    def tile(width):
        return pl.BlockSpec((1, PROJ_TM, width), lambda b, i: (b, i, 0))

    def out(width, dtype):
        return jax.ShapeDtypeStruct((bsz, s, width), dtype)

    vt_spec = pl.BlockSpec((1, KV_PAD, PROJ_TM), lambda b, i: (b, 0, i))
    vt_out = jax.ShapeDtypeStruct((bsz, KV_PAD, s), BF16)
    return pl.pallas_call(
        _proj_kernel,
        grid=(bsz, s // PROJ_TM),
        in_specs=[
            tile(D_MODEL),
            pl.BlockSpec((D_MODEL, _C_END), lambda b, i: (0, 0), pipeline_mode=pl.Buffered(1)),
            pl.BlockSpec((3 * KV_PAD, D_MODEL), lambda b, i: (0, 0), pipeline_mode=pl.Buffered(1)),
            pl.BlockSpec((KV_PAD, 1), lambda b, i: (0, 0)),
        ],
        out_specs=[tile(POOL_WIDTH), tile(Q_PAD), tile(KV_PAD), tile(KV_PAD), tile(KV_PAD),
                   tile(KV_PAD), vt_spec, vt_spec, vt_spec],
        out_shape=[out(POOL_WIDTH, F32), out(Q_PAD, BF16), out(KV_PAD, F32), out(KV_PAD, F32),
                   out(KV_PAD, BF16), out(KV_PAD, BF16),
                   jax.ShapeDtypeStruct((bsz, KV_PAD, s), F32), vt_out, vt_out],
        compiler_params=_params("parallel", "parallel"),
        name="in_proj",
    )(x, w, wvt, bg)


def _gelu_tanh(x):
    c = np.float32(np.sqrt(2.0 / np.pi))
    return x * (0.5 * (1.0 + jnp.tanh(c * (x + 0.044715 * (x * x * x)))))


def _compress_kernel(x_ref, plo_ref, phi_ref, wlo_ref, whi_ref, w2_ref, o_ref, a_scr, b_scr,
                     *, channel_major):
    r = pl.program_id(1)
    nsb = x_ref.shape[1]
    xr = x_ref[0]
    a_scr[r] = _dot((xr + plo_ref[...]).astype(BF16), wlo_ref[...])
    b_scr[r, 0:nsb, :] = _dot((xr + phi_ref[...]).astype(BF16), whi_ref[...])

    @pl.when(r == 0)
    def _():
        b_scr[0, nsb:nsb + 8, :] = jnp.zeros((8, KV_PAD), F32)

    @pl.when(r == CMP_PER_SLC - 1)
    def _():
        for rr in range(CMP_PER_SLC):
            if rr < CMP_PER_SLC - 1:
                h = a_scr[rr] + b_scr[rr + 1, 0:nsb, :]
            else:
                h = a_scr[rr] + b_scr[0, 1:nsb + 1, :]
            act = _gelu_tanh(h).astype(BF16)
            if channel_major:
                o_ref[0, :, rr * nsb:(rr + 1) * nsb] = _dot_nt(w2_ref[...], act).astype(BF16)
            else:
                o_ref[0, rr * nsb:(rr + 1) * nsb, :] = _dot(act, w2_ref[...]).astype(BF16)


def _compress_weights(pos, w1, w2):
    eye_g = jnp.eye(N_KV_GROUPS, dtype=F32)
    w1r = w1.reshape(CMP_BLOCK, HEAD_DIM, HEAD_DIM)
    w1r = jnp.pad(w1r, ((0, 0), (0, HEAD_PAD - HEAD_DIM), (0, HEAD_PAD - HEAD_DIM)))
    w1c = w1r[:, None, :, None, :] * eye_g[None, :, None, :, None]
    w1c = w1c.reshape(CMP_BLOCK, KV_PAD, KV_PAD)
    wlo = w1c[:CMP_STRIDE].reshape(CHUNK_FLAT, KV_PAD).astype(BF16)
    whi = w1c[CMP_STRIDE:].reshape(CHUNK_FLAT, KV_PAD).astype(BF16)
    posp = jnp.pad(pos, ((0, 0), (0, HEAD_PAD - HEAD_DIM)))
    posp = jnp.tile(posp[:, None, :], (1, N_KV_GROUPS, 1))
    plo = posp[:CMP_STRIDE].reshape(1, CHUNK_FLAT)
    phi = posp[CMP_STRIDE:].reshape(1, CHUNK_FLAT)
    w2p = jnp.pad(w2, ((0, HEAD_PAD - HEAD_DIM), (0, HEAD_PAD - HEAD_DIM)))
    w2c = (w2p[None, :, None, :] * eye_g[:, None, :, None]).reshape(KV_PAD, KV_PAD).astype(BF16)
    return plo, phi, wlo, whi, w2c


def _compress(kv, pos, w1, w2, channel_major):
    bsz, s, _ = kv.shape
    nsb = s // SLC_BLOCK
    ncp = CMP_PER_SLC * nsb
    plo, phi, wlo, whi, w2c = _compress_weights(pos, w1, w2)
    if channel_major:
        w2c = w2c.T
    out_dims = (KV_PAD, ncp) if channel_major else (ncp, KV_PAD)
    x = kv.reshape(bsz, nsb, CMP_PER_SLC * CHUNK_FLAT)
    const = lambda b, r: (0, 0)
    return pl.pallas_call(
        functools.partial(_compress_kernel, channel_major=channel_major),
        grid=(bsz, CMP_PER_SLC),
        in_specs=[
            pl.BlockSpec((1, nsb, CHUNK_FLAT), lambda b, r: (b, 0, r)),
            pl.BlockSpec((1, CHUNK_FLAT), const),
            pl.BlockSpec((1, CHUNK_FLAT), const),
            pl.BlockSpec((CHUNK_FLAT, KV_PAD), const),
            pl.BlockSpec((CHUNK_FLAT, KV_PAD), const),
            pl.BlockSpec((KV_PAD, KV_PAD), const),
        ],
        out_specs=pl.BlockSpec((1,) + out_dims, lambda b, r: (b, 0, 0)),
        out_shape=jax.ShapeDtypeStruct((bsz,) + out_dims, BF16),
        scratch_shapes=[pltpu.VMEM((CMP_PER_SLC, nsb, KV_PAD), F32),
                        pltpu.VMEM((CMP_PER_SLC, nsb + 8, KV_PAD), F32)],
        compiler_params=_params("parallel", "arbitrary"),
        name="compress",
    )(x, plo, phi, wlo, whi, w2c)


TQ = 1024
SLC_TQ = 512


def _load_q(q_ref):
    return jnp.concatenate(
        [q_ref[0, :, h * HEAD_PAD:(h + 1) * HEAD_PAD] for h in range(HEADS_PER_GROUP)], axis=0)


def _per_head(row):
    return jnp.concatenate([row] * HEADS_PER_GROUP, axis=1)


def _store_gated(o_ref, g_ref, o_t, norm, branch, cols=None):
    cols = slice(None) if cols is None else cols
    gate = jnp.concatenate([g_ref[0, h * N_BRANCHES + branch:h * N_BRANCHES + branch + 1, cols]
                            for h in range(HEADS_PER_GROUP)], axis=1)
    if norm is not None:
        gate = gate / jnp.maximum(norm, 1e-30)
    o = (o_t * gate).astype(o_ref.dtype)
    tq = o.shape[1] // HEADS_PER_GROUP
    for h in range(HEADS_PER_GROUP):
        o_ref[0, h, :, cols] = o[:, h * tq:(h + 1) * tq]


def _attn_specs(tq=TQ):
    q_spec = pl.BlockSpec((1, tq, GROUP_Q), lambda b, g, i: (b, i, g))
    g_spec = pl.BlockSpec((1, HEAD_PAD, tq), lambda b, g, i: (b, g, i))
    o_spec = pl.BlockSpec((1, HEADS_PER_GROUP, HEAD_PAD, tq), lambda b, g, i: (b, g, 0, i))
    return q_spec, g_spec, o_spec


def _branch_out_shape(bsz, s):
    return jax.ShapeDtypeStruct((bsz, N_Q_HEADS, HEAD_PAD, s), BF16)


def _cmp_kernel(q_ref, kc_ref, vct_ref, g_ref, cend_ref, place_ref, o_ref, selq_ref):
    nsb = kc_ref.shape[1] // CMP_PER_SLC
    t0 = pl.program_id(2) * TQ
    half = nsb // 2
    if half % LANES == 0:
        @pl.when(t0 + TQ <= half * SLC_BLOCK)
        def _():
            _cmp_body(q_ref, kc_ref, vct_ref, g_ref, cend_ref, place_ref, o_ref, selq_ref, half)

        @pl.when(t0 + TQ > half * SLC_BLOCK)
        def _():
            _cmp_body(q_ref, kc_ref, vct_ref, g_ref, cend_ref, place_ref, o_ref, selq_ref, nsb)
    else:
        _cmp_body(q_ref, kc_ref, vct_ref, g_ref, cend_ref, place_ref, o_ref, selq_ref, nsb)


def _cmp_body(q_ref, kc_ref, vct_ref, g_ref, cend_ref, place_ref, o_ref, selq_ref, nvis):
    nsb_all = kc_ref.shape[1] // CMP_PER_SLC
    nsb = nvis
    t0 = pl.program_id(2) * TQ

    def slabs(ref_rows):
        if nvis == nsb_all:
            return ref_rows(0, CMP_PER_SLC * nsb_all)
        return jnp.concatenate([ref_rows(r * nsb_all, nvis) for r in range(CMP_PER_SLC)], axis=0)

    kc = slabs(lambda lo, n: kc_ref[0, lo:lo + n, :])
    cend = slabs(lambda lo, n: cend_ref[lo:lo + n, :])
    if nvis == nsb_all:
        vct = vct_ref[0]
    else:
        vct = jnp.concatenate([vct_ref[0, :, r * nsb_all:r * nsb_all + nvis]
                               for r in range(CMP_PER_SLC)], axis=1)

    s = _dot_nt(kc, _load_q(q_ref))
    s = s + _per_head(jnp.where(cend <= t0, 0.0, NEG_BIG))
    m = jnp.max(s, axis=0, keepdims=True)
    e = jnp.exp2(s - m)
    l = jnp.sum(e, axis=0, keepdims=True)
    tcol = t0 + lax.broadcasted_iota(jnp.int32, (1, TQ), 1)
    any_visible = _per_head(jnp.where(tcol >= CMP_BLOCK - 1, 1.0, 0.0))
    p = e * (any_visible / jnp.maximum(l, 1e-30))

    o_t = _dot(vct, p.astype(BF16))
    _store_gated(o_ref, g_ref, o_t, None, 0)

    imp = p[:, 0:TQ]
    for h in range(1, HEADS_PER_GROUP):
        imp = imp + p[:, h * TQ:(h + 1) * TQ]
    p0, p1, p2, p3 = (imp[r * nsb:(r + 1) * nsb, :] for r in range(CMP_PER_SLC))
    blk = lax.broadcasted_iota(jnp.int32, (nsb, TQ), 0)
    p3_prev = jnp.where(blk == 0, 0.0, pltpu.roll(p3, 1, axis=0))
    imp_slc = 0.5 * p3_prev + p0 + p1 + p2 + 0.5 * p3

    jt = (t0 + lax.broadcasted_iota(jnp.int32, (nsb, TQ), 1)) >> SLC_BLOCK_LOG2
    forced = (blk == 0) | (blk == jt) | (blk == jt - 1)
    free = (blk <= jt) & jnp.logical_not(forced)
    score = jnp.where(free, imp_slc, NEG_BIG)

    def pick(_, sc):
        mx = jnp.max(sc, axis=0, keepdims=True)
        first = jnp.min(jnp.where(sc == mx, blk, nsb), axis=0, keepdims=True)
        return jnp.where(blk == first, -jnp.inf, sc)

    picked = lax.fori_loop(0, min(N_SELECT - N_FORCED, nsb), pick, score, unroll=True) == -jnp.inf
    selected = (picked & free) | forced
    bias = jnp.where(selected, 0.0, NEG_BIG)
    if nvis < nsb_all:
        bias = jnp.concatenate([bias, jnp.full((nsb_all - nvis, TQ), NEG_BIG, F32)], axis=0)
    selq_ref[0, 0] = _dot(bias.T.astype(BF16), place_ref[...]).astype(BF16)


def _mask_lane_placement(nsb):
    j = np.arange(nsb)
    place = np.zeros((nsb, (nsb // MASK_LANES) * HEAD_PAD), np.float32)
    place[j, (j // MASK_LANES) * HEAD_PAD + HEAD_DIM + j % MASK_LANES] = 1.0
    return jnp.asarray(place, BF16)


def _cmp_end_minus_token(nsb):
    row = np.arange(CMP_PER_SLC * nsb)
    end = (row % nsb) * SLC_BLOCK + (row // nsb) * CMP_STRIDE + CMP_BLOCK - 1
    return jnp.asarray(end[:, None] - np.arange(TQ)[None, :], jnp.int32)


def _cmp_attention(q, kcmp, vcmp_t, gates):
    bsz, s, _ = q.shape
    nsb = s // SLC_BLOCK
    ncp = CMP_PER_SLC * nsb
    selq_w = (nsb // MASK_LANES) * HEAD_PAD
    q_spec, g_spec, o_spec = _attn_specs()
    const = lambda b, g, i: (0, 0)
    return pl.pallas_call(
        _cmp_kernel,
        grid=(bsz, N_KV_GROUPS, s // TQ),
        in_specs=[q_spec,
                  pl.BlockSpec((1, ncp, HEAD_PAD), lambda b, g, i: (b, 0, g)),
                  pl.BlockSpec((1, HEAD_PAD, ncp), lambda b, g, i: (b, g, 0)),
                  g_spec,
                  pl.BlockSpec((ncp, TQ), const),
                  pl.BlockSpec((nsb, selq_w), const)],
        out_specs=[o_spec, pl.BlockSpec((1, 1, TQ, selq_w), lambda b, g, i: (b, g, i, 0))],
        out_shape=[_branch_out_shape(bsz, s),
                   jax.ShapeDtypeStruct((bsz, N_KV_GROUPS, s, selq_w), BF16)],
        compiler_params=_params("parallel", "parallel", "arbitrary"),
        name="cmp_attn_topk",
    )(q, kcmp, vcmp_t, gates, _cmp_end_minus_token(nsb), _mask_lane_placement(nsb))


SLC_TK = 512
SUPER_KEYS = MASK_LANES * SLC_BLOCK
SLC_STEP = SUPER_KEYS
SLC_PV = 256
assert SLC_TQ == SLC_TK
assert SUPER_KEYS % SLC_STEP == 0 and SLC_STEP % SLC_TK == 0


def _slc_kernel(q_ref, k_ref, vt_ref, selq_ref, g_ref, o_ref, s_scr):
    cols = HEADS_PER_GROUP * SLC_TQ
    t0 = pl.multiple_of(pl.program_id(2) * SLC_TQ, SLC_TQ)
    n_full = t0 // SLC_STEP
    tail_tiles = (t0 - n_full * SLC_STEP) // SLC_TK

    def masked_q(slab):
        return jnp.concatenate([q_ref[0, :, h * HEAD_PAD:(h + 1) * HEAD_PAD] + slab
                                for h in range(HEADS_PER_GROUP)], axis=0)

    def accumulate(k0, n_keys, m, m_new, acc):
        acc = jnp.exp2(m - m_new) * acc
        for c in range(0, n_keys, SLC_PV):
            p = jnp.exp2(s_scr[c:c + SLC_PV, :] - m_new).astype(BF16)
            acc = acc + _dot(vt_ref[0, :, pl.ds(k0 + c, SLC_PV)], p)
        return m_new, acc

    def step_q(step):
        lane0 = pl.multiple_of((step * SLC_STEP // SUPER_KEYS) * HEAD_PAD, HEAD_PAD)
        return masked_q(selq_ref[0, 0, :, pl.ds(lane0, HEAD_PAD)])

    def sweep(st, carry):
        m, acc = carry
        k0 = pl.multiple_of(st * SLC_STEP, SLC_STEP)
        sc = _dot_nt(k_ref[0, pl.ds(k0, SLC_STEP), :], step_q(st))
        s_scr[...] = sc
        m_new = jnp.maximum(m, jnp.max(sc, axis=0, keepdims=True))
        return accumulate(k0, SLC_STEP, m, m_new, acc)

    def last_step(n_before, carry):
        m, acc = carry
        k0 = pl.multiple_of(n_full * SLC_STEP, SLC_STEP)
        n_keys = n_before + SLC_TQ
        sc = _dot_nt(k_ref[0, pl.ds(k0, n_keys), :], step_q(n_full))
        kk = lax.broadcasted_iota(jnp.int32, (SLC_TQ, SLC_TQ), 0)
        tt = lax.broadcasted_iota(jnp.int32, (SLC_TQ, SLC_TQ), 1)
        own = sc[n_before:, :] + _per_head(jnp.where(kk <= tt, 0.0, NEG_BIG))
        s_scr[n_before:n_keys, :] = own
        m_new = jnp.maximum(m, jnp.max(own, axis=0, keepdims=True))
        if n_before:
            s_scr[0:n_before, :] = sc[:n_before, :]
            m_new = jnp.maximum(m_new, jnp.max(sc[:n_before, :], axis=0, keepdims=True))
        return accumulate(k0, n_keys, m, m_new, acc)

    carry = (jnp.full((1, cols), NEG_BIG, F32), jnp.zeros((HEAD_PAD, cols), F32))
    carry = lax.fori_loop(0, n_full, sweep, carry)
    _, acc = lax.switch(tail_tiles, [functools.partial(last_step, n * SLC_TK)
                                     for n in range(SLC_STEP // SLC_TK)], carry)
    _store_gated(o_ref, g_ref, acc, acc[ONES_ROW:ONES_ROW + 1, :], 1)


def _slc_attention(q, ks, vs_t, selq, gates):
    bsz, s, _ = q.shape
    n_super = s // SUPER_KEYS
    q_spec, g_spec, o_spec = _attn_specs(SLC_TQ)
    return pl.pallas_call(
        _slc_kernel,
        grid=(bsz, N_KV_GROUPS, s // SLC_TQ),
        in_specs=[q_spec,
                  pl.BlockSpec((1, s, HEAD_PAD), lambda b, g, i: (b, 0, g)),
                  pl.BlockSpec((1, HEAD_PAD, s), lambda b, g, i: (b, g, 0)),
                  pl.BlockSpec((1, 1, SLC_TQ, n_super * HEAD_PAD), lambda b, g, i: (b, g, i, 0)),
                  g_spec],
        out_specs=o_spec,
        out_shape=_branch_out_shape(bsz, s),
        scratch_shapes=[pltpu.VMEM((SLC_STEP, HEADS_PER_GROUP * SLC_TQ), F32)],
        compiler_params=_params("parallel", "parallel", "arbitrary"),
        name="slc_attn",
    )(q, ks, vs_t, selq, gates)


WIN_TQ = 1024
WIN_SUB = 256
WIN_KEYS = WINDOW + WIN_SUB


def _win_kernel(q_ref, k_ref, vt_ref, g_ref, o_ref):
    for sub in range(WIN_TQ // WIN_SUB):
        rows = slice(sub * WIN_SUB, (sub + 1) * WIN_SUB)
        t0 = pl.program_id(2) * WIN_TQ + sub * WIN_SUB
        k0 = pl.multiple_of(jnp.maximum(t0 - WINDOW, 0), WIN_SUB)
        q = jnp.concatenate([q_ref[0, rows, h * HEAD_PAD:(h + 1) * HEAD_PAD]
                             for h in range(HEADS_PER_GROUP)], axis=0)
        s = _dot_nt(k_ref[0, pl.ds(k0, WIN_KEYS), :], q)
        diff = ((t0 + lax.broadcasted_iota(jnp.int32, (WIN_KEYS, WIN_SUB), 1))
                - (k0 + lax.broadcasted_iota(jnp.int32, (WIN_KEYS, WIN_SUB), 0)))
        s = s + _per_head(jnp.where((diff >= 0) & (diff < WINDOW), 0.0, NEG_BIG))
        m = jnp.max(s, axis=0, keepdims=True)
        e = jnp.exp2(s - m).astype(BF16)
        o_t = _dot(vt_ref[0, :, pl.ds(k0, WIN_KEYS)], e)
        _store_gated(o_ref, g_ref, o_t, o_t[ONES_ROW:ONES_ROW + 1, :], 2, cols=rows)


def _win_attention(q, kw, vw_t, gates):
    bsz, s, _ = q.shape
    q_spec, g_spec, o_spec = _attn_specs(WIN_TQ)
    return pl.pallas_call(
        _win_kernel,
        grid=(bsz, N_KV_GROUPS, s // WIN_TQ),
        in_specs=[q_spec,
                  pl.BlockSpec((1, s, HEAD_PAD), lambda b, g, i: (b, 0, g)),
                  pl.BlockSpec((1, HEAD_PAD, s), lambda b, g, i: (b, g, 0)),
                  g_spec],
        out_specs=o_spec,
        out_shape=_branch_out_shape(bsz, s),
        compiler_params=_params("parallel", "parallel", "arbitrary"),
        name="win_attn",
    )(q, kw, vw_t, gates)


MIX_TM = 1024
POOL_HALO = 16


def _mix_out_kernel(x_ref, u_ref, halo_ref, oc_ref, os_ref, ow_ref, pw_ref, ps_ref,
                    wo_ref, g_ref, b_ref, o_ref, ext_scr):
    i = pl.program_id(1)
    u = u_ref[0]
    halo = jnp.where(i == 0, 0.0, halo_ref[0])
    ext_scr[0:POOL_HALO, :] = halo
    ext_scr[POOL_HALO:POOL_HALO + MIX_TM, :] = u

    lane = lax.broadcasted_iota(jnp.int32, (MIX_TM, POOL_WIDTH), 1)
    tpos = i * MIX_TM + lax.broadcasted_iota(jnp.int32, (MIX_TM, POOL_WIDTH), 0)
    grp = lane >> (POOL_GROUP_DIM.bit_length() - 1)
    run = u
    win_sum = jnp.zeros_like(u)
    cnt = jnp.zeros_like(u)
    done = 1
    for gidx, w in enumerate(POOL_WINDOWS):
        for kback in range(done, w):
            run = run + ext_scr[POOL_HALO - kback:POOL_HALO - kback + MIX_TM, :]
        done = w
        win_sum = jnp.where(grp == gidx, run, win_sum)
        cnt = jnp.where(grp == gidx, jnp.minimum(tpos + 1, w).astype(F32), cnt)
    pooled = win_sum / cnt - u
    mixed = _dot(pooled.astype(BF16), pw_ref[...]) * ps_ref[...]

    y = _dot(mixed.astype(BF16), wo_ref[0:POOL_WIDTH, :])
    for h in range(0, N_Q_HEADS, 2):
        pair = [(oc_ref[0, hh].astype(F32) + os_ref[0, hh].astype(F32)
                 + ow_ref[0, hh].astype(F32)).T.astype(BF16) for hh in (h, h + 1)]
        lo = POOL_WIDTH + h * HEAD_PAD
        y = y + _dot(jnp.concatenate(pair, axis=1), wo_ref[lo:lo + 2 * HEAD_PAD, :])
    z = ALPHA * x_ref[0] + y
    o_ref[0] = _layer_norm(z, g_ref[...], b_ref[...])


def _mix_out(x, u, o_cmp, o_slc, o_win, pool_w, pool_scale, w_out, g, b):
    bsz, s, _ = x.shape
    n_grp = len(POOL_WINDOWS)
    eye = jnp.eye(n_grp, dtype=F32)
    pw = (pool_w[:, :, None, :] * eye[:, None, :, None]).reshape(POOL_WIDTH, POOL_WIDTH).astype(BF16)
    wo_nsa = w_out[POOL_WIDTH:].reshape(N_Q_HEADS, HEAD_DIM, D_MODEL)
    wo_nsa = jnp.pad(wo_nsa, ((0, 0), (0, HEAD_PAD - HEAD_DIM), (0, 0))).reshape(Q_PAD, D_MODEL)
    wo = jnp.concatenate([w_out[:POOL_WIDTH], wo_nsa], axis=0).astype(BF16)

    def tile(width):
        return pl.BlockSpec((1, MIX_TM, width), lambda bb, i: (bb, i, 0))

    branch = pl.BlockSpec((1, N_Q_HEADS, HEAD_PAD, MIX_TM), lambda bb, i: (bb, 0, 0, i))
    halo_blocks = MIX_TM // POOL_HALO
    const = lambda bb, i: (0, 0)
    return pl.pallas_call(
        _mix_out_kernel,
        grid=(bsz, s // MIX_TM),
        in_specs=[
            tile(D_MODEL), tile(POOL_WIDTH),
            pl.BlockSpec((1, POOL_HALO, POOL_WIDTH),
                         lambda bb, i: (bb, jnp.maximum(i * halo_blocks - 1, 0), 0)),
            branch, branch, branch,
            pl.BlockSpec((POOL_WIDTH, POOL_WIDTH), const),
            pl.BlockSpec((1, POOL_WIDTH), const),
            pl.BlockSpec((POOL_WIDTH + Q_PAD, D_MODEL), const),
            pl.BlockSpec((1, D_MODEL), const),
            pl.BlockSpec((1, D_MODEL), const),
        ],
        out_specs=tile(D_MODEL),
        out_shape=jax.ShapeDtypeStruct((bsz, s, D_MODEL), F32),
        scratch_shapes=[pltpu.VMEM((POOL_HALO + MIX_TM, POOL_WIDTH), F32)],
        compiler_params=_params("parallel", "arbitrary"),
        name="mix_out_ln",
    )(x, u, u, o_cmp, o_slc, o_win, pw, pool_scale.reshape(1, -1), wo,
      g.reshape(1, -1), b.reshape(1, -1))


def kernel(x, ln1_g, ln1_b, ffn1_w_gate, ffn1_w_up, ffn1_w_down, w_in, b_gate, pool_w, pool_scale, cmp_pos_k, cmp_k_w1, cmp_k_w2, cmp_pos_v, cmp_v_w1, cmp_v_w2, w_out, ln2_g, ln2_b, ffn2_w_gate, ffn2_w_up, ffn2_w_down, ln3_g, ln3_b):
    bsz, s, d = x.shape
    assert d == D_MODEL and s % max(SUPER_KEYS, FFN_TM, MIX_TM, PROJ_TM) == 0
    for l in range(DEPTH):
        x = _ffn_ln(x.reshape(bsz * s, d), ffn1_w_gate[l], ffn1_w_up[l], ffn1_w_down[l],
                    ln1_g[l], ln1_b[l]).reshape(bsz, s, d)
        u, q, kc, vc, ks, kw, gates, vs_t, vw_t = _proj(x, w_in[l], b_gate[l])
        kcmp = _compress(kc, cmp_pos_k[l], cmp_k_w1[l], cmp_k_w2[l], channel_major=False)
        vcmp_t = _compress(vc, cmp_pos_v[l], cmp_v_w1[l], cmp_v_w2[l], channel_major=True)
        o_cmp, selq = _cmp_attention(q, kcmp, vcmp_t, gates)
        o_slc = _slc_attention(q, ks, vs_t, selq, gates)
        o_win = _win_attention(q, kw, vw_t, gates)
        x = _mix_out(x, u, o_cmp, o_slc, o_win, pool_w[l], pool_scale[l], w_out[l],
                     ln2_g[l], ln2_b[l])
        x = _ffn_ln(x.reshape(bsz * s, d), ffn2_w_gate[l], ffn2_w_up[l], ffn2_w_down[l],
                    ln3_g[l], ln3_b[l]).reshape(bsz, s, d)
    return x
```
